```python
import jax
import jax.numpy as jnp
from jax import lax
import numpy as np

D_MODEL = 1024
BATCH = 8
SEQ = 8192
DEPTH = 1

A_HEADS = 8
A_HEAD_DIM = 64
IDX_HEADS = 8
IDX_DIM = 64
TOPK_MAX = 256
B_HEADS = 8
B_NOPE = 64
B_ROPE = 32
B_V = 64
Q_LORA = 256
KV_LORA = 128
D_MIX = A_HEADS * A_HEAD_DIM + B_HEADS * B_V
IN_SIZES = (A_HEADS * A_HEAD_DIM, A_HEAD_DIM, A_HEAD_DIM, IDX_HEADS * IDX_DIM, IDX_DIM, IDX_HEADS, Q_LORA, KV_LORA, B_ROPE)
D_IN = sum(IN_SIZES)
N_EXPERTS = 32
TOP_K = 4
D_FF = 1024
SWIGLU_LIMIT = 7.0
SWIGLU_ALPHA = 1.702
ROPE_THETA = 10000.0
Q_BLOCK = 128
LN_EPS = 1e-5
RMS_EPS = 1e-6
NEG_INF = -1e30
DN_ALPHA = (2 * DEPTH) ** 0.25
DN_BETA = (8 * DEPTH) ** -0.25

kernel_name = 'hybrid_dsa_mla_moe_deepnorm'


def layer_norm(x, g, b):
    xf = x.astype(jnp.float32)
    mu = jnp.mean(xf, axis=-1, keepdims=True)
    var = jnp.mean(jnp.square(xf - mu), axis=-1, keepdims=True)
    y = (xf - mu) * lax.rsqrt(var + LN_EPS) * g.astype(jnp.float32) + b.astype(jnp.float32)
    return y.astype(x.dtype)


def rms_norm(x, g):
    xf = x.astype(jnp.float32)
    y = xf * lax.rsqrt(jnp.mean(jnp.square(xf), axis=-1, keepdims=True) + RMS_EPS) * g.astype(jnp.float32)
    return y.astype(x.dtype)


def rope_cos_sin(positions, dim):
    inv_freq = ROPE_THETA ** (-jnp.arange(0, dim, 2, dtype=jnp.float32) / dim)
    ang = positions.astype(jnp.float32)[..., None] * inv_freq
    return jnp.cos(ang), jnp.sin(ang)


def apply_rope(x, cos, sin):
    x1, x2 = jnp.split(x.astype(jnp.float32), 2, axis=-1)
    return jnp.concatenate([x1 * cos - x2 * sin, x2 * cos + x1 * sin], axis=-1).astype(x.dtype)


def causal_mask(q0, q1):
    return jnp.arange(q1)[None, :] <= jnp.arange(q0, q1)[:, None]


def dsa_attention(q, kv, iq, ik, iw):
    seq = q.shape[1]
    k_sel = min(TOPK_MAX, seq // 4)
    scale = A_HEAD_DIM ** -0.5
    take = jax.vmap(lambda kv_b, idx_b: kv_b[idx_b])
    outs = []
    for q0 in range(0, seq, Q_BLOCK):
        q1 = q0 + Q_BLOCK
        t_abs = jnp.arange(q0, q1)
        rel = jax.nn.relu(jnp.einsum('bthd,bsd->bths', iq[:, q0:q1], ik[:, :q1]).astype(jnp.float32))
        score = jnp.einsum('bths,bth->bts', rel, iw[:, q0:q1].astype(jnp.float32))
        score = jnp.where(causal_mask(q0, q1)[None], score, -jnp.inf)
        _, sel = lax.top_k(score, min(k_sel, q1))
        valid = sel <= t_abs[None, :, None]
        k_g, v_g = jnp.split(take(kv[:, :q1], sel), 2, axis=-1)
        logits = jnp.einsum('bthd,btkd->bthk', q[:, q0:q1], k_g).astype(jnp.float32) * scale
        logits = jnp.where(valid[:, :, None, :], logits, NEG_INF)
        p = jax.nn.softmax(logits, axis=-1).astype(v_g.dtype)
        outs.append(jnp.einsum('bthk,btkd->bthd', p, v_g))
    return jnp.concatenate(outs, axis=1)


def mla_attention(q_nope, q_rope, k_nope, k_rope, v):
    seq = q_nope.shape[1]
    scale = (B_NOPE + B_ROPE) ** -0.5
    outs = []
    for q0 in range(0, seq, Q_BLOCK):
        q1 = q0 + Q_BLOCK
        logits = (jnp.einsum('bthd,bshd->bhts', q_nope[:, q0:q1], k_nope[:, :q1])
                  + jnp.einsum('bthr,bsr->bhts', q_rope[:, q0:q1], k_rope[:, :q1])).astype(jnp.float32) * scale
        logits = jnp.where(causal_mask(q0, q1)[None, None], logits, NEG_INF)
        p = jax.nn.softmax(logits, axis=-1).astype(v.dtype)
        outs.append(jnp.einsum('bhts,bshd->bthd', p, v[:, :q1]))
    return jnp.concatenate(outs, axis=1)


def moe_ffn(h, w_router, b_router, w_gate, b_gate, w_up, b_up, w_down, b_down):
    bsz, seq, d = h.shape
    hf = h.reshape(bsz * seq, d)
    logits = jnp.einsum('nd,de->ne', hf, w_router).astype(jnp.float32) + b_router.astype(jnp.float32)
    top_v, top_i = lax.top_k(logits, TOP_K)
    gates = jax.nn.softmax(top_v, axis=-1)
    comb = jnp.einsum('nk,nke->ne', gates, jax.nn.one_hot(top_i, N_EXPERTS, dtype=jnp.float32)).astype(h.dtype)
    out = jnp.zeros_like(hf)
    for e in range(N_EXPERTS):
        g = jnp.minimum(hf @ w_gate[e] + b_gate[e], SWIGLU_LIMIT)
        u = jnp.clip(hf @ w_up[e] + b_up[e], -SWIGLU_LIMIT, SWIGLU_LIMIT)
        act = g * jax.nn.sigmoid(SWIGLU_ALPHA * g) * (u + 1.0)
        out = out + comb[:, e:e + 1] * (act @ w_down[e] + b_down[e])
    return out.reshape(bsz, seq, d)


def hybrid_layer(x, rope_a, rope_i, rope_r, w_in, ik_g, ik_b, q_norm_g, w_q_up, kv_norm_g, w_kv_up,
                 w_out, ln1_g, ln1_b, w_router, b_router, w_gate, b_gate, w_up, b_up, w_down, b_down,
                 ln2_g, ln2_b):
    bsz, seq, _ = x.shape
    cos_a, sin_a = rope_a
    cos_i, sin_i = rope_i
    cos_r, sin_r = rope_r
    offsets = [int(o) for o in np.cumsum(IN_SIZES)[:-1]]
    proj = jnp.einsum('bsd,de->bse', x, w_in)
    a_q, a_k, a_v, i_q, i_k, i_w, b_qc, b_kvc, b_kr = jnp.split(proj, offsets, axis=-1)

    a_q = apply_rope(a_q.reshape(bsz, seq, A_HEADS, A_HEAD_DIM), cos_a[:, :, None], sin_a[:, :, None])
    a_kv = jnp.concatenate([apply_rope(a_k, cos_a, sin_a), a_v], axis=-1)
    i_q = apply_rope(i_q.reshape(bsz, seq, IDX_HEADS, IDX_DIM), cos_i[:, :, None], sin_i[:, :, None])
    i_k = apply_rope(layer_norm(i_k, ik_g, ik_b), cos_i, sin_i)
    i_w = i_w * (IDX_HEADS * IDX_DIM) ** -0.5
    a_out = dsa_attention(a_q, a_kv, i_q, i_k, i_w)

    q = jnp.einsum('bsr,re->bse', rms_norm(b_qc, q_norm_g), w_q_up).reshape(bsz, seq, B_HEADS, B_NOPE + B_ROPE)
    q_nope = q[..., :B_NOPE]
    q_rope = apply_rope(q[..., B_NOPE:], cos_r[:, :, None], sin_r[:, :, None])
    kvu = jnp.einsum('bsr,re->bse', rms_norm(b_kvc, kv_norm_g), w_kv_up).reshape(bsz, seq, B_HEADS, B_NOPE + B_V)
    k_nope, v = kvu[..., :B_NOPE], kvu[..., B_NOPE:]
    k_rope = apply_rope(b_kr, cos_r, sin_r)
    b_out = mla_attention(q_nope, q_rope, k_nope, k_rope, v)

    heads = jnp.concatenate([a_out.reshape(bsz, seq, -1), b_out.reshape(bsz, seq, -1)], axis=-1)
    mix = jnp.einsum('bse,ed->bsd', heads, w_out)
    h = layer_norm(DN_ALPHA * x + mix, ln1_g, ln1_b)
    ffn = moe_ffn(h, w_router, b_router, w_gate, b_gate, w_up, b_up, w_down, b_down)
    return layer_norm(DN_ALPHA * h + ffn, ln2_g, ln2_b)


def setup_inputs(seed: int = 0) -> dict:
    key = jax.random.key(seed)
    ks = jax.random.split(key, 24)
    f32 = jnp.float32

    def nrm(k, shape, scale):
        return jax.random.normal(k, shape, f32) * scale

    x = jax.random.normal(ks[0], (BATCH, SEQ, D_MODEL), f32)
    positions = (jax.random.randint(ks[1], (BATCH, 1), 0, 4096, dtype=jnp.int32)
                 + jnp.arange(SEQ, dtype=jnp.int32)[None, :])
    col_scale = jnp.concatenate([jnp.ones((IN_SIZES[0] + IN_SIZES[1],), f32),
                                 jnp.full((IN_SIZES[2],), DN_BETA, f32),
                                 jnp.ones((sum(IN_SIZES[3:]),), f32)])
    w_mix_in = nrm(ks[2], (DEPTH, D_MODEL, D_IN), D_MODEL ** -0.5) * col_scale
    idx_k_norm_g = 1.0 + nrm(ks[3], (DEPTH, IDX_DIM), 0.02)
    idx_k_norm_b = nrm(ks[4], (DEPTH, IDX_DIM), 0.02)
    mla_q_norm_g = 1.0 + nrm(ks[5], (DEPTH, Q_LORA), 0.02)
    mla_w_q_up = nrm(ks[6], (DEPTH, Q_LORA, B_HEADS * (B_NOPE + B_ROPE)), Q_LORA ** -0.5)
    mla_kv_norm_g = 1.0 + nrm(ks[7], (DEPTH, KV_LORA), 0.02)
    kv_scale = jnp.concatenate([jnp.ones((B_NOPE,), f32), jnp.full((B_V,), DN_BETA, f32)])
    mla_w_kv_up = (nrm(ks[8], (DEPTH, KV_LORA, B_HEADS, B_NOPE + B_V), KV_LORA ** -0.5) * kv_scale
                   ).reshape(DEPTH, KV_LORA, B_HEADS * (B_NOPE + B_V))
    w_mix_out = nrm(ks[9], (DEPTH, D_MIX, D_MODEL), D_MIX ** -0.5 * DN_BETA)
    ln1_g = 1.0 + nrm(ks[10], (DEPTH, D_MODEL), 0.02)
    ln1_b = nrm(ks[11], (DEPTH, D_MODEL), 0.02)
    w_router = nrm(ks[12], (DEPTH, D_MODEL, N_EXPERTS), D_MODEL ** -0.5)
    b_router = nrm(ks[13], (DEPTH, N_EXPERTS), 0.01)
    w_gate = nrm(ks[14], (DEPTH, N_EXPERTS, D_MODEL, D_FF), D_MODEL ** -0.5 * DN_BETA)
    b_gate = nrm(ks[15], (DEPTH, N_EXPERTS, D_FF), 0.01)
    w_up = nrm(ks[16], (DEPTH, N_EXPERTS, D_MODEL, D_FF), D_MODEL ** -0.5 * DN_BETA)
    b_up = nrm(ks[17], (DEPTH, N_EXPERTS, D_FF), 0.01)
    w_down = nrm(ks[18], (DEPTH, N_EXPERTS, D_FF, D_MODEL), D_FF ** -0.5 * DN_BETA)
    b_down = nrm(ks[19], (DEPTH, N_EXPERTS, D_MODEL), 0.01)
    ln2_g = 1.0 + nrm(ks[20], (DEPTH, D_MODEL), 0.02)
    ln2_b = nrm(ks[21], (DEPTH, D_MODEL), 0.02)
    return {'x': x, 'positions': positions, 'w_mix_in': w_mix_in,
            'idx_k_norm_g': idx_k_norm_g, 'idx_k_norm_b': idx_k_norm_b,
            'mla_q_norm_g': mla_q_norm_g, 'mla_w_q_up': mla_w_q_up,
            'mla_kv_norm_g': mla_kv_norm_g, 'mla_w_kv_up': mla_w_kv_up,
            'w_mix_out': w_mix_out, 'ln1_g': ln1_g, 'ln1_b': ln1_b,
            'w_router': w_router, 'b_router': b_router,
            'w_gate': w_gate, 'b_gate': b_gate, 'w_up': w_up, 'b_up': b_up,
            'w_down': w_down, 'b_down': b_down, 'ln2_g': ln2_g, 'ln2_b': ln2_b}


def reference(x, positions, w_mix_in, idx_k_norm_g, idx_k_norm_b, mla_q_norm_g, mla_w_q_up,
              mla_kv_norm_g, mla_w_kv_up, w_mix_out, ln1_g, ln1_b, w_router, b_router,
              w_gate, b_gate, w_up, b_up, w_down, b_down, ln2_g, ln2_b):
    rope_a = rope_cos_sin(positions, A_HEAD_DIM)
    rope_i = rope_cos_sin(positions, IDX_DIM)
    rope_r = rope_cos_sin(positions, B_ROPE)
    for l in range(DEPTH):
        x = hybrid_layer(x, rope_a, rope_i, rope_r, w_mix_in[l], idx_k_norm_g[l], idx_k_norm_b[l],
                         mla_q_norm_g[l], mla_w_q_up[l], mla_kv_norm_g[l], mla_w_kv_up[l],
                         w_mix_out[l], ln1_g[l], ln1_b[l], w_router[l], b_router[l],
                         w_gate[l], b_gate[l], w_up[l], b_up[l], w_down[l], b_down[l],
                         ln2_g[l], ln2_b[l])
    return x
```

```python
import functools

import jax
import jax.numpy as jnp
import numpy as np
from jax import lax
from jax.experimental import pallas as pl
from jax.experimental.pallas import tpu as pltpu

F32 = jnp.float32
BF16 = jnp.bfloat16

A_HEADS, A_HEAD_DIM = 8, 64
IDX_HEADS, IDX_DIM = 8, 64
TOPK_MAX = 256
B_HEADS, B_NOPE, B_ROPE, B_V = 8, 64, 32, 64
Q_LORA, KV_LORA = 256, 128
N_EXPERTS, TOP_K = 32, 4
SWIGLU_LIMIT, SWIGLU_ALPHA = 7.0, 1.702
ROPE_THETA = 10000.0
LN_EPS, RMS_EPS = 1e-5, 1e-6

LANES = 128
VMEM_LIMIT = 56 * 1024 * 1024

PROJ_TM = 512
ATT_TQ = 128
ATT_TK = 512
MIX_TM = 512
MOE_TM = 256
SEL_MAX_ITERS = 400

NEG_BIG = -1e30


def _dot(a, b):
    return jnp.dot(a, b, preferred_element_type=F32)


def _dot_t(a, b):
    return lax.dot_general(a, b, (((1,), (1,)), ((), ())), preferred_element_type=F32)


def _lane_iota(shape):
    return lax.broadcasted_iota(jnp.int32, shape, len(shape) - 1)


_C_AQ, _C_AQR, _C_IQ, _C_IQR = 0, 512, 1024, 1536
_C_KV, _C_KVR, _C_IK, _C_QC, _C_KVC, _C_KR, _C_KRR, _C_IW = 2048, 2176, 2304, 2432, 2688, 2816, 2944, 3072
_C_TOTAL = 3200


def _rot_cols(w, dim):
    k, n = w.shape
    w3 = w.reshape(k, n // dim, dim)
    return jnp.concatenate([-w3[..., dim // 2:], w3[..., :dim // 2]], axis=-1).reshape(k, n)


def _pad_cols(w, width):
    return jnp.pad(w, ((0, 0), (0, width - w.shape[1])))


def _pack_weights(w_in, w_q_up, w_kv_up):
    o = np.cumsum((0, 512, 64, 64, 512, 64, 8, 256, 128, 32))
    a_q, a_k, a_v, i_q, i_k, i_w, b_qc, b_kvc, b_kr = [w_in[:, o[i]:o[i + 1]] for i in range(9)]
    zeros64 = jnp.zeros_like(a_k)
    wcat = jnp.concatenate([
        a_q, _rot_cols(a_q, 64), i_q, _rot_cols(i_q, 64),
        a_k, a_v, _rot_cols(a_k, 64), zeros64,
        _pad_cols(i_k, 128), b_qc, b_kvc,
        _pad_cols(b_kr, 128), _pad_cols(_rot_cols(b_kr, 32), 128), _pad_cols(i_w, 128)], axis=1)
    assert wcat.shape[1] == _C_TOTAL
    wq = w_q_up.reshape(Q_LORA, B_HEADS, B_NOPE + B_ROPE)
    wq_nope = wq[..., :B_NOPE].reshape(Q_LORA, B_HEADS * B_NOPE)
    wq_rope = wq[..., B_NOPE:]
    wq_rope_rot = jnp.concatenate([-wq_rope[..., B_ROPE // 2:], wq_rope[..., :B_ROPE // 2]], axis=-1)
    pad = ((0, 0), (0, 0), (0, LANES - B_ROPE))
    wq2 = jnp.concatenate([wq_nope, jnp.pad(wq_rope, pad).reshape(Q_LORA, B_HEADS * LANES),
                           jnp.pad(wq_rope_rot, pad).reshape(Q_LORA, B_HEADS * LANES)], axis=1)
    wkv = w_kv_up.reshape(KV_LORA, B_HEADS, B_NOPE + B_V)
    w_uk = wkv[..., :B_NOPE]
    w_uv = wkv[..., B_NOPE:]
    eye = jnp.eye(B_HEADS, dtype=w_kv_up.dtype)
    bd_uk = jnp.einsum('lhd,hg->hdgl', w_uk, eye).reshape(B_HEADS * B_NOPE, B_HEADS * KV_LORA)
    bd_uv = jnp.einsum('lhd,hg->hlgd', w_uv, eye).reshape(B_HEADS * KV_LORA, B_HEADS * B_V)
    return wcat.astype(BF16), wq2.astype(BF16), bd_uk.astype(BF16), bd_uv.astype(BF16)


def _rope_tables(positions):
    pos = positions.astype(F32)[..., None]
    inv64 = ROPE_THETA ** (-jnp.arange(0, 64, 2, dtype=F32) / 64)
    inv32 = ROPE_THETA ** (-jnp.arange(0, 32, 2, dtype=F32) / 32)
    a64, a32 = pos * inv64, pos * inv32
    return (jnp.tile(jnp.cos(a64), 4), jnp.tile(jnp.sin(a64), 4),
            jnp.tile(jnp.cos(a32), 8), jnp.tile(jnp.sin(a32), 8))


def _proj_kernel(x_ref, ca_ref, sa_ref, cr_ref, sr_ref, wcat_ref, wq2_ref, bduk_ref,
                 ikg_ref, ikb_ref, qg_ref, kvg_ref,
                 aq_ref, iq_ref, iw_ref, akv_ref, ik_ref, qabs_ref, ckv_ref):
    tm = x_ref.shape[0]
    xb = x_ref[...].astype(BF16)
    ca, sa, cr, sr = ca_ref[...], sa_ref[...], cr_ref[...], sr_ref[...]
    lane = _lane_iota((tm, LANES))
    low64 = lane < 64

    def proj(c0, width):
        return _dot(xb, wcat_ref[:, c0:c0 + width])

    ca4 = jnp.concatenate([ca] * 4, axis=1)
    sa4 = jnp.concatenate([sa] * 4, axis=1)

    def split_heads(v, out_ref):
        for p in range(4):
            blk = v[:, p * LANES:(p + 1) * LANES]
            out_ref[0, 2 * p] = jnp.where(low64, blk, 0.0).astype(BF16)
            out_ref[0, 2 * p + 1] = jnp.where(low64, pltpu.roll(blk, 64, 1), 0.0).astype(BF16)

    aq = (proj(_C_AQ, 512) * ca4 + proj(_C_AQR, 512) * sa4) * (A_HEAD_DIM ** -0.5)
    split_heads(aq, aq_ref)
    iq = proj(_C_IQ, 512) * ca4 + proj(_C_IQR, 512) * sa4
    split_heads(iq, iq_ref)

    kc = jnp.where(low64, ca, 1.0)
    ks = jnp.where(low64, sa, 0.0)
    akv_ref[0] = (proj(_C_KV, 128) * kc + proj(_C_KVR, 128) * ks).astype(BF16)

    y = proj(_C_IK, 128)
    mu = jnp.sum(y, axis=1, keepdims=True) * (1.0 / IDX_DIM)
    d = jnp.where(low64, y - mu, 0.0)
    var = jnp.sum(d * d, axis=1, keepdims=True) * (1.0 / IDX_DIM)
    yn = d * lax.rsqrt(var + LN_EPS) * ikg_ref[...] + ikb_ref[...]
    rot = jnp.where(lane < 32, -pltpu.roll(yn, LANES - 32, 1), pltpu.roll(yn, 32, 1))
    ik_ref[0] = jnp.where(low64, yn * ca + rot * sa, 0.0).astype(BF16)

    iw_ref[0] = proj(_C_IW, 128)[:, :IDX_HEADS] * ((IDX_HEADS * IDX_DIM) ** -0.5)

    qc = proj(_C_QC, Q_LORA)
    qn = qc * lax.rsqrt(jnp.mean(qc * qc, axis=1, keepdims=True) + RMS_EPS) * qg_ref[...]
    qn = qn.astype(BF16)
    scale = (B_NOPE + B_ROPE) ** -0.5
    q_nope = _dot(qn, wq2_ref[:, 0:512]).astype(BF16)
    q_lat = _dot(q_nope, bduk_ref[...]) * scale
    cr8 = jnp.concatenate([cr] * 8, axis=1)
    sr8 = jnp.concatenate([sr] * 8, axis=1)
    q_r = (_dot(qn, wq2_ref[:, 512:1536]) * cr8 + _dot(qn, wq2_ref[:, 1536:2560]) * sr8) * scale
    for h in range(B_HEADS):
        qabs_ref[0, h] = jnp.concatenate(
            [q_lat[:, h * LANES:(h + 1) * LANES], q_r[:, h * LANES:(h + 1) * LANES]], axis=1).astype(BF16)

    kvc = proj(_C_KVC, KV_LORA)
    kvn = kvc * lax.rsqrt(jnp.mean(kvc * kvc, axis=1, keepdims=True) + RMS_EPS) * kvg_ref[...]
    kr = proj(_C_KR, 128) * cr + proj(_C_KRR, 128) * sr
    ckv_ref[0] = jnp.concatenate([kvn, kr], axis=1).astype(BF16)


def _proj(x, tabs, wcat, wq2, bd_uk, ik_g, ik_b, q_g, kv_g):
    bsz, seq, d = x.shape
    tm = PROJ_TM
    nt = seq // tm
    tok = lambda b, i: (b, i, 0)
    const2 = lambda b, i: (0, 0)
    head = lambda b, i: (b, 0, i, 0)
    tab_spec = pl.BlockSpec((None, tm, LANES), tok)
    return pl.pallas_call(
        _proj_kernel,
        grid=(bsz, nt),
        in_specs=[pl.BlockSpec((None, tm, d), tok), tab_spec, tab_spec, tab_spec, tab_spec,
                  pl.BlockSpec(wcat.shape, const2), pl.BlockSpec(wq2.shape, const2),
                  pl.BlockSpec(bd_uk.shape, const2),
                  pl.BlockSpec((1, LANES), const2), pl.BlockSpec((1, LANES), const2),
                  pl.BlockSpec((1, Q_LORA), const2), pl.BlockSpec((1, KV_LORA), const2)],
        out_specs=[pl.BlockSpec((1, A_HEADS, tm, LANES), head),
                   pl.BlockSpec((1, IDX_HEADS, tm, LANES), head),
                   pl.BlockSpec((1, tm, IDX_HEADS), tok),
                   pl.BlockSpec((1, tm, LANES), tok),
                   pl.BlockSpec((1, tm, LANES), tok),
                   pl.BlockSpec((1, B_HEADS, tm, 2 * LANES), head),
                   pl.BlockSpec((1, tm, 2 * LANES), tok)],
        out_shape=[jax.ShapeDtypeStruct((bsz, A_HEADS, seq, LANES), BF16),
                   jax.ShapeDtypeStruct((bsz, IDX_HEADS, seq, LANES), BF16),
                   jax.ShapeDtypeStruct((bsz, seq, IDX_HEADS), F32),
                   jax.ShapeDtypeStruct((bsz, seq, LANES), BF16),
                   jax.ShapeDtypeStruct((bsz, seq, LANES), BF16),
                   jax.ShapeDtypeStruct((bsz, B_HEADS, seq, 2 * LANES), BF16),
                   jax.ShapeDtypeStruct((bsz, seq, 2 * LANES), BF16)],
        compiler_params=pltpu.CompilerParams(
            dimension_semantics=("arbitrary", "arbitrary"), vmem_limit_bytes=VMEM_LIMIT),
        name="proj",
    )(x, *tabs, wcat, wq2, bd_uk, ik_g, ik_b, q_g, kv_g)


def _dsa_kernel(aq_ref, iq_ref, iw_ref, akv_ref, ik_ref, out_ref, s_sc, m_sc, l_sc, acc_sc):
    nh, tq, _ = aq_ref.shape
    tk = ATT_TK
    seq = akv_ref.shape[0]
    j = pl.program_id(1)
    nkb = (j * tq) // tk + 1
    q2 = aq_ref[...].reshape(nh * tq, LANES)
    iq2 = iq_ref[...].reshape(nh * tq, LANES)
    w = iw_ref[...]
    trow = j * tq + lax.broadcasted_iota(jnp.int32, (tq, tk), 0)
    kcol = lax.broadcasted_iota(jnp.int32, (tq, tk), 1)
    col128 = lax.broadcasted_iota(jnp.int32, (tq, LANES), 1).astype(F32)

    def score_body(kb, carry):
        rmax, rmin = carry
        k0 = pl.multiple_of(kb * tk, tk)
        y = _dot_t(iq2, ik_ref[pl.ds(k0, tk), :])
        s = jnp.zeros((tq, tk), F32)
        for h in range(nh):
            s = s + w[:, h:h + 1] * jnp.maximum(y[h * tq:(h + 1) * tq], 0.0)
        causal = (k0 + kcol) <= trow
        s_lo = jnp.where(causal, s, -jnp.inf)
        s_sc[:, pl.ds(k0, tk)] = s_lo
        rmax = jnp.maximum(rmax, jnp.max(s_lo, axis=1, keepdims=True))
        rmin = jnp.minimum(rmin, jnp.min(jnp.where(causal, s, jnp.inf), axis=1, keepdims=True))
        return rmax, rmin

    rmax, rmin = lax.fori_loop(0, nkb, score_body,
                               (jnp.full((tq, 1), -jnp.inf, F32), jnp.full((tq, 1), jnp.inf, F32)))

    t_abs = j * tq + lax.broadcasted_iota(jnp.int32, (tq, 1), 0)
    n_causal = (t_abs + 1).astype(F32)
    kt = jnp.minimum(n_causal, float(TOPK_MAX))

    def chunks(fn, init):
        def body(kb, acc):
            k0 = pl.multiple_of(kb * tk, tk)
            for c in range(tk // LANES):
                off = k0 + c * LANES
                acc = fn(acc, s_sc[:, pl.ds(off, LANES)], off)
            return acc
        return lax.fori_loop(0, nkb, body, init)

    def count_ge(thr):
        thr_b = jnp.broadcast_to(thr, (tq, LANES))
        acc = chunks(lambda a, blk, off: a + jnp.where(blk >= thr_b, 1.0, 0.0), jnp.zeros((tq, LANES), F32))
        return jnp.sum(acc, axis=1, keepdims=True)

    def propose(it, lo, hi, clo, chi):
        frac = jnp.clip((clo - kt + 0.5) / jnp.maximum(clo - chi, 1.0), 1.0 / 32, 31.0 / 32)
        mid_i = lo + (hi - lo) * frac
        mid_b = 0.5 * lo + 0.5 * hi
        use_i = jnp.logical_and(it % 2 == 0, jnp.logical_and(mid_i > lo, mid_i < hi))
        mid = jnp.where(use_i, mid_i, mid_b)
        active = jnp.logical_and(clo != kt, jnp.logical_and(mid > lo, mid < hi))
        return mid, active

    def sel_cond(c):
        return c[1]

    def sel_body(c):
        it, _, lo, hi, clo, chi = c
        mid, active = propose(it, lo, hi, clo, chi)
        cm = count_ge(mid)
        ge = cm >= kt
        up = jnp.logical_and(active, ge)
        dn = jnp.logical_and(active, jnp.logical_not(ge))
        lo, clo = jnp.where(up, mid, lo), jnp.where(up, cm, clo)
        hi, chi = jnp.where(dn, mid, hi), jnp.where(dn, cm, chi)
        _, active = propose(it + 1, lo, hi, clo, chi)
        go = jnp.logical_and(jnp.max(jnp.where(active, 1.0, 0.0)) > 0.0, it + 1 < SEL_MAX_ITERS)
        return it + 1, go, lo, hi, clo, chi

    lo0 = rmin
    hi0 = rmax + jnp.abs(rmax) + 1e-30
    clo0, chi0 = n_causal, jnp.zeros((tq, 1), F32)
    _, active0 = propose(0, lo0, hi0, clo0, chi0)
    go0 = jnp.max(jnp.where(active0, 1.0, 0.0)) > 0.0
    _, _, lo, hi, clo, chi = lax.while_loop(sel_cond, sel_body, (jnp.int32(0), go0, lo0, hi0, clo0, chi0))
    lo_b = jnp.broadcast_to(lo, (tq, LANES))

    tie = clo > kt

    @pl.when(jnp.max(jnp.where(tie, 1.0, 0.0)) > 0.0)
    def _():
        need = kt - chi

        def count_eq_le(jmax):
            j_b = jnp.broadcast_to(jmax, (tq, LANES))
            return jnp.sum(chunks(
                lambda a, blk, off: a + jnp.where(jnp.logical_and(blk == lo_b, col128 + off.astype(F32) <= j_b), 1.0, 0.0),
                jnp.zeros((tq, LANES), F32)), axis=1, keepdims=True)

        def idx_body(_, c):
            jl, jh = c
            jm = jnp.floor((jl + jh) * 0.5)
            ok = count_eq_le(jm) >= need
            return jnp.where(ok, jl, jm), jnp.where(ok, jm, jh)

        n_steps = int(np.ceil(np.log2(seq))) + 1
        _, jh = lax.fori_loop(0, n_steps, idx_body,
                              (jnp.full((tq, 1), -1.0, F32), jnp.full((tq, 1), float(seq - 1), F32)))
        jh_b = jnp.broadcast_to(jnp.where(tie, jh, float(seq)), (tq, LANES))

        def fix_body(kb, _):
            k0 = pl.multiple_of(kb * tk, tk)
            for c in range(tk // LANES):
                off = k0 + c * LANES
                blk = s_sc[:, pl.ds(off, LANES)]
                drop = jnp.logical_and(blk == lo_b, col128 + off.astype(F32) > jh_b)
                s_sc[:, pl.ds(off, LANES)] = jnp.where(drop, -jnp.inf, blk)
            return 0

        lax.fori_loop(0, nkb, fix_body, 0)

    m_sc[...] = jnp.full(m_sc.shape, NEG_BIG, F32)
    l_sc[...] = jnp.zeros(l_sc.shape, F32)
    acc_sc[...] = jnp.zeros(acc_sc.shape, F32)
    lo_k = jnp.broadcast_to(lo, (tq, tk))

    def att_body(kb, _):
        k0 = pl.multiple_of(kb * tk, tk)
        kv = akv_ref[pl.ds(k0, tk), :]
        logits = _dot_t(q2, kv)
        sel = s_sc[:, pl.ds(k0, tk)] >= lo_k
        for h in range(nh):
            rows = slice(h * tq, (h + 1) * tq)
            lh = logits[rows]
            m_old = m_sc[rows]
            m_new = jnp.maximum(m_old, jnp.max(jnp.where(sel, lh, NEG_BIG), axis=1, keepdims=True))
            alpha = jnp.exp(m_old - m_new)
            p = jnp.where(sel, jnp.exp(lh - m_new), 0.0)
            l_sc[rows] = alpha * l_sc[rows] + jnp.sum(p, axis=1, keepdims=True)
            acc_sc[rows] = alpha * acc_sc[rows] + _dot(p.astype(BF16), kv)
            m_sc[rows] = m_new
        return 0

    lax.fori_loop(0, nkb, att_body, 0)

    lane = _lane_iota((tq, LANES))
    for p in range(nh // 2):
        r0 = slice(2 * p * tq, (2 * p + 1) * tq)
        r1 = slice((2 * p + 1) * tq, (2 * p + 2) * tq)
        even = acc_sc[r0] / l_sc[r0]
        odd = acc_sc[r1] / l_sc[r1]
        out_ref[:, p * LANES:(p + 1) * LANES] = jnp.where(lane < 64, pltpu.roll(even, 64, 1), odd).astype(out_ref.dtype)


def _dsa(aq, iq, iw, akv, ikp):
    bsz, nh, seq, _ = aq.shape
    tq = ATT_TQ
    head = lambda b, j: (b, 0, j, 0)
    full = lambda b, j: (b, 0, 0)
    return pl.pallas_call(
        _dsa_kernel,
        grid=(bsz, seq // tq),
        in_specs=[pl.BlockSpec((None, nh, tq, LANES), head), pl.BlockSpec((None, nh, tq, LANES), head),
                  pl.BlockSpec((None, tq, nh), lambda b, j: (b, j, 0)),
                  pl.BlockSpec((None, seq, LANES), full), pl.BlockSpec((None, seq, LANES), full)],
        out_specs=pl.BlockSpec((None, tq, nh * A_HEAD_DIM), lambda b, j: (b, j, 0)),
        out_shape=jax.ShapeDtypeStruct((bsz, seq, nh * A_HEAD_DIM), BF16),
        scratch_shapes=[pltpu.VMEM((tq, seq), F32), pltpu.VMEM((nh * tq, 1), F32),
                        pltpu.VMEM((nh * tq, 1), F32), pltpu.VMEM((nh * tq, LANES), F32)],
        compiler_params=pltpu.CompilerParams(
            dimension_semantics=("arbitrary", "arbitrary"), vmem_limit_bytes=VMEM_LIMIT),
        name="dsa",
    )(aq, iq, iw, akv, ikp)


def _mla_kernel(q_ref, ckv_ref, bduv_ref, out_ref, m_sc, l_sc, acc_sc):
    nh, tq, dq = q_ref.shape
    tk = ATT_TK
    j = pl.program_id(1)
    nkb = (j * tq) // tk + 1
    q2 = q_ref[...].reshape(nh * tq, dq)
    trow = j * tq + lax.broadcasted_iota(jnp.int32, (tq, tk), 0)
    kcol = lax.broadcasted_iota(jnp.int32, (tq, tk), 1)
    m_sc[...] = jnp.full(m_sc.shape, NEG_BIG, F32)
    l_sc[...] = jnp.zeros(l_sc.shape, F32)
    acc_sc[...] = jnp.zeros(acc_sc.shape, F32)

    def att_body(kb, _):
        k0 = pl.multiple_of(kb * tk, tk)
        c = ckv_ref[pl.ds(k0, tk), :]
        logits = _dot_t(q2, c)
        causal = (k0 + kcol) <= trow
        lat = c[:, :KV_LORA]
        for h in range(nh):
            rows = slice(h * tq, (h + 1) * tq)
            lh = jnp.where(causal, logits[rows], NEG_BIG)
            m_old = m_sc[rows]
            m_new = jnp.maximum(m_old, jnp.max(lh, axis=1, keepdims=True))
            alpha = jnp.exp(m_old - m_new)
            p = jnp.exp(lh - m_new)
            l_sc[rows] = alpha * l_sc[rows] + jnp.sum(p, axis=1, keepdims=True)
            acc_sc[rows] = alpha * acc_sc[rows] + _dot(p.astype(BF16), lat)
            m_sc[rows] = m_new
        return 0

    lax.fori_loop(0, nkb, att_body, 0)
    o_lat = jnp.concatenate([(acc_sc[h * tq:(h + 1) * tq] / l_sc[h * tq:(h + 1) * tq]).astype(BF16)
                             for h in range(nh)], axis=1)
    out_ref[...] = _dot(o_lat, bduv_ref[...]).astype(out_ref.dtype)


def _mla(qabs, ckv, bd_uv):
    bsz, nh, seq, dq = qabs.shape
    tq = ATT_TQ
    return pl.pallas_call(
        _mla_kernel,
        grid=(bsz, seq // tq),
        in_specs=[pl.BlockSpec((None, nh, tq, dq), lambda b, j: (b, 0, j, 0)),
                  pl.BlockSpec((None, seq, dq), lambda b, j: (b, 0, 0)),
                  pl.BlockSpec(bd_uv.shape, lambda b, j: (0, 0))],
        out_specs=pl.BlockSpec((None, tq, nh * B_V), lambda b, j: (b, j, 0)),
        out_shape=jax.ShapeDtypeStruct((bsz, seq, nh * B_V), BF16),
        scratch_shapes=[pltpu.VMEM((nh * tq, 1), F32), pltpu.VMEM((nh * tq, 1), F32),
                        pltpu.VMEM((nh * tq, KV_LORA), F32)],
        compiler_params=pltpu.CompilerParams(
            dimension_semantics=("arbitrary", "arbitrary"), vmem_limit_bytes=VMEM_LIMIT),
        name="mla",
    )(qabs, ckv, bd_uv)


def _layer_norm(z, g, b):
    mu = jnp.mean(z, axis=1, keepdims=True)
    d = z - mu
    var = jnp.mean(d * d, axis=1, keepdims=True)
    return d * lax.rsqrt(var + LN_EPS) * g + b


def _split_bf16(v):
    hi = v.astype(BF16)
    return hi, (v - hi.astype(F32)).astype(BF16)


def _mix_kernel(x_ref, a_ref, b_ref, wout_ref, g_ref, beta_ref, wr_ref, br_ref, h_ref, ti_ref, tg_ref, *, alpha):
    tm = x_ref.shape[0]
    half = a_ref.shape[1]
    mix = _dot(a_ref[...], wout_ref[0:half, :]) + _dot(b_ref[...], wout_ref[half:2 * half, :])
    h = _layer_norm(alpha * x_ref[...] + mix, g_ref[...], beta_ref[...])
    h_ref[...] = h
    h_hi, h_lo = _split_bf16(h)
    w_hi, w_lo = _split_bf16(wr_ref[...])
    logits = _dot(h_hi, w_hi) + (_dot(h_lo, w_hi) + _dot(h_hi, w_lo)) + br_ref[...]
    lane = _lane_iota((tm, LANES))
    logits = jnp.where(lane < N_EXPERTS, logits, -jnp.inf)
    ids = jnp.zeros((tm, LANES), jnp.int32)
    vals = []
    for k in range(TOP_K):
        m = jnp.max(logits, axis=1, keepdims=True)
        idx = jnp.min(jnp.where(logits == m, lane, LANES), axis=1, keepdims=True)
        ids = jnp.where(lane == k, idx, ids)
        logits = jnp.where(lane == idx, -jnp.inf, logits)
        vals.append(m)
    es = [jnp.exp(v - vals[0]) for v in vals]
    den = (es[0] + es[1]) + (es[2] + es[3])
    gates = jnp.zeros((tm, LANES), F32)
    for k in range(TOP_K):
        gates = jnp.where(lane == k, es[k] / den, gates)
    ti_ref[...] = ids[:, :TOP_K]
    tg_ref[...] = gates[:, :TOP_K]


def _mix(x2, a_out, b_out, w_out, g, beta, w_router, b_router, alpha):
    n, d = x2.shape
    tm = MIX_TM
    half = a_out.shape[1]
    row = lambda i: (i, 0)
    const = lambda i: (0, 0)
    wr = jnp.pad(w_router, ((0, 0), (0, LANES - N_EXPERTS)))
    br = jnp.pad(b_router, (0, LANES - N_EXPERTS)).reshape(1, LANES)
    return pl.pallas_call(
        functools.partial(_mix_kernel, alpha=alpha),
        grid=(n // tm,),
        in_specs=[pl.BlockSpec((tm, d), row), pl.BlockSpec((tm, half), row), pl.BlockSpec((tm, half), row),
                  pl.BlockSpec(w_out.shape, const), pl.BlockSpec((1, d), const), pl.BlockSpec((1, d), const),
                  pl.BlockSpec(wr.shape, const), pl.BlockSpec((1, LANES), const)],
        out_specs=[pl.BlockSpec((tm, d), row), pl.BlockSpec((tm, TOP_K), row), pl.BlockSpec((tm, TOP_K), row)],
        out_shape=[jax.ShapeDtypeStruct((n, d), F32), jax.ShapeDtypeStruct((n, TOP_K), jnp.int32),
                   jax.ShapeDtypeStruct((n, TOP_K), F32)],
        compiler_params=pltpu.CompilerParams(dimension_semantics=("arbitrary",), vmem_limit_bytes=VMEM_LIMIT),
        name="mix",
    )(x2, a_out, b_out, w_out.astype(BF16), g.reshape(1, d), beta.reshape(1, d), wr, br)


def _route(top_i, gates, tm):
    n = top_i.shape[0]
    na = n * TOP_K
    n_tiles = na // tm + N_EXPERTS
    e_flat = top_i.reshape(na)
    order = jnp.argsort(e_flat, stable=True).astype(jnp.int32)
    e_sorted = e_flat[order]
    counts = jnp.sum(e_flat[:, None] == jnp.arange(N_EXPERTS, dtype=jnp.int32)[None, :], axis=0, dtype=jnp.int32)
    starts = jnp.cumsum(counts) - counts
    tiles_per = (counts + tm - 1) // tm
    tile_end = jnp.cumsum(tiles_per)
    pstart = (tile_end - tiles_per) * tm
    ppos = pstart[e_sorted] + (jnp.arange(na, dtype=jnp.int32) - starts[e_sorted])
    p_total = n_tiles * tm
    tok = order // TOP_K
    slot = order % TOP_K
    tok_pad = jnp.zeros((p_total,), jnp.int32).at[ppos].set(tok)
    trash = na + (jnp.arange(p_total, dtype=jnp.int32) % tm)
    dst_pad = trash.at[ppos].set(slot * n + tok)
    gate_pad = jnp.zeros((p_total,), F32).at[ppos].set(gates.reshape(na)[order])
    tile_ids = jnp.arange(n_tiles, dtype=jnp.int32)
    n_valid = tile_end[-1]
    tile_expert = jnp.searchsorted(tile_end, jnp.minimum(tile_ids, n_valid - 1), side='right').astype(jnp.int32)
    tile_valid = (tile_ids < n_valid).astype(jnp.int32)
    return (tok_pad.reshape(n_tiles, tm), dst_pad.reshape(n_tiles, tm), gate_pad.reshape(p_total, 1),
            tile_expert, tile_valid)


def _moe_kernel(te_ref, tv_ref, h_hbm, tok_hbm, dst_hbm, gate_ref, wg_ref, bg_ref, wu_ref, bu_ref, wd_ref, bd_ref,
                y_hbm, tok_s, dst_s, xbuf, ybuf, sem_idx, sem_in, sem_out):
    del te_ref
    i = pl.program_id(0)
    tm = xbuf.shape[0]

    @pl.when(i == 0)
    def _():
        spare = y_hbm.at[pl.ds(y_hbm.shape[0] - tm, tm), :]
        ybuf[...] = jnp.zeros(ybuf.shape, ybuf.dtype)
        cp = pltpu.make_async_copy(ybuf, spare, sem_out)
        cp.start()
        cp.wait()

    @pl.when(tv_ref[i] > 0)
    def _():
        cp_tok = pltpu.make_async_copy(tok_hbm.at[i], tok_s, sem_idx.at[0])
        cp_dst = pltpu.make_async_copy(dst_hbm.at[i], dst_s, sem_idx.at[1])
        cp_tok.start()
        cp_dst.start()
        cp_tok.wait()

        def gather(r, _):
            pltpu.make_async_copy(h_hbm.at[pl.ds(tok_s[r], 1), :], xbuf.at[pl.ds(r, 1), :], sem_in).start()
            return 0

        lax.fori_loop(0, tm, gather, 0, unroll=8)
        pltpu.make_async_copy(h_hbm.at[pl.ds(0, tm), :], xbuf, sem_in).wait()

        x = xbuf[...].astype(BF16)
        g = jnp.minimum(_dot(x, wg_ref[0]) + bg_ref[0], SWIGLU_LIMIT)
        u = jnp.clip(_dot(x, wu_ref[0]) + bu_ref[0], -SWIGLU_LIMIT, SWIGLU_LIMIT)
        act = g * (1.0 / (1.0 + jnp.exp(-SWIGLU_ALPHA * g))) * (u + 1.0)
        ybuf[...] = (_dot(act.astype(BF16), wd_ref[0]) + bd_ref[0]) * gate_ref[...]

        cp_dst.wait()

        def scatter(r, _):
            pltpu.make_async_copy(ybuf.at[pl.ds(r, 1), :], y_hbm.at[pl.ds(dst_s[r], 1), :], sem_out).start()
            return 0

        lax.fori_loop(0, tm, scatter, 0, unroll=8)
        pltpu.make_async_copy(ybuf, y_hbm.at[pl.ds(0, tm), :], sem_out).wait()


def _moe(h, route, w_gate, b_gate, w_up, b_up, w_down, b_down):
    n, d = h.shape
    tok_pad, dst_pad, gate_pad, tile_expert, tile_valid = route
    n_tiles, tm = tok_pad.shape
    ne, _, dff = w_gate.shape
    wmap = lambda i, te, tv: (te[i], 0, 0)
    any_spec = pl.BlockSpec(memory_space=pl.ANY)
    grid_spec = pltpu.PrefetchScalarGridSpec(
        num_scalar_prefetch=2,
        grid=(n_tiles,),
        in_specs=[any_spec, any_spec, any_spec,
                  pl.BlockSpec((tm, 1), lambda i, te, tv: (i, 0)),
                  pl.BlockSpec((1, d, dff), wmap), pl.BlockSpec((1, 1, dff), wmap),
                  pl.BlockSpec((1, d, dff), wmap), pl.BlockSpec((1, 1, dff), wmap),
                  pl.BlockSpec((1, dff, d), wmap), pl.BlockSpec((1, 1, d), wmap)],
        out_specs=any_spec,
        scratch_shapes=[pltpu.SMEM((tm,), jnp.int32), pltpu.SMEM((tm,), jnp.int32),
                        pltpu.VMEM((tm, d), F32), pltpu.VMEM((tm, d), F32),
                        pltpu.SemaphoreType.DMA((2,)), pltpu.SemaphoreType.DMA, pltpu.SemaphoreType.DMA])
    return pl.pallas_call(
        _moe_kernel,
        grid_spec=grid_spec,
        out_shape=jax.ShapeDtypeStruct((TOP_K * n + tm, d), F32),
        compiler_params=pltpu.CompilerParams(dimension_semantics=("arbitrary",), vmem_limit_bytes=VMEM_LIMIT),
        name="moe",
    )(tile_expert, tile_valid, h, tok_pad, dst_pad, gate_pad,
      w_gate.astype(BF16), b_gate.reshape(ne, 1, dff), w_up.astype(BF16), b_up.reshape(ne, 1, dff),
      w_down.astype(BF16), b_down.reshape(ne, 1, d))


def _combine_kernel(h_ref, y0_ref, y1_ref, y2_ref, y3_ref, g_ref, beta_ref, o_ref, *, alpha):
    ffn = (y0_ref[...] + y1_ref[...]) + (y2_ref[...] + y3_ref[...])
    o_ref[...] = _layer_norm(alpha * h_ref[...] + ffn, g_ref[...], beta_ref[...])


def _combine(h, y, g, beta, alpha):
    n, d = h.shape
    tm = MIX_TM
    nt = n // tm
    const = lambda i: (0, 0)
    slot_specs = [pl.BlockSpec((tm, d), functools.partial(lambda i, s: (i + s * nt, 0), s=s)) for s in range(TOP_K)]
    return pl.pallas_call(
        functools.partial(_combine_kernel, alpha=alpha),
        grid=(nt,),
        in_specs=[pl.BlockSpec((tm, d), lambda i: (i, 0))] + slot_specs
                 + [pl.BlockSpec((1, d), const), pl.BlockSpec((1, d), const)],
        out_specs=pl.BlockSpec((tm, d), lambda i: (i, 0)),
        out_shape=jax.ShapeDtypeStruct((n, d), F32),
        compiler_params=pltpu.CompilerParams(dimension_semantics=("arbitrary",), vmem_limit_bytes=VMEM_LIMIT),
        name="combine",
    )(h, y, y, y, y, g.reshape(1, d), beta.reshape(1, d))


def _pad_row(v, width):
    return jnp.pad(v, (0, width - v.shape[0])).reshape(1, width)


def _layer(x, tabs, alpha, w_in, ik_g, ik_b, q_g, w_q_up, kv_g, w_kv_up, w_out, ln1_g, ln1_b,
           w_router, b_router, w_gate, b_gate, w_up, b_up, w_down, b_down, ln2_g, ln2_b):
    bsz, seq, d = x.shape
    wcat, wq2, bd_uk, bd_uv = _pack_weights(w_in, w_q_up, w_kv_up)
    aq, iq, iw, akv, ikp, qabs, ckv = _proj(
        x, tabs, wcat, wq2, bd_uk, _pad_row(ik_g, LANES), _pad_row(ik_b, LANES),
        q_g.reshape(1, Q_LORA), kv_g.reshape(1, KV_LORA))
    a_out = _dsa(aq, iq, iw, akv, ikp)
    b_out = _mla(qabs, ckv, bd_uv)
    n = bsz * seq
    h, top_i, gates = _mix(x.reshape(n, d), a_out.reshape(n, -1), b_out.reshape(n, -1), w_out,
                           ln1_g, ln1_b, w_router, b_router, alpha)
    y = _moe(h, _route(top_i, gates, MOE_TM), w_gate, b_gate, w_up, b_up, w_down, b_down)
    return _combine(h, y, ln2_g, ln2_b, alpha).reshape(bsz, seq, d)


def kernel(x, positions, w_mix_in, idx_k_norm_g, idx_k_norm_b, mla_q_norm_g, mla_w_q_up, mla_kv_norm_g, mla_w_kv_up, w_mix_out, ln1_g, ln1_b, w_router, b_router, w_gate, b_gate, w_up, b_up, w_down, b_down, ln2_g, ln2_b):
    depth = w_mix_in.shape[0]
    alpha = float((2 * depth) ** 0.25)
    tabs = _rope_tables(positions)
    for l in range(depth):
        x = _layer(x, tabs, alpha, w_mix_in[l], idx_k_norm_g[l], idx_k_norm_b[l], mla_q_norm_g[l], mla_w_q_up[l],
                   mla_kv_norm_g[l], mla_w_kv_up[l], w_mix_out[l], ln1_g[l], ln1_b[l], w_router[l], b_router[l],
                   w_gate[l], b_gate[l], w_up[l], b_up[l], w_down[l], b_down[l], ln2_g[l], ln2_b[l])
    return x
```

```python
import functools

import jax
import jax.numpy as jnp
import numpy as np
from jax import lax
from jax.experimental import pallas as pl
from jax.experimental.pallas import tpu as pltpu

F32 = jnp.float32
BF16 = jnp.bfloat16

A_HEADS, A_HEAD_DIM = 8, 64
IDX_HEADS, IDX_DIM = 8, 64
TOPK_MAX = 256
B_HEADS, B_NOPE, B_ROPE, B_V = 8, 64, 32, 64
Q_LORA, KV_LORA = 256, 128
N_EXPERTS, TOP_K = 32, 4
SWIGLU_LIMIT, SWIGLU_ALPHA = 7.0, 1.702
ROPE_THETA = 10000.0
LN_EPS, RMS_EPS = 1e-5, 1e-6

LANES = 128
SUBLANES = 8
VMEM_LIMIT = 56 * 1024 * 1024

PROJ_TM = 512
ATT_TQ = LANES
ATT_TK = 512
ATT_GROUPS = 1
LOG2_E = 1.4426950408889634
MIX_TM = 512
MOE_TM = 256
ROW_TM = 256
SEL_MAX_ITERS = 400
SEL_UNROLL = 2

MASKED = -1e30
M_FLOOR = -1e29
TINY = float(np.finfo(np.float32).tiny)


def _dot(a, b):
    return jnp.dot(a, b, preferred_element_type=F32)


def _dot_t(a, b):
    return lax.dot_general(a, b, (((1,), (1,)), ((), ())), preferred_element_type=F32)


def _lane_iota(shape):
    return lax.broadcasted_iota(jnp.int32, shape, len(shape) - 1)


def _fold(v, op, rows=64):
    n = v.shape[0] // rows
    acc = v[0:rows]
    for i in range(1, n):
        acc = op(acc, v[i * rows:(i + 1) * rows])
    return acc


def _rows_reduce(v, op):
    if op == "sum":
        return jnp.sum(v, axis=0, keepdims=True)
    if op == "max":
        return jnp.max(v, axis=0, keepdims=True)
    return jnp.min(v, axis=0, keepdims=True)


_R_AQ, _R_AQR, _R_IQ, _R_IQR, _R_V, _R_IW, _R_QC, _R_KVC, _R_TOTAL = 0, 512, 1024, 1536, 2048, 2112, 2128, 2384, 2512
_C_KV, _C_KVR, _C_IK, _C_KVC, _C_KR, _C_KRR, _C_TOTAL = 0, 128, 256, 384, 512, 640, 768


def _rot_cols(w, dim):
    k, n = w.shape
    w3 = w.reshape(k, n // dim, dim)
    return jnp.concatenate([-w3[..., dim // 2:], w3[..., :dim // 2]], axis=-1).reshape(k, n)


def _pad_cols(w, width):
    return jnp.pad(w, ((0, 0), (0, width - w.shape[1])))


def _pack_weights(w_in, w_q_up, w_kv_up):
    o = np.cumsum((0, 512, 64, 64, 512, 64, 8, 256, 128, 32))
    a_q, a_k, a_v, i_q, i_k, i_w, b_qc, b_kvc, b_kr = [w_in[:, o[i]:o[i + 1]] for i in range(9)]
    w_t = jnp.concatenate([a_q, _rot_cols(a_q, 64), i_q, _rot_cols(i_q, 64), a_v, _pad_cols(i_w, 16),
                           b_qc, b_kvc], axis=1).T
    assert w_t.shape[0] == _R_TOTAL
    w_n = jnp.concatenate([a_k, a_v, _rot_cols(a_k, 64), jnp.zeros_like(a_k), _pad_cols(i_k, 128), b_kvc,
                           _pad_cols(b_kr, 128), _pad_cols(_rot_cols(b_kr, 32), 128)], axis=1)
    assert w_n.shape[1] == _C_TOTAL
    wq = w_q_up.reshape(Q_LORA, B_HEADS, B_NOPE + B_ROPE)
    wq_rope = wq[..., B_NOPE:].reshape(Q_LORA, B_HEADS * B_ROPE)
    wq_t = jnp.concatenate([wq[..., :B_NOPE].reshape(Q_LORA, B_HEADS * B_NOPE), wq_rope,
                            _rot_cols(wq_rope, B_ROPE)], axis=1).T
    wkv = w_kv_up.reshape(KV_LORA, B_HEADS, B_NOPE + B_V)
    eye = jnp.eye(B_HEADS, dtype=w_kv_up.dtype)
    bd_uk_t = jnp.einsum('lhd,hg->glhd', wkv[..., :B_NOPE], eye).reshape(B_HEADS * KV_LORA, B_HEADS * B_NOPE)
    w_uv_t = jnp.transpose(wkv[..., B_NOPE:], (1, 2, 0))
    return w_t.astype(BF16), w_n.astype(BF16), wq_t.astype(BF16), bd_uk_t.astype(BF16), w_uv_t.astype(BF16)


def _rope_tables(positions):
    pos = positions.astype(F32)[..., None]
    inv64 = ROPE_THETA ** (-jnp.arange(0, 64, 2, dtype=F32) / 64)
    inv32 = ROPE_THETA ** (-jnp.arange(0, 32, 2, dtype=F32) / 32)
    c64, s64 = jnp.cos(pos * inv64), jnp.sin(pos * inv64)
    c32, s32 = jnp.cos(pos * inv32), jnp.sin(pos * inv32)
    nat = (jnp.tile(c64, 4), jnp.tile(s64, 4), jnp.tile(c32, 8), jnp.tile(s32, 8))
    tr = tuple(jnp.swapaxes(jnp.tile(t, 2), 1, 2) for t in (c64, s64, c32, s32))
    return nat + tr


def _proj_kernel(x_ref, ca_ref, sa_ref, cr_ref, sr_ref, cat_ref, sat_ref, crt_ref, srt_ref,
                 wt_ref, wn_ref, wqt_ref, bdukt_ref, ikg_ref, ikb_ref, qg_ref, kvg_ref, kvgt_ref,
                 aqt_ref, iqt_ref, iwt_ref, akv_ref, avt_ref, ik_ref, qabst_ref, ckv_ref, latt_ref):
    tm = x_ref.shape[0]
    nqb = tm // ATT_TQ
    xb = x_ref[...].astype(BF16)
    ca, sa, cr, sr = ca_ref[...], sa_ref[...], cr_ref[...], sr_ref[...]
    lane = _lane_iota((tm, LANES))
    low64 = lane < 64

    def proj_t(r0, rows):
        return _dot_t(wt_ref[r0:r0 + rows, :], xb)

    def proj_n(c0, width):
        return _dot(xb, wn_ref[:, c0:c0 + width])

    cat8 = jnp.concatenate([cat_ref[...]] * 8, axis=0)
    sat8 = jnp.concatenate([sat_ref[...]] * 8, axis=0)

    def split_heads(v, out_ref):
        vb = v.astype(BF16)
        zeros = jnp.zeros((64, ATT_TQ), BF16)
        for c in range(nqb):
            for h in range(8):
                out_ref[c, 0:64, h * ATT_TQ:(h + 1) * ATT_TQ] = vb[64 * h:64 * h + 64, c * ATT_TQ:(c + 1) * ATT_TQ]
                out_ref[c, 64:128, h * ATT_TQ:(h + 1) * ATT_TQ] = zeros

    aq_t = (proj_t(_R_AQ, 512) * cat8 + proj_t(_R_AQR, 512) * sat8) * (A_HEAD_DIM ** -0.5 * LOG2_E)
    split_heads(aq_t, aqt_ref)
    iq_t = proj_t(_R_IQ, 512) * cat8 + proj_t(_R_IQR, 512) * sat8
    split_heads(iq_t, iqt_ref)
    iw_t = proj_t(_R_IW, 16)[0:IDX_HEADS] * ((IDX_HEADS * IDX_DIM) ** -0.5)
    for c in range(nqb):
        iwt_ref[c] = iw_t[:, c * ATT_TQ:(c + 1) * ATT_TQ]
    avt_ref[...] = proj_t(_R_V, 64).astype(BF16)

    kc = jnp.where(low64, ca, 1.0)
    ks = jnp.where(low64, sa, 0.0)
    akv_ref[...] = (proj_n(_C_KV, 128) * kc + proj_n(_C_KVR, 128) * ks).astype(BF16)

    y = proj_n(_C_IK, 128)
    mu = jnp.sum(y, axis=1, keepdims=True) * (1.0 / IDX_DIM)
    d = jnp.where(low64, y - mu, 0.0)
    var = jnp.sum(d * d, axis=1, keepdims=True) * (1.0 / IDX_DIM)
    yn = d * lax.rsqrt(var + LN_EPS) * ikg_ref[...] + ikb_ref[...]
    rot = jnp.where(lane < 32, -pltpu.roll(yn, LANES - 32, 1), pltpu.roll(yn, 32, 1))
    ik_ref[...] = jnp.where(low64, yn * ca + rot * sa, 0.0).astype(BF16)

    qc_t = proj_t(_R_QC, Q_LORA)
    qn_t = (qc_t * lax.rsqrt(jnp.mean(qc_t * qc_t, axis=0, keepdims=True) + RMS_EPS) * qg_ref[...]).astype(BF16)
    q_t = _dot(wqt_ref[...], qn_t)
    scale = (B_NOPE + B_ROPE) ** -0.5 * LOG2_E
    q_lat_t = (_dot(bdukt_ref[...], q_t[0:512].astype(BF16)) * scale).astype(BF16)
    crt8 = jnp.concatenate([crt_ref[...]] * 8, axis=0)
    srt8 = jnp.concatenate([srt_ref[...]] * 8, axis=0)
    q_r_t = ((q_t[512:768] * crt8 + q_t[768:1024] * srt8) * scale).astype(BF16)
    zeros = jnp.zeros((2 * LANES - KV_LORA - B_ROPE, ATT_TQ), BF16)
    for c in range(nqb):
        tok = slice(c * ATT_TQ, (c + 1) * ATT_TQ)
        for h in range(B_HEADS):
            col = slice(h * ATT_TQ, (h + 1) * ATT_TQ)
            qabst_ref[c, 0:KV_LORA, col] = q_lat_t[h * KV_LORA:(h + 1) * KV_LORA, tok]
            qabst_ref[c, KV_LORA:KV_LORA + B_ROPE, col] = q_r_t[h * B_ROPE:(h + 1) * B_ROPE, tok]
            qabst_ref[c, KV_LORA + B_ROPE:2 * LANES, col] = zeros

    kvc = proj_n(_C_KVC, KV_LORA)
    kvn = kvc * lax.rsqrt(jnp.mean(kvc * kvc, axis=1, keepdims=True) + RMS_EPS) * kvg_ref[...]
    kr = proj_n(_C_KR, 128) * cr + proj_n(_C_KRR, 128) * sr
    ckv_ref[...] = jnp.concatenate([kvn, kr], axis=1).astype(BF16)
    kvc_t = proj_t(_R_KVC, KV_LORA)
    latt_ref[...] = (kvc_t * lax.rsqrt(jnp.mean(kvc_t * kvc_t, axis=0, keepdims=True) + RMS_EPS)
                     * kvgt_ref[...]).astype(BF16)


def _proj(x, tabs, w_t, w_n, wq_t, bd_uk_t, ik_g, ik_b, q_g, kv_g):
    bsz, seq, d = x.shape
    tm = PROJ_TM
    nqb = tm // ATT_TQ
    tok = lambda b, i: (b, i, 0)
    feat = lambda b, i: (b, 0, i)
    qblk = lambda b, i: (b, i, 0, 0)
    const2 = lambda b, i: (0, 0)
    nat_spec = pl.BlockSpec((None, tm, LANES), tok)
    col = lambda v: jnp.broadcast_to(v.reshape(-1, 1), (v.shape[0], tm))
    ik_g = jnp.pad(ik_g, (0, LANES - IDX_DIM)).reshape(1, LANES)
    ik_b = jnp.pad(ik_b, (0, LANES - IDX_DIM)).reshape(1, LANES)
    consts = (w_t, w_n, wq_t, bd_uk_t, ik_g, ik_b, col(q_g), kv_g.reshape(1, KV_LORA), col(kv_g))
    return pl.pallas_call(
        _proj_kernel,
        grid=(bsz, seq // tm),
        in_specs=[pl.BlockSpec((None, tm, d), tok), nat_spec, nat_spec, nat_spec, nat_spec,
                  pl.BlockSpec((None, 64, tm), feat), pl.BlockSpec((None, 64, tm), feat),
                  pl.BlockSpec((None, 32, tm), feat), pl.BlockSpec((None, 32, tm), feat)]
                 + [pl.BlockSpec(c.shape, const2) for c in consts],
        out_specs=[pl.BlockSpec((None, nqb, LANES, A_HEADS * ATT_TQ), qblk),
                   pl.BlockSpec((None, nqb, LANES, IDX_HEADS * ATT_TQ), qblk),
                   pl.BlockSpec((None, nqb, IDX_HEADS, ATT_TQ), qblk),
                   pl.BlockSpec((None, tm, LANES), tok),
                   pl.BlockSpec((None, A_HEAD_DIM, tm), feat),
                   pl.BlockSpec((None, tm, LANES), tok),
                   pl.BlockSpec((None, nqb, 2 * LANES, B_HEADS * ATT_TQ), qblk),
                   pl.BlockSpec((None, tm, 2 * LANES), tok),
                   pl.BlockSpec((None, KV_LORA, tm), feat)],
        out_shape=[jax.ShapeDtypeStruct((bsz, seq // ATT_TQ, LANES, A_HEADS * ATT_TQ), BF16),
                   jax.ShapeDtypeStruct((bsz, seq // ATT_TQ, LANES, IDX_HEADS * ATT_TQ), BF16),
                   jax.ShapeDtypeStruct((bsz, seq // ATT_TQ, IDX_HEADS, ATT_TQ), F32),
                   jax.ShapeDtypeStruct((bsz, seq, LANES), BF16),
                   jax.ShapeDtypeStruct((bsz, A_HEAD_DIM, seq), BF16),
                   jax.ShapeDtypeStruct((bsz, seq, LANES), BF16),
                   jax.ShapeDtypeStruct((bsz, seq // ATT_TQ, 2 * LANES, B_HEADS * ATT_TQ), BF16),
                   jax.ShapeDtypeStruct((bsz, seq, 2 * LANES), BF16),
                   jax.ShapeDtypeStruct((bsz, KV_LORA, seq), BF16)],
        compiler_params=pltpu.CompilerParams(
            dimension_semantics=("arbitrary", "arbitrary"), vmem_limit_bytes=VMEM_LIMIT),
        name="proj",
    )(x, *tabs, *consts)


def _attend_init(nh, acc_rows):
    hpg = nh // ATT_GROUPS
    return ([jnp.full((1, LANES), M_FLOOR, F32) for _ in range(nh)],
            [jnp.zeros((SUBLANES, LANES), F32) for _ in range(nh)],
            [jnp.zeros((acc_rows, hpg * LANES), F32) for _ in range(ATT_GROUPS)])


def _attend_block(k_blk, vt_blk, qt_ref, keep, state):
    ms, ls, accs = state
    nh = len(ms)
    hpg = nh // ATT_GROUPS
    new_ms, new_ls, new_accs = [], [], []
    for g in range(ATT_GROUPS):
        logits = _dot(k_blk, qt_ref[:, g * hpg * LANES:(g + 1) * hpg * LANES])
        ps, alphas = [], []
        for i in range(hpg):
            h = g * hpg + i
            lh = logits[:, i * LANES:(i + 1) * LANES]
            if keep is not None:
                lh = jnp.where(keep, lh, MASKED)
            m_new = jnp.maximum(ms[h], _rows_reduce(_fold(lh, jnp.maximum), "max"))
            alpha = jnp.exp2(ms[h] - m_new)
            p = jnp.exp2(lh - m_new)
            new_ls.append(alpha * ls[h] + _fold(_fold(p, jnp.add), jnp.add, SUBLANES))
            new_ms.append(m_new)
            ps.append(p.astype(BF16))
            alphas.append(alpha)
        new_accs.append(jnp.concatenate(alphas, axis=1) * accs[g] + _dot(vt_blk, jnp.concatenate(ps, axis=1)))
    return new_ms, new_ls, new_accs


def _attend_finish(state):
    _, ls, accs = state
    hpg = len(ls) // ATT_GROUPS
    outs = []
    for g, acc in enumerate(accs):
        den = jnp.concatenate([_rows_reduce(ls[g * hpg + i], "sum") for i in range(hpg)], axis=1)
        outs.append(acc / den)
    return jnp.concatenate(outs, axis=1)


def _dsa_kernel(aqt_ref, iqt_ref, iwt_ref, akv_ref, avt_ref, ik_ref, out_ref, s_sc):
    nh = iwt_ref.shape[0]
    tq, tk = ATT_TQ, ATT_TK
    seq = akv_ref.shape[0]
    j = pl.program_id(1)
    nkb = (j * tq) // tk + 1
    iqt = iqt_ref[...]
    w = iwt_ref[...]
    krow = lax.broadcasted_iota(jnp.int32, (tk, tq), 0)
    qcol = j * tq + lax.broadcasted_iota(jnp.int32, (tk, tq), 1)
    krow_f = krow.astype(F32)

    def score_body(kb, carry):
        cmax, cmin = carry
        k0 = pl.multiple_of(kb * tk, tk)
        y = _dot(ik_ref[pl.ds(k0, tk), :], iqt)
        s = jnp.zeros((tk, tq), F32)
        for h in range(nh):
            s = s + w[h:h + 1, :] * jnp.maximum(y[:, h * tq:(h + 1) * tq], 0.0)
        causal = (k0 + krow) <= qcol
        s_lo = jnp.where(causal, s, -jnp.inf)
        s_sc[pl.ds(k0, tk), :] = s_lo
        cmax = jnp.maximum(cmax, _fold(s_lo, jnp.maximum))
        cmin = jnp.minimum(cmin, _fold(jnp.where(causal, s, jnp.inf), jnp.minimum))
        return cmax, cmin

    cmax, cmin = lax.fori_loop(0, nkb, score_body,
                               (jnp.full((64, tq), -jnp.inf, F32), jnp.full((64, tq), jnp.inf, F32)))
    rmax = _rows_reduce(cmax, "max")
    rmin = _rows_reduce(cmin, "min")

    n_causal = (j * tq + lax.broadcasted_iota(jnp.int32, (1, tq), 1) + 1).astype(F32)
    kt = jnp.minimum(n_causal, float(TOPK_MAX))

    def count(pred):
        def body(kb, acc):
            k0 = pl.multiple_of(kb * tk, tk)
            return acc + _fold(jnp.where(pred(s_sc[pl.ds(k0, tk), :], k0), 1.0, 0.0), jnp.add)
        return _rows_reduce(lax.fori_loop(0, nkb, body, jnp.zeros((64, tq), F32)), "sum")

    def propose(it, lo, hi, clo):
        mid = 0.5 * lo + 0.5 * hi
        mid = jnp.where(jnp.logical_and(it == 0, jnp.logical_and(lo < 0.0, hi > 0.0)), 0.0, mid)
        mid = jnp.where(jnp.logical_and(it == 1, jnp.logical_and(lo == 0.0, hi > TINY)), TINY, mid)
        active = jnp.logical_and(clo != kt, jnp.logical_and(mid > lo, mid < hi))
        return mid, active

    def sel_step(it, lo, hi, clo, chi):
        mid, active = propose(it, lo, hi, clo)
        cm = count(lambda blk, k0: blk >= mid)
        ge = cm >= kt
        up = jnp.logical_and(active, ge)
        dn = jnp.logical_and(active, jnp.logical_not(ge))
        return jnp.where(up, mid, lo), jnp.where(dn, mid, hi), jnp.where(up, cm, clo), jnp.where(dn, cm, chi)

    def sel_body(c):
        it, _, lo, hi, clo, chi = c
        for u in range(SEL_UNROLL):
            lo, hi, clo, chi = sel_step(it + u, lo, hi, clo, chi)
        _, active = propose(it + SEL_UNROLL, lo, hi, clo)
        go = jnp.logical_and(jnp.max(jnp.where(active, 1.0, 0.0)) > 0.0, it + SEL_UNROLL < SEL_MAX_ITERS)
        return it + SEL_UNROLL, go, lo, hi, clo, chi

    lo0 = rmin
    hi0 = rmax + jnp.abs(rmax) + 1e-30
    clo0, chi0 = n_causal, jnp.zeros((1, tq), F32)
    _, active0 = propose(0, lo0, hi0, clo0)
    go0 = jnp.max(jnp.where(active0, 1.0, 0.0)) > 0.0
    _, _, lo, hi, clo, chi = lax.while_loop(lambda c: c[1], sel_body, (jnp.int32(0), go0, lo0, hi0, clo0, chi0))

    tie = clo > kt

    @pl.when(jnp.max(jnp.where(tie, 1.0, 0.0)) > 0.0)
    def _():
        need = kt - chi

        def idx_body(_, c):
            jl, jh = c
            jm = jnp.floor((jl + jh) * 0.5)
            ok = count(lambda blk, k0: jnp.logical_and(blk == lo, krow_f + k0.astype(F32) <= jm)) >= need
            return jnp.where(ok, jl, jm), jnp.where(ok, jm, jh)

        n_steps = int(np.ceil(np.log2(seq))) + 1
        _, jh = lax.fori_loop(0, n_steps, idx_body,
                              (jnp.full((1, tq), -1.0, F32), jnp.full((1, tq), float(seq - 1), F32)))
        jh = jnp.where(tie, jh, float(seq))

        def fix_body(kb, _):
            k0 = pl.multiple_of(kb * tk, tk)
            blk = s_sc[pl.ds(k0, tk), :]
            drop = jnp.logical_and(blk == lo, krow_f + k0.astype(F32) > jh)
            s_sc[pl.ds(k0, tk), :] = jnp.where(drop, -jnp.inf, blk)
            return 0

        lax.fori_loop(0, nkb, fix_body, 0)

    def att_body(kb, state):
        k0 = pl.multiple_of(kb * tk, tk)
        keep = s_sc[pl.ds(k0, tk), :] >= lo
        return _attend_block(akv_ref[pl.ds(k0, tk), :], avt_ref[:, pl.ds(k0, tk)], aqt_ref, keep, state)

    o_t = _attend_finish(lax.fori_loop(0, nkb, att_body, _attend_init(nh, A_HEAD_DIM)))
    for p in range(nh // 2):
        pair = jnp.concatenate([o_t[:, (2 * p) * tq:(2 * p + 1) * tq], o_t[:, (2 * p + 1) * tq:(2 * p + 2) * tq]], axis=0)
        out_ref[:, p * LANES:(p + 1) * LANES] = pair.T.astype(out_ref.dtype)


def _dsa(aqt, iqt, iwt, akv, avt, ikp):
    bsz, nqb, _, hq = aqt.shape
    nh = iwt.shape[2]
    seq = akv.shape[1]
    tq = ATT_TQ
    qblk = lambda b, j: (b, j, 0, 0)
    full = lambda b, j: (b, 0, 0)
    return pl.pallas_call(
        _dsa_kernel,
        grid=(bsz, nqb),
        in_specs=[pl.BlockSpec((None, None, LANES, hq), qblk), pl.BlockSpec((None, None, LANES, hq), qblk),
                  pl.BlockSpec((None, None, nh, tq), qblk),
                  pl.BlockSpec((None, seq, LANES), full), pl.BlockSpec((None, A_HEAD_DIM, seq), full),
                  pl.BlockSpec((None, seq, LANES), full)],
        out_specs=pl.BlockSpec((None, tq, nh * A_HEAD_DIM), lambda b, j: (b, j, 0)),
        out_shape=jax.ShapeDtypeStruct((bsz, seq, nh * A_HEAD_DIM), BF16),
        scratch_shapes=[pltpu.VMEM((seq, tq), F32)],
        compiler_params=pltpu.CompilerParams(
            dimension_semantics=("arbitrary", "arbitrary"), vmem_limit_bytes=VMEM_LIMIT),
        name="dsa",
    )(aqt, iqt, iwt, akv, avt, ikp)


def _mla_kernel(qt_ref, ckv_ref, latt_ref, wuvt_ref, out_ref):
    nh = wuvt_ref.shape[0]
    tq, tk = ATT_TQ, ATT_TK
    j = pl.program_id(1)
    n_full = (j * tq) // tk

    def block(kb, state, masked):
        k0 = pl.multiple_of(kb * tk, tk)
        keep = None
        if masked:
            krow = k0 + lax.broadcasted_iota(jnp.int32, (tk, tq), 0)
            keep = krow <= j * tq + lax.broadcasted_iota(jnp.int32, (tk, tq), 1)
        return _attend_block(ckv_ref[pl.ds(k0, tk), :], latt_ref[:, pl.ds(k0, tk)], qt_ref, keep, state)

    state = lax.fori_loop(0, n_full, lambda kb, st: block(kb, st, False), _attend_init(nh, KV_LORA))
    o_lat_t = _attend_finish(block(n_full, state, True)).astype(BF16)
    for p in range(nh // 2):
        pair = jnp.concatenate([_dot(wuvt_ref[h], o_lat_t[:, h * tq:(h + 1) * tq]) for h in (2 * p, 2 * p + 1)], axis=0)
        out_ref[:, p * LANES:(p + 1) * LANES] = pair.T.astype(out_ref.dtype)


def _mla(qabst, ckv, latt, w_uv_t):
    bsz, nqb, dq, hq = qabst.shape
    nh = w_uv_t.shape[0]
    seq = ckv.shape[1]
    tq = ATT_TQ
    return pl.pallas_call(
        _mla_kernel,
        grid=(bsz, nqb),
        in_specs=[pl.BlockSpec((None, None, dq, hq), lambda b, j: (b, j, 0, 0)),
                  pl.BlockSpec((None, seq, dq), lambda b, j: (b, 0, 0)),
                  pl.BlockSpec((None, KV_LORA, seq), lambda b, j: (b, 0, 0)),
                  pl.BlockSpec(w_uv_t.shape, lambda b, j: (0, 0, 0))],
        out_specs=pl.BlockSpec((None, tq, nh * B_V), lambda b, j: (b, j, 0)),
        out_shape=jax.ShapeDtypeStruct((bsz, seq, nh * B_V), BF16),
        compiler_params=pltpu.CompilerParams(
            dimension_semantics=("arbitrary", "arbitrary"), vmem_limit_bytes=VMEM_LIMIT),
        name="mla",
    )(qabst, ckv, latt, w_uv_t)


def _layer_norm(z, g, b):
    mu = jnp.mean(z, axis=1, keepdims=True)
    d = z - mu
    var = jnp.mean(d * d, axis=1, keepdims=True)
    return d * lax.rsqrt(var + LN_EPS) * g + b


def _split_bf16(v):
    hi = v.astype(BF16)
    return hi, (v - hi.astype(F32)).astype(BF16)


def _to_row_tiles(v, ref):
    for c in range(v.shape[1] // LANES):
        ref[:, c, :] = v[:, c * LANES:(c + 1) * LANES]


def _from_row_tiles(ref):
    return jnp.concatenate([ref[:, c, :] for c in range(ref.shape[1])], axis=1)


def _mix_kernel(x_ref, a_ref, b_ref, wout_ref, g_ref, beta_ref, wr_ref, br_ref,
                h_ref, ti_ref, tg_ref, rank_ref, cnt_ref, carry_sc, *, alpha):
    tm = x_ref.shape[0]
    half = a_ref.shape[1]
    i = pl.program_id(0)

    @pl.when(i == 0)
    def _():
        carry_sc[...] = jnp.zeros(carry_sc.shape, F32)

    mix = _dot(a_ref[...], wout_ref[0:half, :]) + _dot(b_ref[...], wout_ref[half:2 * half, :])
    h = _layer_norm(alpha * x_ref[...] + mix, g_ref[...], beta_ref[...])
    _to_row_tiles(h, h_ref)
    h_hi, h_lo = _split_bf16(h)
    w_hi, w_lo = _split_bf16(wr_ref[...])
    logits = _dot(h_hi, w_hi) + (_dot(h_lo, w_hi) + _dot(h_hi, w_lo)) + br_ref[...]
    lane = _lane_iota((tm, LANES))
    logits = jnp.where(lane < N_EXPERTS, logits, -jnp.inf)
    ids = jnp.zeros((tm, LANES), jnp.int32)
    vals, hots = [], []
    for k in range(TOP_K):
        m = jnp.max(logits, axis=1, keepdims=True)
        idx = jnp.min(jnp.where(logits == m, lane, LANES), axis=1, keepdims=True)
        hot = lane == idx
        ids = jnp.where(lane == k, idx, ids)
        logits = jnp.where(hot, -jnp.inf, logits)
        vals.append(m)
        hots.append(hot)
    es = [jnp.exp(v - vals[0]) for v in vals]
    den = (es[0] + es[1]) + (es[2] + es[3])
    gates = jnp.zeros((tm, LANES), F32)
    for k in range(TOP_K):
        gates = jnp.where(lane == k, es[k] / den, gates)
    ti_ref[...] = ids[:, :TOP_K]
    tg_ref[...] = gates[:, :TOP_K]

    onehot = jnp.where(jnp.logical_or(jnp.logical_or(hots[0], hots[1]), jnp.logical_or(hots[2], hots[3])), 1.0, 0.0)
    r_i = lax.broadcasted_iota(jnp.int32, (tm, tm), 0)
    c_i = lax.broadcasted_iota(jnp.int32, (tm, tm), 1)
    before = _dot(jnp.where(c_i < r_i, 1.0, 0.0).astype(BF16), onehot.astype(BF16)) + carry_sc[...]
    ranks = jnp.zeros((tm, LANES), F32)
    for k in range(TOP_K):
        ranks = jnp.where(lane == k, jnp.sum(jnp.where(hots[k], before, 0.0), axis=1, keepdims=True), ranks)
    rank_ref[...] = ranks[:, :TOP_K].astype(jnp.int32)
    carry_sc[...] = carry_sc[...] + jnp.sum(onehot, axis=0, keepdims=True)
    cnt_ref[...] = carry_sc[...].astype(jnp.int32)


def _mix(x2, a_out, b_out, w_out, g, beta, w_router, b_router, alpha):
    n, d = x2.shape
    tm = MIX_TM
    half = a_out.shape[1]
    row = lambda i: (i, 0)
    const = lambda i: (0, 0)
    wr = jnp.pad(w_router, ((0, 0), (0, LANES - N_EXPERTS)))
    br = jnp.pad(b_router, (0, LANES - N_EXPERTS)).reshape(1, LANES)
    return pl.pallas_call(
        functools.partial(_mix_kernel, alpha=alpha),
        grid=(n // tm,),
        in_specs=[pl.BlockSpec((tm, d), row), pl.BlockSpec((tm, half), row), pl.BlockSpec((tm, half), row),
                  pl.BlockSpec(w_out.shape, const), pl.BlockSpec((1, d), const), pl.BlockSpec((1, d), const),
                  pl.BlockSpec(wr.shape, const), pl.BlockSpec((1, LANES), const)],
        out_specs=[pl.BlockSpec((tm, d // LANES, LANES), lambda i: (i, 0, 0)), pl.BlockSpec((tm, TOP_K), row),
                   pl.BlockSpec((tm, TOP_K), row), pl.BlockSpec((tm, TOP_K), row), pl.BlockSpec((1, LANES), const)],
        out_shape=[jax.ShapeDtypeStruct((n, d // LANES, LANES), F32), jax.ShapeDtypeStruct((n, TOP_K), jnp.int32),
                   jax.ShapeDtypeStruct((n, TOP_K), F32), jax.ShapeDtypeStruct((n, TOP_K), jnp.int32),
                   jax.ShapeDtypeStruct((1, LANES), jnp.int32)],
        scratch_shapes=[pltpu.VMEM((1, LANES), F32)],
        compiler_params=pltpu.CompilerParams(dimension_semantics=("arbitrary",), vmem_limit_bytes=VMEM_LIMIT),
        name="mix",
    )(x2, a_out, b_out, w_out.astype(BF16), g.reshape(1, d), beta.reshape(1, d), wr, br)


def _route(top_i, rank, counts, tm):
    n = top_i.shape[0]
    n_tiles = (n * TOP_K) // tm + N_EXPERTS
    counts = counts[0, :N_EXPERTS]
    tiles_per = (counts + tm - 1) // tm
    tile_end = jnp.cumsum(tiles_per)
    pstart = (tile_end - tiles_per) * tm
    pos = rank + jnp.sum(jnp.where(top_i[..., None] == jnp.arange(N_EXPERTS, dtype=jnp.int32), pstart, 0), axis=-1)
    tile_ids = jnp.arange(n_tiles, dtype=jnp.int32)
    n_valid = tile_end[-1]
    tile_expert = jnp.sum(tile_end[None, :] <= jnp.minimum(tile_ids, n_valid - 1)[:, None], axis=1).astype(jnp.int32)
    tile_src = jnp.minimum(tile_ids, n_valid - 1)
    tile_valid = (tile_ids < n_valid).astype(jnp.int32)
    pad_start = jnp.concatenate([pstart + counts, n_valid[None]]).astype(jnp.int32)
    return pos.astype(jnp.int32), tile_expert, tile_src.astype(jnp.int32), tile_valid, pad_start, n_tiles


def _dispatch_kernel(pad_ref, h_ref, pos_hbm, xs_hbm, pos_s, zbuf, sem_idx, sem_row):
    i = pl.program_id(0)
    tm = h_ref.shape[0]

    @pl.when(i == 0)
    def _():
        zt = zbuf.shape[0]
        zbuf[...] = jnp.zeros(zbuf.shape, zbuf.dtype)
        for e in range(N_EXPERTS):
            cpe = pltpu.make_async_copy(zbuf, xs_hbm.at[pl.ds(pad_ref[e], zt)], sem_row)
            cpe.start()
            cpe.wait()

        def fill(t, _):
            cpz = pltpu.make_async_copy(zbuf, xs_hbm.at[pl.ds(pl.multiple_of(t * zt, zt), zt)], sem_row)
            cpz.start()
            cpz.wait()
            return 0

        lax.fori_loop(pad_ref[N_EXPERTS], xs_hbm.shape[0] // zt, fill, 0)

    cp = pltpu.make_async_copy(pos_hbm.at[i], pos_s, sem_idx)
    cp.start()
    cp.wait()

    def body(r, _):
        for s in range(TOP_K):
            pltpu.make_async_copy(h_ref.at[r], xs_hbm.at[pos_s[r * TOP_K + s]], sem_row).start()
        return 0

    lax.fori_loop(0, tm, body, 0, unroll=8)
    for s in range(TOP_K):
        pltpu.make_async_copy(h_ref, xs_hbm.at[pl.ds(0, tm)], sem_row).wait()


def _dispatch(h3, pos, pad_start, n_rows, zrows):
    n, nt8, _ = h3.shape
    tm = ROW_TM
    return pl.pallas_call(
        _dispatch_kernel,
        grid_spec=pltpu.PrefetchScalarGridSpec(
            num_scalar_prefetch=1,
            grid=(n // tm,),
            in_specs=[pl.BlockSpec((tm, nt8, LANES), lambda i, pad: (i, 0, 0)), pl.BlockSpec(memory_space=pl.ANY)],
            out_specs=pl.BlockSpec(memory_space=pl.ANY),
            scratch_shapes=[pltpu.SMEM((tm * TOP_K,), jnp.int32), pltpu.VMEM((zrows, nt8, LANES), F32),
                            pltpu.SemaphoreType.DMA, pltpu.SemaphoreType.DMA]),
        out_shape=jax.ShapeDtypeStruct((n_rows + zrows, nt8, LANES), F32),
        compiler_params=pltpu.CompilerParams(dimension_semantics=("arbitrary",), vmem_limit_bytes=VMEM_LIMIT),
        name="dispatch",
    )(pad_start, h3, pos.reshape(n // tm, tm * TOP_K))


def _moe_kernel(te_ref, ts_ref, tv_ref, xs_ref, wg_ref, bg_ref, wu_ref, bu_ref, wd_ref, bd_ref, ys_ref):
    del te_ref, ts_ref
    i = pl.program_id(0)

    @pl.when(tv_ref[i] > 0)
    def _():
        x = _from_row_tiles(xs_ref).astype(BF16)
        g = jnp.minimum(_dot(x, wg_ref[0]) + bg_ref[0], SWIGLU_LIMIT)
        u = jnp.clip(_dot(x, wu_ref[0]) + bu_ref[0], -SWIGLU_LIMIT, SWIGLU_LIMIT)
        act = g * (1.0 / (1.0 + jnp.exp(-SWIGLU_ALPHA * g))) * (u + 1.0)
        _to_row_tiles(_dot(act.astype(BF16), wd_ref[0]) + bd_ref[0], ys_ref)

    @pl.when(tv_ref[i] == 0)
    def _():
        ys_ref[...] = jnp.zeros(ys_ref.shape, ys_ref.dtype)


def _moe(xs, tile_expert, tile_src, tile_valid, n_tiles, w_gate, b_gate, w_up, b_up, w_down, b_down):
    _, nt8, _ = xs.shape
    tm = MOE_TM
    ne, d, dff = w_gate.shape
    wmap = lambda i, te, ts, tv: (te[i], 0, 0)
    grid_spec = pltpu.PrefetchScalarGridSpec(
        num_scalar_prefetch=3,
        grid=(n_tiles,),
        in_specs=[pl.BlockSpec((tm, nt8, LANES), lambda i, te, ts, tv: (ts[i], 0, 0)),
                  pl.BlockSpec((1, d, dff), wmap), pl.BlockSpec((1, 1, dff), wmap),
                  pl.BlockSpec((1, d, dff), wmap), pl.BlockSpec((1, 1, dff), wmap),
                  pl.BlockSpec((1, dff, d), wmap), pl.BlockSpec((1, 1, d), wmap)],
        out_specs=pl.BlockSpec((tm, nt8, LANES), lambda i, te, ts, tv: (i, 0, 0)))
    return pl.pallas_call(
        _moe_kernel,
        grid_spec=grid_spec,
        out_shape=jax.ShapeDtypeStruct((n_tiles * tm, nt8, LANES), F32),
        compiler_params=pltpu.CompilerParams(dimension_semantics=("arbitrary",), vmem_limit_bytes=VMEM_LIMIT),
        name="moe",
    )(tile_expert, tile_src, tile_valid, xs,
      w_gate.astype(BF16), b_gate.reshape(ne, 1, dff), w_up.astype(BF16), b_up.reshape(ne, 1, dff),
      w_down.astype(BF16), b_down.reshape(ne, 1, d))


def _combine_kernel(h_ref, gate_ref, pos_hbm, ys_hbm, g_ref, beta_ref, o_ref, pos_s, ybuf, sem_idx, sem_row, *, alpha):
    i = pl.program_id(0)
    tm = h_ref.shape[0]
    cp = pltpu.make_async_copy(pos_hbm.at[i], pos_s, sem_idx)
    cp.start()
    cp.wait()

    def body(r, _):
        for s in range(TOP_K):
            pltpu.make_async_copy(ys_hbm.at[pos_s[r * TOP_K + s]], ybuf.at[s, r], sem_row).start()
        return 0

    lax.fori_loop(0, tm, body, 0, unroll=8)
    for s in range(TOP_K):
        pltpu.make_async_copy(ys_hbm.at[pl.ds(0, tm)], ybuf.at[s], sem_row).wait()

    gates = gate_ref[...]
    ffn = None
    for s in range(TOP_K):
        term = gates[:, s:s + 1] * _from_row_tiles(ybuf.at[s])
        ffn = term if ffn is None else ffn + term
    o_ref[...] = _layer_norm(alpha * _from_row_tiles(h_ref) + ffn, g_ref[...], beta_ref[...])


def _combine(h3, gates, pos, ys, g, beta, alpha):
    n, nt8, _ = h3.shape
    d = nt8 * LANES
    tm = ROW_TM
    const = lambda i: (0, 0)
    return pl.pallas_call(
        functools.partial(_combine_kernel, alpha=alpha),
        grid=(n // tm,),
        in_specs=[pl.BlockSpec((tm, nt8, LANES), lambda i: (i, 0, 0)), pl.BlockSpec((tm, TOP_K), lambda i: (i, 0)),
                  pl.BlockSpec(memory_space=pl.ANY), pl.BlockSpec(memory_space=pl.ANY),
                  pl.BlockSpec((1, d), const), pl.BlockSpec((1, d), const)],
        out_specs=pl.BlockSpec((tm, d), lambda i: (i, 0)),
        out_shape=jax.ShapeDtypeStruct((n, d), F32),
        scratch_shapes=[pltpu.SMEM((tm * TOP_K,), jnp.int32), pltpu.VMEM((TOP_K, tm, nt8, LANES), F32),
                        pltpu.SemaphoreType.DMA, pltpu.SemaphoreType.DMA],
        compiler_params=pltpu.CompilerParams(dimension_semantics=("arbitrary",), vmem_limit_bytes=VMEM_LIMIT),
        name="combine",
    )(h3, gates, pos.reshape(n // tm, tm * TOP_K), ys, g.reshape(1, d), beta.reshape(1, d))


def _layer(x, tabs, alpha, w_in, ik_g, ik_b, q_g, w_q_up, kv_g, w_kv_up, w_out, ln1_g, ln1_b,
           w_router, b_router, w_gate, b_gate, w_up, b_up, w_down, b_down, ln2_g, ln2_b):
    bsz, seq, d = x.shape
    w_t, w_n, wq_t, bd_uk_t, w_uv_t = _pack_weights(w_in, w_q_up, w_kv_up)
    aqt, iqt, iwt, akv, avt, ikp, qabst, ckv, latt = _proj(x, tabs, w_t, w_n, wq_t, bd_uk_t, ik_g, ik_b, q_g, kv_g)
    a_out = _dsa(aqt, iqt, iwt, akv, avt, ikp)
    b_out = _mla(qabst, ckv, latt, w_uv_t)
    n = bsz * seq
    h3, top_i, gates, rank, counts = _mix(x.reshape(n, d), a_out.reshape(n, -1), b_out.reshape(n, -1), w_out,
                                          ln1_g, ln1_b, w_router, b_router, alpha)
    pos, tile_expert, tile_src, tile_valid, pad_start, n_tiles = _route(top_i, rank, counts, MOE_TM)
    xs = _dispatch(h3, pos, pad_start, n_tiles * MOE_TM, MOE_TM)
    ys = _moe(xs, tile_expert, tile_src, tile_valid, n_tiles, w_gate, b_gate, w_up, b_up, w_down, b_down)
    return _combine(h3, gates, pos, ys, ln2_g, ln2_b, alpha).reshape(bsz, seq, d)


def kernel(x, positions, w_mix_in, idx_k_norm_g, idx_k_norm_b, mla_q_norm_g, mla_w_q_up, mla_kv_norm_g, mla_w_kv_up, w_mix_out, ln1_g, ln1_b, w_router, b_router, w_gate, b_gate, w_up, b_up, w_down, b_down, ln2_g, ln2_b):
    depth = w_mix_in.shape[0]
    alpha = float((2 * depth) ** 0.25)
    tabs = _rope_tables(positions)
    for l in range(depth):
        x = _layer(x, tabs, alpha, w_mix_in[l], idx_k_norm_g[l], idx_k_norm_b[l], mla_q_norm_g[l], mla_w_q_up[l],
                   mla_kv_norm_g[l], mla_w_kv_up[l], w_mix_out[l], ln1_g[l], ln1_b[l], w_router[l], b_router[l],
                   w_gate[l], b_gate[l], w_up[l], b_up[l], w_down[l], b_down[l], ln2_g[l], ln2_b[l])
    return x
```

```python
import functools

import jax
import jax.numpy as jnp
import numpy as np
from jax import lax
from jax.experimental import pallas as pl
from jax.experimental.pallas import tpu as pltpu

F32 = jnp.float32
BF16 = jnp.bfloat16

A_HEADS, A_HEAD_DIM = 8, 64
IDX_HEADS, IDX_DIM = 8, 64
TOPK_MAX = 256
B_HEADS, B_NOPE, B_ROPE, B_V = 8, 64, 32, 64
Q_LORA, KV_LORA = 256, 128
N_EXPERTS, TOP_K = 32, 4
SWIGLU_LIMIT, SWIGLU_ALPHA = 7.0, 1.702
ROPE_THETA = 10000.0
LN_EPS, RMS_EPS = 1e-5, 1e-6

LANES = 128
SUBLANES = 8
VMEM_LIMIT = 56 * 1024 * 1024

PROJ_TM = 512
ATT_TQ = LANES
ATT_TK = 1024
SEL_TK = 512
ONES_ROWS = 16
ATT_GROUPS = 1
LOG2_E = 1.4426950408889634
MIX_TM = 512
MOE_TM = 256
ROW_TM = 256
SEL_MAX_ITERS = 400
SEL_UNROLL = 4

MASKED = -1e30
M_FLOOR = -1e29
TINY = float(np.finfo(np.float32).tiny)


def _dot(a, b):
    return jnp.dot(a, b, preferred_element_type=F32)


def _dot_t(a, b):
    return lax.dot_general(a, b, (((1,), (1,)), ((), ())), preferred_element_type=F32)


def _lane_iota(shape):
    return lax.broadcasted_iota(jnp.int32, shape, len(shape) - 1)


def _fold(v, op, rows=64):
    n = v.shape[0] // rows
    acc = v[0:rows]
    for i in range(1, n):
        acc = op(acc, v[i * rows:(i + 1) * rows])
    return acc


def _rows_reduce(v, op):
    if op == "sum":
        return jnp.sum(v, axis=0, keepdims=True)
    if op == "max":
        return jnp.max(v, axis=0, keepdims=True)
    return jnp.min(v, axis=0, keepdims=True)


_R_AQ, _R_AQR, _R_IQ, _R_IQR, _R_V, _R_IW, _R_QC, _R_KVC, _R_TOTAL = 0, 512, 1024, 1536, 2048, 2112, 2128, 2384, 2512
_C_KV, _C_KVR, _C_IK, _C_KVC, _C_KR, _C_KRR, _C_TOTAL = 0, 128, 256, 384, 512, 640, 768


def _rot_cols(w, dim):
    k, n = w.shape
    w3 = w.reshape(k, n // dim, dim)
    return jnp.concatenate([-w3[..., dim // 2:], w3[..., :dim // 2]], axis=-1).reshape(k, n)


def _pad_cols(w, width):
    return jnp.pad(w, ((0, 0), (0, width - w.shape[1])))


def _pack_weights(w_in, w_q_up, w_kv_up):
    o = np.cumsum((0, 512, 64, 64, 512, 64, 8, 256, 128, 32))
    a_q, a_k, a_v, i_q, i_k, i_w, b_qc, b_kvc, b_kr = [w_in[:, o[i]:o[i + 1]] for i in range(9)]
    w_t = jnp.concatenate([a_q, _rot_cols(a_q, 64), i_q, _rot_cols(i_q, 64), a_v, _pad_cols(i_w, 16),
                           b_qc, b_kvc], axis=1).T
    assert w_t.shape[0] == _R_TOTAL
    w_n = jnp.concatenate([a_k, a_v, _rot_cols(a_k, 64), jnp.zeros_like(a_k), _pad_cols(i_k, 128), b_kvc,
                           _pad_cols(b_kr, 128), _pad_cols(_rot_cols(b_kr, 32), 128)], axis=1)
    assert w_n.shape[1] == _C_TOTAL
    wq = w_q_up.reshape(Q_LORA, B_HEADS, B_NOPE + B_ROPE)
    wq_rope = wq[..., B_NOPE:].reshape(Q_LORA, B_HEADS * B_ROPE)
    wq_t = jnp.concatenate([wq[..., :B_NOPE].reshape(Q_LORA, B_HEADS * B_NOPE), wq_rope,
                            _rot_cols(wq_rope, B_ROPE)], axis=1).T
    wkv = w_kv_up.reshape(KV_LORA, B_HEADS, B_NOPE + B_V)
    eye = jnp.eye(B_HEADS, dtype=w_kv_up.dtype)
    bd_uk_t = jnp.einsum('lhd,hg->glhd', wkv[..., :B_NOPE], eye).reshape(B_HEADS * KV_LORA, B_HEADS * B_NOPE)
    w_uv_t = jnp.transpose(wkv[..., B_NOPE:], (1, 2, 0))
    return w_t.astype(BF16), w_n.astype(BF16), wq_t.astype(BF16), bd_uk_t.astype(BF16), w_uv_t.astype(BF16)


def _rope_tables(positions):
    pos = positions.astype(F32)[..., None]
    inv64 = ROPE_THETA ** (-jnp.arange(0, 64, 2, dtype=F32) / 64)
    inv32 = ROPE_THETA ** (-jnp.arange(0, 32, 2, dtype=F32) / 32)
    c64, s64 = jnp.cos(pos * inv64), jnp.sin(pos * inv64)
    c32, s32 = jnp.cos(pos * inv32), jnp.sin(pos * inv32)
    nat = (jnp.tile(c64, 4), jnp.tile(s64, 4), jnp.tile(c32, 8), jnp.tile(s32, 8))
    tr = tuple(jnp.swapaxes(jnp.tile(t, 2), 1, 2) for t in (c64, s64, c32, s32))
    return nat + tr


def _proj_kernel(x_ref, ca_ref, sa_ref, cr_ref, sr_ref, cat_ref, sat_ref, crt_ref, srt_ref,
                 wt_ref, wn_ref, wqt_ref, bdukt_ref, ikg_ref, ikb_ref, qg_ref, kvg_ref, kvgt_ref,
                 aqt_ref, iqt_ref, iwt_ref, akv_ref, avt_ref, ik_ref, qabst_ref, ckv_ref, latt_ref):
    tm = x_ref.shape[0]
    nqb = tm // ATT_TQ
    xb = x_ref[...].astype(BF16)
    ca, sa, cr, sr = ca_ref[...], sa_ref[...], cr_ref[...], sr_ref[...]
    lane = _lane_iota((tm, LANES))
    low64 = lane < 64

    def proj_t(r0, rows):
        return _dot_t(wt_ref[r0:r0 + rows, :], xb)

    def proj_n(c0, width):
        return _dot(xb, wn_ref[:, c0:c0 + width])

    cat8 = jnp.concatenate([cat_ref[...]] * 8, axis=0)
    sat8 = jnp.concatenate([sat_ref[...]] * 8, axis=0)

    def split_heads(v, out_ref):
        vb = v.astype(BF16)
        zeros = jnp.zeros((64, ATT_TQ), BF16)
        for c in range(nqb):
            for h in range(8):
                out_ref[c, 0:64, h * ATT_TQ:(h + 1) * ATT_TQ] = vb[64 * h:64 * h + 64, c * ATT_TQ:(c + 1) * ATT_TQ]
                out_ref[c, 64:128, h * ATT_TQ:(h + 1) * ATT_TQ] = zeros

    aq_t = (proj_t(_R_AQ, 512) * cat8 + proj_t(_R_AQR, 512) * sat8) * (A_HEAD_DIM ** -0.5 * LOG2_E)
    split_heads(aq_t, aqt_ref)
    iq_t = proj_t(_R_IQ, 512) * cat8 + proj_t(_R_IQR, 512) * sat8
    split_heads(iq_t, iqt_ref)
    iw_t = proj_t(_R_IW, 16)[0:IDX_HEADS] * ((IDX_HEADS * IDX_DIM) ** -0.5)
    for c in range(nqb):
        iwt_ref[c] = iw_t[:, c * ATT_TQ:(c + 1) * ATT_TQ]
    ones_rows = jnp.where(lax.broadcasted_iota(jnp.int32, (ONES_ROWS, tm), 0) == 0, 1.0, 0.0).astype(BF16)
    avt_ref[...] = jnp.concatenate([proj_t(_R_V, 64).astype(BF16), ones_rows], axis=0)

    kc = jnp.where(low64, ca, 1.0)
    ks = jnp.where(low64, sa, 0.0)
    akv_ref[...] = (proj_n(_C_KV, 128) * kc + proj_n(_C_KVR, 128) * ks).astype(BF16)

    y = proj_n(_C_IK, 128)
    mu = jnp.sum(y, axis=1, keepdims=True) * (1.0 / IDX_DIM)
    d = jnp.where(low64, y - mu, 0.0)
    var = jnp.sum(d * d, axis=1, keepdims=True) * (1.0 / IDX_DIM)
    yn = d * lax.rsqrt(var + LN_EPS) * ikg_ref[...] + ikb_ref[...]
    rot = jnp.where(lane < 32, -pltpu.roll(yn, LANES - 32, 1), pltpu.roll(yn, 32, 1))
    ik_ref[...] = jnp.where(low64, yn * ca + rot * sa, 0.0).astype(BF16)

    qc_t = proj_t(_R_QC, Q_LORA)
    qn_t = (qc_t * lax.rsqrt(jnp.mean(qc_t * qc_t, axis=0, keepdims=True) + RMS_EPS) * qg_ref[...]).astype(BF16)
    q_t = _dot(wqt_ref[...], qn_t)
    scale = (B_NOPE + B_ROPE) ** -0.5 * LOG2_E
    q_lat_t = (_dot(bdukt_ref[...], q_t[0:512].astype(BF16)) * scale).astype(BF16)
    crt8 = jnp.concatenate([crt_ref[...]] * 8, axis=0)
    srt8 = jnp.concatenate([srt_ref[...]] * 8, axis=0)
    q_r_t = ((q_t[512:768] * crt8 + q_t[768:1024] * srt8) * scale).astype(BF16)
    zeros = jnp.zeros((2 * LANES - KV_LORA - B_ROPE, ATT_TQ), BF16)
    for c in range(nqb):
        tok = slice(c * ATT_TQ, (c + 1) * ATT_TQ)
        for h in range(B_HEADS):
            col = slice(h * ATT_TQ, (h + 1) * ATT_TQ)
            qabst_ref[c, 0:KV_LORA, col] = q_lat_t[h * KV_LORA:(h + 1) * KV_LORA, tok]
            qabst_ref[c, KV_LORA:KV_LORA + B_ROPE, col] = q_r_t[h * B_ROPE:(h + 1) * B_ROPE, tok]
            qabst_ref[c, KV_LORA + B_ROPE:2 * LANES, col] = zeros

    kvc = proj_n(_C_KVC, KV_LORA)
    kvn = kvc * lax.rsqrt(jnp.mean(kvc * kvc, axis=1, keepdims=True) + RMS_EPS) * kvg_ref[...]
    kr = proj_n(_C_KR, 128) * cr + proj_n(_C_KRR, 128) * sr
    ckv_ref[...] = jnp.concatenate([kvn, kr], axis=1).astype(BF16)
    kvc_t = proj_t(_R_KVC, KV_LORA)
    lat_t = (kvc_t * lax.rsqrt(jnp.mean(kvc_t * kvc_t, axis=0, keepdims=True) + RMS_EPS) * kvgt_ref[...]).astype(BF16)
    latt_ref[...] = jnp.concatenate([lat_t, ones_rows], axis=0)


def _proj(x, tabs, w_t, w_n, wq_t, bd_uk_t, ik_g, ik_b, q_g, kv_g):
    bsz, seq, d = x.shape
    tm = PROJ_TM
    nqb = tm // ATT_TQ
    tok = lambda b, i: (b, i, 0)
    feat = lambda b, i: (b, 0, i)
    qblk = lambda b, i: (b, i, 0, 0)
    const2 = lambda b, i: (0, 0)
    nat_spec = pl.BlockSpec((None, tm, LANES), tok)
    col = lambda v: jnp.broadcast_to(v.reshape(-1, 1), (v.shape[0], tm))
    ik_g = jnp.pad(ik_g, (0, LANES - IDX_DIM)).reshape(1, LANES)
    ik_b = jnp.pad(ik_b, (0, LANES - IDX_DIM)).reshape(1, LANES)
    consts = (w_t, w_n, wq_t, bd_uk_t, ik_g, ik_b, col(q_g), kv_g.reshape(1, KV_LORA), col(kv_g))
    return pl.pallas_call(
        _proj_kernel,
        grid=(bsz, seq // tm),
        in_specs=[pl.BlockSpec((None, tm, d), tok), nat_spec, nat_spec, nat_spec, nat_spec,
                  pl.BlockSpec((None, 64, tm), feat), pl.BlockSpec((None, 64, tm), feat),
                  pl.BlockSpec((None, 32, tm), feat), pl.BlockSpec((None, 32, tm), feat)]
                 + [pl.BlockSpec(c.shape, const2) for c in consts],
        out_specs=[pl.BlockSpec((None, nqb, LANES, A_HEADS * ATT_TQ), qblk),
                   pl.BlockSpec((None, nqb, LANES, IDX_HEADS * ATT_TQ), qblk),
                   pl.BlockSpec((None, nqb, IDX_HEADS, ATT_TQ), qblk),
                   pl.BlockSpec((None, tm, LANES), tok),
                   pl.BlockSpec((None, A_HEAD_DIM + ONES_ROWS, tm), feat),
                   pl.BlockSpec((None, tm, LANES), tok),
                   pl.BlockSpec((None, nqb, 2 * LANES, B_HEADS * ATT_TQ), qblk),
                   pl.BlockSpec((None, tm, 2 * LANES), tok),
                   pl.BlockSpec((None, KV_LORA + ONES_ROWS, tm), feat)],
        out_shape=[jax.ShapeDtypeStruct((bsz, seq // ATT_TQ, LANES, A_HEADS * ATT_TQ), BF16),
                   jax.ShapeDtypeStruct((bsz, seq // ATT_TQ, LANES, IDX_HEADS * ATT_TQ), BF16),
                   jax.ShapeDtypeStruct((bsz, seq // ATT_TQ, IDX_HEADS, ATT_TQ), F32),
                   jax.ShapeDtypeStruct((bsz, seq, LANES), BF16),
                   jax.ShapeDtypeStruct((bsz, A_HEAD_DIM + ONES_ROWS, seq), BF16),
                   jax.ShapeDtypeStruct((bsz, seq, LANES), BF16),
                   jax.ShapeDtypeStruct((bsz, seq // ATT_TQ, 2 * LANES, B_HEADS * ATT_TQ), BF16),
                   jax.ShapeDtypeStruct((bsz, seq, 2 * LANES), BF16),
                   jax.ShapeDtypeStruct((bsz, KV_LORA + ONES_ROWS, seq), BF16)],
        compiler_params=pltpu.CompilerParams(
            dimension_semantics=("arbitrary", "arbitrary"), vmem_limit_bytes=VMEM_LIMIT),
        name="proj",
    )(x, *tabs, *consts)


def _attend_init(nh, val_rows):
    hpg = nh // ATT_GROUPS
    return ([jnp.full((1, LANES), M_FLOOR, F32) for _ in range(nh)],
            [jnp.zeros((val_rows + ONES_ROWS, hpg * LANES), F32) for _ in range(ATT_GROUPS)])


def _attend_block(k_blk, vt_blk, qt_ref, keep, state):
    ms, accs = state
    nh = len(ms)
    hpg = nh // ATT_GROUPS
    new_ms, new_accs = [], []
    for g in range(ATT_GROUPS):
        logits = _dot(k_blk, qt_ref[:, g * hpg * LANES:(g + 1) * hpg * LANES])
        ps, alphas = [], []
        for i in range(hpg):
            h = g * hpg + i
            lh = logits[:, i * LANES:(i + 1) * LANES]
            if keep is not None:
                lh = jnp.where(keep, lh, MASKED)
            m_new = jnp.maximum(ms[h], _rows_reduce(_fold(lh, jnp.maximum), "max"))
            alphas.append(jnp.exp2(ms[h] - m_new))
            ps.append(jnp.exp2(lh - m_new).astype(BF16))
            new_ms.append(m_new)
        new_accs.append(jnp.concatenate(alphas, axis=1) * accs[g] + _dot(vt_blk, jnp.concatenate(ps, axis=1)))
    return new_ms, new_accs


def _attend_finish(state, val_rows):
    return jnp.concatenate([acc[0:val_rows] / acc[val_rows:val_rows + 1] for acc in state[1]], axis=1)


def _dsa_kernel(aqt_ref, iqt_ref, iwt_ref, akv_ref, avt_ref, ik_ref, out_ref, s_sc):
    nh = iwt_ref.shape[0]
    tq, tk = ATT_TQ, ATT_TK
    seq = akv_ref.shape[0]
    j = pl.program_id(1)
    nkb = (j * tq) // tk + 1
    iqt = iqt_ref[...]
    w = iwt_ref[...]
    krow = lax.broadcasted_iota(jnp.int32, (tk, tq), 0)
    qcol = j * tq + lax.broadcasted_iota(jnp.int32, (tk, tq), 1)

    def score_body(kb, carry):
        cmax, cmin = carry
        k0 = pl.multiple_of(kb * tk, tk)
        y = _dot(ik_ref[pl.ds(k0, tk), :], iqt)
        s = jnp.zeros((tk, tq), F32)
        for h in range(nh):
            s = s + w[h:h + 1, :] * jnp.maximum(y[:, h * tq:(h + 1) * tq], 0.0)
        causal = (k0 + krow) <= qcol
        s_lo = jnp.where(causal, s, -jnp.inf)
        s_sc[pl.ds(k0, tk), :] = s_lo
        cmax = jnp.maximum(cmax, _fold(s_lo, jnp.maximum))
        cmin = jnp.minimum(cmin, _fold(jnp.where(causal, s, jnp.inf), jnp.minimum))
        return cmax, cmin

    cmax, cmin = lax.fori_loop(0, nkb, score_body,
                               (jnp.full((64, tq), -jnp.inf, F32), jnp.full((64, tq), jnp.inf, F32)))
    rmax = _rows_reduce(cmax, "max")
    rmin = _rows_reduce(cmin, "min")

    n_causal = (j * tq + lax.broadcasted_iota(jnp.int32, (1, tq), 1) + 1).astype(F32)
    kt = jnp.minimum(n_causal, float(TOPK_MAX))

    tc = SEL_TK
    ncb = (j * tq) // tc + 1
    crow_f = lax.broadcasted_iota(jnp.int32, (tc, tq), 0).astype(F32)

    def count(pred):
        def body(cb, acc):
            k0 = pl.multiple_of(cb * tc, tc)
            return acc + _fold(jnp.where(pred(s_sc[pl.ds(k0, tc), :], k0), 1.0, 0.0), jnp.add)
        return _rows_reduce(lax.fori_loop(0, ncb, body, jnp.zeros((64, tq), F32)), "sum")

    def propose(it, lo, hi, clo):
        mid = 0.5 * lo + 0.5 * hi
        mid = jnp.where(jnp.logical_and(it == 0, jnp.logical_and(lo < 0.0, hi > 0.0)), 0.0, mid)
        mid = jnp.where(jnp.logical_and(it == 1, jnp.logical_and(lo == 0.0, hi > TINY)), TINY, mid)
        active = jnp.logical_and(clo != kt, jnp.logical_and(mid > lo, mid < hi))
        return mid, active

    def sel_step(it, lo, hi, clo, chi):
        mid, active = propose(it, lo, hi, clo)
        cm = count(lambda blk, k0: blk >= mid)
        ge = cm >= kt
        up = jnp.logical_and(active, ge)
        dn = jnp.logical_and(active, jnp.logical_not(ge))
        return jnp.where(up, mid, lo), jnp.where(dn, mid, hi), jnp.where(up, cm, clo), jnp.where(dn, cm, chi)

    def sel_body(c):
        it, _, lo, hi, clo, chi = c
        for u in range(SEL_UNROLL):
            lo, hi, clo, chi = sel_step(it + u, lo, hi, clo, chi)
        _, active = propose(it + SEL_UNROLL, lo, hi, clo)
        go = jnp.logical_and(jnp.max(jnp.where(active, 1.0, 0.0)) > 0.0, it + SEL_UNROLL < SEL_MAX_ITERS)
        return it + SEL_UNROLL, go, lo, hi, clo, chi

    lo0 = rmin
    hi0 = rmax + jnp.abs(rmax) + 1e-30
    clo0, chi0 = n_causal, jnp.zeros((1, tq), F32)
    _, active0 = propose(0, lo0, hi0, clo0)
    go0 = jnp.max(jnp.where(active0, 1.0, 0.0)) > 0.0
    _, _, lo, hi, clo, chi = lax.while_loop(lambda c: c[1], sel_body, (jnp.int32(0), go0, lo0, hi0, clo0, chi0))

    tie = clo > kt

    @pl.when(jnp.max(jnp.where(tie, 1.0, 0.0)) > 0.0)
    def _():
        need = kt - chi

        def idx_body(_, c):
            jl, jh = c
            jm = jnp.floor((jl + jh) * 0.5)
            ok = count(lambda blk, k0: jnp.logical_and(blk == lo, crow_f + k0.astype(F32) <= jm)) >= need
            return jnp.where(ok, jl, jm), jnp.where(ok, jm, jh)

        n_steps = int(np.ceil(np.log2(seq))) + 1
        _, jh = lax.fori_loop(0, n_steps, idx_body,
                              (jnp.full((1, tq), -1.0, F32), jnp.full((1, tq), float(seq - 1), F32)))
        jh = jnp.where(tie, jh, float(seq))

        def fix_body(cb, _):
            k0 = pl.multiple_of(cb * tc, tc)
            blk = s_sc[pl.ds(k0, tc), :]
            drop = jnp.logical_and(blk == lo, crow_f + k0.astype(F32) > jh)
            s_sc[pl.ds(k0, tc), :] = jnp.where(drop, -jnp.inf, blk)
            return 0

        lax.fori_loop(0, ncb, fix_body, 0)

    def att_body(kb, state):
        k0 = pl.multiple_of(kb * tk, tk)
        keep = s_sc[pl.ds(k0, tk), :] >= lo
        return _attend_block(akv_ref[pl.ds(k0, tk), :], avt_ref[:, pl.ds(k0, tk)], aqt_ref, keep, state)

    o_t = _attend_finish(lax.fori_loop(0, nkb, att_body, _attend_init(nh, A_HEAD_DIM)), A_HEAD_DIM)
    for p in range(nh // 2):
        pair = jnp.concatenate([o_t[:, (2 * p) * tq:(2 * p + 1) * tq], o_t[:, (2 * p + 1) * tq:(2 * p + 2) * tq]], axis=0)
        out_ref[:, p * LANES:(p + 1) * LANES] = pair.T.astype(out_ref.dtype)


def _dsa(aqt, iqt, iwt, akv, avt, ikp):
    bsz, nqb, _, hq = aqt.shape
    nh = iwt.shape[2]
    seq = akv.shape[1]
    tq = ATT_TQ
    qblk = lambda b, j: (b, j, 0, 0)
    full = lambda b, j: (b, 0, 0)
    return pl.pallas_call(
        _dsa_kernel,
        grid=(bsz, nqb),
        in_specs=[pl.BlockSpec((None, None, LANES, hq), qblk), pl.BlockSpec((None, None, LANES, hq), qblk),
                  pl.BlockSpec((None, None, nh, tq), qblk),
                  pl.BlockSpec((None, seq, LANES), full), pl.BlockSpec((None, A_HEAD_DIM + ONES_ROWS, seq), full),
                  pl.BlockSpec((None, seq, LANES), full)],
        out_specs=pl.BlockSpec((None, tq, nh * A_HEAD_DIM), lambda b, j: (b, j, 0)),
        out_shape=jax.ShapeDtypeStruct((bsz, seq, nh * A_HEAD_DIM), BF16),
        scratch_shapes=[pltpu.VMEM((seq, tq), F32)],
        compiler_params=pltpu.CompilerParams(
            dimension_semantics=("arbitrary", "arbitrary"), vmem_limit_bytes=VMEM_LIMIT),
        name="dsa",
    )(aqt, iqt, iwt, akv, avt, ikp)


def _mla_kernel(qt_ref, ckv_ref, latt_ref, wuvt_ref, out_ref):
    nh = wuvt_ref.shape[0]
    tq, tk = ATT_TQ, ATT_TK
    j = pl.program_id(1)
    n_full = (j * tq) // tk

    def block(kb, state, masked):
        k0 = pl.multiple_of(kb * tk, tk)
        keep = None
        if masked:
            krow = k0 + lax.broadcasted_iota(jnp.int32, (tk, tq), 0)
            keep = krow <= j * tq + lax.broadcasted_iota(jnp.int32, (tk, tq), 1)
        return _attend_block(ckv_ref[pl.ds(k0, tk), :], latt_ref[:, pl.ds(k0, tk)], qt_ref, keep, state)

    state = lax.fori_loop(0, n_full, lambda kb, st: block(kb, st, False), _attend_init(nh, KV_LORA))
    o_lat_t = _attend_finish(block(n_full, state, True), KV_LORA).astype(BF16)
    for p in range(nh // 2):
        pair = jnp.concatenate([_dot(wuvt_ref[h], o_lat_t[:, h * tq:(h + 1) * tq]) for h in (2 * p, 2 * p + 1)], axis=0)
        out_ref[:, p * LANES:(p + 1) * LANES] = pair.T.astype(out_ref.dtype)


def _mla(qabst, ckv, latt, w_uv_t):
    bsz, nqb, dq, hq = qabst.shape
    nh = w_uv_t.shape[0]
    seq = ckv.shape[1]
    tq = ATT_TQ
    return pl.pallas_call(
        _mla_kernel,
        grid=(bsz, nqb),
        in_specs=[pl.BlockSpec((None, None, dq, hq), lambda b, j: (b, j, 0, 0)),
                  pl.BlockSpec((None, seq, dq), lambda b, j: (b, 0, 0)),
                  pl.BlockSpec((None, KV_LORA + ONES_ROWS, seq), lambda b, j: (b, 0, 0)),
                  pl.BlockSpec(w_uv_t.shape, lambda b, j: (0, 0, 0))],
        out_specs=pl.BlockSpec((None, tq, nh * B_V), lambda b, j: (b, j, 0)),
        out_shape=jax.ShapeDtypeStruct((bsz, seq, nh * B_V), BF16),
        compiler_params=pltpu.CompilerParams(
            dimension_semantics=("arbitrary", "arbitrary"), vmem_limit_bytes=VMEM_LIMIT),
        name="mla",
    )(qabst, ckv, latt, w_uv_t)


def _layer_norm(z, g, b):
    mu = jnp.mean(z, axis=1, keepdims=True)
    d = z - mu
    var = jnp.mean(d * d, axis=1, keepdims=True)
    return d * lax.rsqrt(var + LN_EPS) * g + b


def _split_bf16(v):
    hi = v.astype(BF16)
    return hi, (v - hi.astype(F32)).astype(BF16)


def _to_row_tiles(v, ref):
    for c in range(v.shape[1] // LANES):
        ref[:, c, :] = v[:, c * LANES:(c + 1) * LANES]


def _from_row_tiles(ref):
    return jnp.concatenate([ref[:, c, :] for c in range(ref.shape[1])], axis=1)


def _mix_kernel(x_ref, a_ref, b_ref, wout_ref, g_ref, beta_ref, wr_ref, br_ref,
                h_ref, ti_ref, tg_ref, rank_ref, cnt_ref, carry_sc, *, alpha):
    tm = x_ref.shape[0]
    half = a_ref.shape[1]
    i = pl.program_id(0)

    @pl.when(i == 0)
    def _():
        carry_sc[...] = jnp.zeros(carry_sc.shape, F32)

    mix = _dot(a_ref[...], wout_ref[0:half, :]) + _dot(b_ref[...], wout_ref[half:2 * half, :])
    h = _layer_norm(alpha * x_ref[...] + mix, g_ref[...], beta_ref[...])
    _to_row_tiles(h, h_ref)
    h_hi, h_lo = _split_bf16(h)
    w_hi, w_lo = _split_bf16(wr_ref[...])
    logits = _dot(h_hi, w_hi) + (_dot(h_lo, w_hi) + _dot(h_hi, w_lo)) + br_ref[...]
    lane = _lane_iota((tm, LANES))
    logits = jnp.where(lane < N_EXPERTS, logits, -jnp.inf)
    ids = jnp.zeros((tm, LANES), jnp.int32)
    vals, hots = [], []
    for k in range(TOP_K):
        m = jnp.max(logits, axis=1, keepdims=True)
        idx = jnp.min(jnp.where(logits == m, lane, LANES), axis=1, keepdims=True)
        hot = lane == idx
        ids = jnp.where(lane == k, idx, ids)
        logits = jnp.where(hot, -jnp.inf, logits)
        vals.append(m)
        hots.append(hot)
    es = [jnp.exp(v - vals[0]) for v in vals]
    den = (es[0] + es[1]) + (es[2] + es[3])
    ti_ref[...] = ids[:, :TOP_K]
    for k in range(TOP_K):
        tg_ref[:, k, :] = jnp.broadcast_to(es[k] / den, (tm, LANES))

    onehot = jnp.where(jnp.logical_or(jnp.logical_or(hots[0], hots[1]), jnp.logical_or(hots[2], hots[3])), 1.0, 0.0)
    r_i = lax.broadcasted_iota(jnp.int32, (tm, tm), 0)
    c_i = lax.broadcasted_iota(jnp.int32, (tm, tm), 1)
    before = _dot(jnp.where(c_i < r_i, 1.0, 0.0).astype(BF16), onehot.astype(BF16)) + carry_sc[...]
    ranks = jnp.zeros((tm, LANES), F32)
    for k in range(TOP_K):
        ranks = jnp.where(lane == k, jnp.sum(jnp.where(hots[k], before, 0.0), axis=1, keepdims=True), ranks)
    rank_ref[...] = ranks[:, :TOP_K].astype(jnp.int32)
    carry_sc[...] = carry_sc[...] + jnp.sum(onehot, axis=0, keepdims=True)
    cnt_ref[...] = carry_sc[...].astype(jnp.int32)


def _mix(x2, a_out, b_out, w_out, g, beta, w_router, b_router, alpha):
    n, d = x2.shape
    tm = MIX_TM
    half = a_out.shape[1]
    row = lambda i: (i, 0)
    const = lambda i: (0, 0)
    wr = jnp.pad(w_router, ((0, 0), (0, LANES - N_EXPERTS)))
    br = jnp.pad(b_router, (0, LANES - N_EXPERTS)).reshape(1, LANES)
    return pl.pallas_call(
        functools.partial(_mix_kernel, alpha=alpha),
        grid=(n // tm,),
        in_specs=[pl.BlockSpec((tm, d), row), pl.BlockSpec((tm, half), row), pl.BlockSpec((tm, half), row),
                  pl.BlockSpec(w_out.shape, const), pl.BlockSpec((1, d), const), pl.BlockSpec((1, d), const),
                  pl.BlockSpec(wr.shape, const), pl.BlockSpec((1, LANES), const)],
        out_specs=[pl.BlockSpec((tm, d // LANES, LANES), lambda i: (i, 0, 0)), pl.BlockSpec((tm, TOP_K), row),
                   pl.BlockSpec((tm, TOP_K, LANES), lambda i: (i, 0, 0)), pl.BlockSpec((tm, TOP_K), row),
                   pl.BlockSpec((1, LANES), const)],
        out_shape=[jax.ShapeDtypeStruct((n, d // LANES, LANES), F32), jax.ShapeDtypeStruct((n, TOP_K), jnp.int32),
                   jax.ShapeDtypeStruct((n, TOP_K, LANES), F32), jax.ShapeDtypeStruct((n, TOP_K), jnp.int32),
                   jax.ShapeDtypeStruct((1, LANES), jnp.int32)],
        scratch_shapes=[pltpu.VMEM((1, LANES), F32)],
        compiler_params=pltpu.CompilerParams(dimension_semantics=("arbitrary",), vmem_limit_bytes=VMEM_LIMIT),
        name="mix",
    )(x2, a_out, b_out, w_out.astype(BF16), g.reshape(1, d), beta.reshape(1, d), wr, br)


def _route(top_i, rank, counts, tm):
    n = top_i.shape[0]
    n_tiles = (n * TOP_K) // tm + N_EXPERTS
    counts = counts[0, :N_EXPERTS]
    tiles_per = (counts + tm - 1) // tm
    tile_end = jnp.cumsum(tiles_per)
    pstart = (tile_end - tiles_per) * tm
    pos = rank + jnp.sum(jnp.where(top_i[..., None] == jnp.arange(N_EXPERTS, dtype=jnp.int32), pstart, 0), axis=-1)
    tile_ids = jnp.arange(n_tiles, dtype=jnp.int32)
    n_valid = tile_end[-1]
    tile_expert = jnp.sum(tile_end[None, :] <= jnp.minimum(tile_ids, n_valid - 1)[:, None], axis=1).astype(jnp.int32)
    tile_src = jnp.minimum(tile_ids, n_valid - 1)
    tile_valid = (tile_ids < n_valid).astype(jnp.int32)
    pad_start = jnp.concatenate([pstart + counts, n_valid[None]]).astype(jnp.int32)
    return pos.astype(jnp.int32), tile_expert, tile_src.astype(jnp.int32), tile_valid, pad_start, n_tiles


def _dispatch_kernel(pad_ref, h_ref, pos_hbm, xs_hbm, pos_s, zbuf, sem_idx, sem_row):
    i = pl.program_id(0)
    tm = h_ref.shape[0]

    @pl.when(i == 0)
    def _():
        zt = zbuf.shape[0]
        zbuf[...] = jnp.zeros(zbuf.shape, zbuf.dtype)
        for e in range(N_EXPERTS):
            cpe = pltpu.make_async_copy(zbuf, xs_hbm.at[pl.ds(pad_ref[e], zt)], sem_row)
            cpe.start()
            cpe.wait()

        def fill(t, _):
            cpz = pltpu.make_async_copy(zbuf, xs_hbm.at[pl.ds(pl.multiple_of(t * zt, zt), zt)], sem_row)
            cpz.start()
            cpz.wait()
            return 0

        lax.fori_loop(pad_ref[N_EXPERTS], xs_hbm.shape[0] // zt, fill, 0)

    cp = pltpu.make_async_copy(pos_hbm.at[i], pos_s, sem_idx)
    cp.start()
    cp.wait()

    def body(r, _):
        for s in range(TOP_K):
            pltpu.make_async_copy(h_ref.at[r], xs_hbm.at[pos_s[r * TOP_K + s]], sem_row).start(priority=s % 2)
        return 0

    lax.fori_loop(0, tm, body, 0, unroll=8)
    for s in range(TOP_K):
        pltpu.make_async_copy(h_ref, xs_hbm.at[pl.ds(0, tm)], sem_row).wait()


def _dispatch(h3, pos, pad_start, n_rows, zrows):
    n, nt8, _ = h3.shape
    tm = ROW_TM
    return pl.pallas_call(
        _dispatch_kernel,
        grid_spec=pltpu.PrefetchScalarGridSpec(
            num_scalar_prefetch=1,
            grid=(n // tm,),
            in_specs=[pl.BlockSpec((tm, nt8, LANES), lambda i, pad: (i, 0, 0)), pl.BlockSpec(memory_space=pl.ANY)],
            out_specs=pl.BlockSpec(memory_space=pl.ANY),
            scratch_shapes=[pltpu.SMEM((tm * TOP_K,), jnp.int32), pltpu.VMEM((zrows, nt8, LANES), F32),
                            pltpu.SemaphoreType.DMA, pltpu.SemaphoreType.DMA]),
        out_shape=jax.ShapeDtypeStruct((n_rows + zrows, nt8, LANES), F32),
        compiler_params=pltpu.CompilerParams(dimension_semantics=("arbitrary",), vmem_limit_bytes=VMEM_LIMIT),
        name="dispatch",
    )(pad_start, h3, pos.reshape(n // tm, tm * TOP_K))


def _moe_kernel(te_ref, ts_ref, tv_ref, xs_ref, wg_ref, bg_ref, wu_ref, bu_ref, wd_ref, bd_ref, ys_ref):
    del te_ref, ts_ref
    i = pl.program_id(0)

    @pl.when(tv_ref[i] > 0)
    def _():
        x = _from_row_tiles(xs_ref).astype(BF16)
        g = jnp.minimum(_dot(x, wg_ref[0]) + bg_ref[0], SWIGLU_LIMIT)
        u = jnp.clip(_dot(x, wu_ref[0]) + bu_ref[0], -SWIGLU_LIMIT, SWIGLU_LIMIT)
        act = g * (1.0 / (1.0 + jnp.exp(-SWIGLU_ALPHA * g))) * (u + 1.0)
        _to_row_tiles(_dot(act.astype(BF16), wd_ref[0]) + bd_ref[0], ys_ref)

    @pl.when(tv_ref[i] == 0)
    def _():
        ys_ref[...] = jnp.zeros(ys_ref.shape, ys_ref.dtype)


def _moe(xs, tile_expert, tile_src, tile_valid, n_tiles, w_gate, b_gate, w_up, b_up, w_down, b_down):
    _, nt8, _ = xs.shape
    tm = MOE_TM
    ne, d, dff = w_gate.shape
    wmap = lambda i, te, ts, tv: (te[i], 0, 0)
    grid_spec = pltpu.PrefetchScalarGridSpec(
        num_scalar_prefetch=3,
        grid=(n_tiles,),
        in_specs=[pl.BlockSpec((tm, nt8, LANES), lambda i, te, ts, tv: (ts[i], 0, 0)),
                  pl.BlockSpec((1, d, dff), wmap), pl.BlockSpec((1, 1, dff), wmap),
                  pl.BlockSpec((1, d, dff), wmap), pl.BlockSpec((1, 1, dff), wmap),
                  pl.BlockSpec((1, dff, d), wmap), pl.BlockSpec((1, 1, d), wmap)],
        out_specs=pl.BlockSpec((tm, nt8, LANES), lambda i, te, ts, tv: (i, 0, 0)))
    return pl.pallas_call(
        _moe_kernel,
        grid_spec=grid_spec,
        out_shape=jax.ShapeDtypeStruct((n_tiles * tm, nt8, LANES), F32),
        compiler_params=pltpu.CompilerParams(dimension_semantics=("arbitrary",), vmem_limit_bytes=VMEM_LIMIT),
        name="moe",
    )(tile_expert, tile_src, tile_valid, xs,
      w_gate.astype(BF16), b_gate.reshape(ne, 1, dff), w_up.astype(BF16), b_up.reshape(ne, 1, dff),
      w_down.astype(BF16), b_down.reshape(ne, 1, d))


def _combine_kernel(h_ref, gate_ref, pos_hbm, ys_hbm, g_ref, beta_ref, o_ref, pos_s, ybuf, sem_idx, sem_row, *, alpha):
    i = pl.program_id(0)
    tm = h_ref.shape[0]
    cp = pltpu.make_async_copy(pos_hbm.at[i], pos_s, sem_idx)
    cp.start()
    cp.wait()

    def body(r, _):
        for s in range(TOP_K):
            pltpu.make_async_copy(ys_hbm.at[pos_s[r * TOP_K + s]], ybuf.at[s, r], sem_row).start(priority=s % 2)
        return 0

    lax.fori_loop(0, tm, body, 0, unroll=8)
    for s in range(TOP_K):
        pltpu.make_async_copy(ys_hbm.at[pl.ds(0, tm)], ybuf.at[s], sem_row).wait()

    z = alpha * h_ref[...]
    for s in range(TOP_K):
        z = z + gate_ref[:, s:s + 1, :] * ybuf[s]
    inv_d = 1.0 / (z.shape[1] * z.shape[2])
    mu = jnp.sum(z, axis=(1, 2), keepdims=True) * inv_d
    dz = z - mu
    var = jnp.sum(dz * dz, axis=(1, 2), keepdims=True) * inv_d
    o_ref[...] = dz * lax.rsqrt(var + LN_EPS) * g_ref[...] + beta_ref[...]


def _combine(h3, gates, pos, ys, g, beta, alpha):
    n, nt8, _ = h3.shape
    tm = ROW_TM
    tile = lambda i: (i, 0, 0)
    const = lambda i: (0, 0, 0)
    return pl.pallas_call(
        functools.partial(_combine_kernel, alpha=alpha),
        grid=(n // tm,),
        in_specs=[pl.BlockSpec((tm, nt8, LANES), tile), pl.BlockSpec((tm, TOP_K, LANES), tile),
                  pl.BlockSpec(memory_space=pl.ANY), pl.BlockSpec(memory_space=pl.ANY),
                  pl.BlockSpec((1, nt8, LANES), const), pl.BlockSpec((1, nt8, LANES), const)],
        out_specs=pl.BlockSpec((tm, nt8, LANES), tile),
        out_shape=jax.ShapeDtypeStruct((n, nt8, LANES), F32),
        scratch_shapes=[pltpu.SMEM((tm * TOP_K,), jnp.int32), pltpu.VMEM((TOP_K, tm, nt8, LANES), F32),
                        pltpu.SemaphoreType.DMA, pltpu.SemaphoreType.DMA],
        compiler_params=pltpu.CompilerParams(dimension_semantics=("arbitrary",), vmem_limit_bytes=VMEM_LIMIT),
        name="combine",
    )(h3, gates, pos.reshape(n // tm, tm * TOP_K), ys, g.reshape(1, nt8, LANES), beta.reshape(1, nt8, LANES))


def _layer(x, tabs, alpha, w_in, ik_g, ik_b, q_g, w_q_up, kv_g, w_kv_up, w_out, ln1_g, ln1_b,
           w_router, b_router, w_gate, b_gate, w_up, b_up, w_down, b_down, ln2_g, ln2_b):
    bsz, seq, d = x.shape
    w_t, w_n, wq_t, bd_uk_t, w_uv_t = _pack_weights(w_in, w_q_up, w_kv_up)
    aqt, iqt, iwt, akv, avt, ikp, qabst, ckv, latt = _proj(x, tabs, w_t, w_n, wq_t, bd_uk_t, ik_g, ik_b, q_g, kv_g)
    a_out = _dsa(aqt, iqt, iwt, akv, avt, ikp)
    b_out = _mla(qabst, ckv, latt, w_uv_t)
    n = bsz * seq
    h3, top_i, gates, rank, counts = _mix(x.reshape(n, d), a_out.reshape(n, -1), b_out.reshape(n, -1), w_out,
                                          ln1_g, ln1_b, w_router, b_router, alpha)
    pos, tile_expert, tile_src, tile_valid, pad_start, n_tiles = _route(top_i, rank, counts, MOE_TM)
    xs = _dispatch(h3, pos, pad_start, n_tiles * MOE_TM, MOE_TM)
    ys = _moe(xs, tile_expert, tile_src, tile_valid, n_tiles, w_gate, b_gate, w_up, b_up, w_down, b_down)
    return _combine(h3, gates, pos, ys, ln2_g, ln2_b, alpha).reshape(bsz, seq, d)


def kernel(x, positions, w_mix_in, idx_k_norm_g, idx_k_norm_b, mla_q_norm_g, mla_w_q_up, mla_kv_norm_g, mla_w_kv_up, w_mix_out, ln1_g, ln1_b, w_router, b_router, w_gate, b_gate, w_up, b_up, w_down, b_down, ln2_g, ln2_b):
    depth = w_mix_in.shape[0]
    alpha = float((2 * depth) ** 0.25)
    tabs = _rope_tables(positions)
    for l in range(depth):
        x = _layer(x, tabs, alpha, w_mix_in[l], idx_k_norm_g[l], idx_k_norm_b[l], mla_q_norm_g[l], mla_w_q_up[l],
                   mla_kv_norm_g[l], mla_w_kv_up[l], w_mix_out[l], ln1_g[l], ln1_b[l], w_router[l], b_router[l],
                   w_gate[l], b_gate[l], w_up[l], b_up[l], w_down[l], b_down[l], ln2_g[l], ln2_b[l])
    return x
```

```python
import functools

import jax
import jax.numpy as jnp
import numpy as np
from jax import lax
from jax.experimental import pallas as pl
from jax.experimental.pallas import tpu as pltpu

F32 = jnp.float32
BF16 = jnp.bfloat16

A_HEADS, A_HEAD_DIM = 8, 64
IDX_HEADS, IDX_DIM = 8, 64
TOPK_MAX = 256
B_HEADS, B_NOPE, B_ROPE, B_V = 8, 64, 32, 64
Q_LORA, KV_LORA = 256, 128
N_EXPERTS, TOP_K = 32, 4
SWIGLU_LIMIT, SWIGLU_ALPHA = 7.0, 1.702
ROPE_THETA = 10000.0
LN_EPS, RMS_EPS = 1e-5, 1e-6

LANES = 128
SUBLANES = 8
VMEM_LIMIT = 56 * 1024 * 1024

PROJ_TM = 512
ATT_TQ = LANES
ATT_TK = 512
SCORE_TK = 1024
SEL_TK = 512
ONES_ROWS = 16
LOG2_E = 1.4426950408889634
MIX_TM = 512
MOE_TM = 256
ROW_TM = 256
SEL_MAX_ITERS = 400
SEL_UNROLL = 4

MASKED = -1e30
M_FLOOR = -1e29
TINY = float(np.finfo(np.float32).tiny)


def _dot(a, b):
    return jnp.dot(a, b, preferred_element_type=F32)


def _dot_t(a, b):
    return lax.dot_general(a, b, (((1,), (1,)), ((), ())), preferred_element_type=F32)


def _lane_iota(shape):
    return lax.broadcasted_iota(jnp.int32, shape, len(shape) - 1)


def _fold(v, op, rows=64):
    n = v.shape[0] // rows
    acc = v[0:rows]
    for i in range(1, n):
        acc = op(acc, v[i * rows:(i + 1) * rows])
    return acc


def _rows_reduce(v, op):
    if op == "sum":
        return jnp.sum(v, axis=0, keepdims=True)
    if op == "max":
        return jnp.max(v, axis=0, keepdims=True)
    return jnp.min(v, axis=0, keepdims=True)


_R_AQ, _R_AQR, _R_IQ, _R_IQR, _R_V, _R_IW, _R_QC, _R_KVC, _R_TOTAL = 0, 512, 1024, 1536, 2048, 2112, 2128, 2384, 2512
_C_KV, _C_KVR, _C_IK, _C_KVC, _C_KR, _C_KRR, _C_TOTAL = 0, 128, 256, 384, 512, 640, 768


def _rot_cols(w, dim):
    k, n = w.shape
    w3 = w.reshape(k, n // dim, dim)
    return jnp.concatenate([-w3[..., dim // 2:], w3[..., :dim // 2]], axis=-1).reshape(k, n)


def _pad_cols(w, width):
    return jnp.pad(w, ((0, 0), (0, width - w.shape[1])))


def _pack_weights(w_in, w_q_up, w_kv_up):
    o = np.cumsum((0, 512, 64, 64, 512, 64, 8, 256, 128, 32))
    a_q, a_k, a_v, i_q, i_k, i_w, b_qc, b_kvc, b_kr = [w_in[:, o[i]:o[i + 1]] for i in range(9)]
    w_t = jnp.concatenate([a_q, _rot_cols(a_q, 64), i_q, _rot_cols(i_q, 64), a_v, _pad_cols(i_w, 16),
                           b_qc, b_kvc], axis=1).T
    assert w_t.shape[0] == _R_TOTAL
    w_n = jnp.concatenate([a_k, a_v, _rot_cols(a_k, 64), jnp.zeros_like(a_k), _pad_cols(i_k, 128), b_kvc,
                           _pad_cols(b_kr, 128), _pad_cols(_rot_cols(b_kr, 32), 128)], axis=1)
    assert w_n.shape[1] == _C_TOTAL
    wq = w_q_up.reshape(Q_LORA, B_HEADS, B_NOPE + B_ROPE)
    wq_rope = wq[..., B_NOPE:].reshape(Q_LORA, B_HEADS * B_ROPE)
    wq_t = jnp.concatenate([wq[..., :B_NOPE].reshape(Q_LORA, B_HEADS * B_NOPE), wq_rope,
                            _rot_cols(wq_rope, B_ROPE)], axis=1).T
    wkv = w_kv_up.reshape(KV_LORA, B_HEADS, B_NOPE + B_V)
    eye = jnp.eye(B_HEADS, dtype=w_kv_up.dtype)
    bd_uk_t = jnp.einsum('lhd,hg->glhd', wkv[..., :B_NOPE], eye).reshape(B_HEADS * KV_LORA, B_HEADS * B_NOPE)
    w_uv_t = jnp.transpose(wkv[..., B_NOPE:], (1, 2, 0))
    return w_t.astype(BF16), w_n.astype(BF16), wq_t.astype(BF16), bd_uk_t.astype(BF16), w_uv_t.astype(BF16)


def _rope_tables(positions):
    pos = positions.astype(F32)[..., None]
    inv64 = ROPE_THETA ** (-jnp.arange(0, 64, 2, dtype=F32) / 64)
    inv32 = ROPE_THETA ** (-jnp.arange(0, 32, 2, dtype=F32) / 32)
    c64, s64 = jnp.cos(pos * inv64), jnp.sin(pos * inv64)
    c32, s32 = jnp.cos(pos * inv32), jnp.sin(pos * inv32)
    nat = (jnp.tile(c64, 4), jnp.tile(s64, 4), jnp.tile(c32, 8), jnp.tile(s32, 8))
    tr = tuple(jnp.swapaxes(jnp.tile(t, 2), 1, 2) for t in (c64, s64, c32, s32))
    return nat + tr


def _proj_kernel(x_ref, ca_ref, sa_ref, cr_ref, sr_ref, cat_ref, sat_ref, crt_ref, srt_ref,
                 wt_ref, wn_ref, wqt_ref, bdukt_ref, ikg_ref, ikb_ref, qg_ref, kvg_ref, kvgt_ref,
                 aqt_ref, iqt_ref, iwt_ref, akv_ref, avt_ref, ik_ref, qabst_ref, ckv_ref, latt_ref):
    tm = x_ref.shape[0]
    nqb = tm // ATT_TQ
    xb = x_ref[...].astype(BF16)
    ca, sa, cr, sr = ca_ref[...], sa_ref[...], cr_ref[...], sr_ref[...]
    lane = _lane_iota((tm, LANES))
    low64 = lane < 64

    def proj_t(r0, rows):
        return _dot_t(wt_ref[r0:r0 + rows, :], xb)

    def proj_n(c0, width):
        return _dot(xb, wn_ref[:, c0:c0 + width])

    cat8 = jnp.concatenate([cat_ref[...]] * 8, axis=0)
    sat8 = jnp.concatenate([sat_ref[...]] * 8, axis=0)

    def split_heads(v, out_ref):
        vb = v.astype(BF16)
        zeros = jnp.zeros((64, ATT_TQ), BF16)
        for c in range(nqb):
            for h in range(8):
                out_ref[c, 0:64, h * ATT_TQ:(h + 1) * ATT_TQ] = vb[64 * h:64 * h + 64, c * ATT_TQ:(c + 1) * ATT_TQ]
                out_ref[c, 64:128, h * ATT_TQ:(h + 1) * ATT_TQ] = zeros

    aq_t = (proj_t(_R_AQ, 512) * cat8 + proj_t(_R_AQR, 512) * sat8) * (A_HEAD_DIM ** -0.5 * LOG2_E)
    split_heads(aq_t, aqt_ref)
    iq_t = proj_t(_R_IQ, 512) * cat8 + proj_t(_R_IQR, 512) * sat8
    split_heads(iq_t, iqt_ref)
    iw_t = proj_t(_R_IW, 16)[0:IDX_HEADS] * ((IDX_HEADS * IDX_DIM) ** -0.5)
    for c in range(nqb):
        iwt_ref[c] = iw_t[:, c * ATT_TQ:(c + 1) * ATT_TQ]
    ones_rows = jnp.where(lax.broadcasted_iota(jnp.int32, (ONES_ROWS, tm), 0) == 0, 1.0, 0.0).astype(BF16)
    avt_ref[...] = jnp.concatenate([proj_t(_R_V, 64).astype(BF16), ones_rows], axis=0)

    kc = jnp.where(low64, ca, 1.0)
    ks = jnp.where(low64, sa, 0.0)
    akv_ref[...] = (proj_n(_C_KV, 128) * kc + proj_n(_C_KVR, 128) * ks).astype(BF16)

    y = proj_n(_C_IK, 128)
    mu = jnp.sum(y, axis=1, keepdims=True) * (1.0 / IDX_DIM)
    d = jnp.where(low64, y - mu, 0.0)
    var = jnp.sum(d * d, axis=1, keepdims=True) * (1.0 / IDX_DIM)
    yn = d * lax.rsqrt(var + LN_EPS) * ikg_ref[...] + ikb_ref[...]
    rot = jnp.where(lane < 32, -pltpu.roll(yn, LANES - 32, 1), pltpu.roll(yn, 32, 1))
    ik_ref[...] = jnp.where(low64, yn * ca + rot * sa, 0.0).astype(BF16)

    qc_t = proj_t(_R_QC, Q_LORA)
    qn_t = (qc_t * lax.rsqrt(jnp.mean(qc_t * qc_t, axis=0, keepdims=True) + RMS_EPS) * qg_ref[...]).astype(BF16)
    q_t = _dot(wqt_ref[...], qn_t)
    scale = (B_NOPE + B_ROPE) ** -0.5 * LOG2_E
    q_lat_t = (_dot(bdukt_ref[...], q_t[0:512].astype(BF16)) * scale).astype(BF16)
    crt8 = jnp.concatenate([crt_ref[...]] * 8, axis=0)
    srt8 = jnp.concatenate([srt_ref[...]] * 8, axis=0)
    q_r_t = ((q_t[512:768] * crt8 + q_t[768:1024] * srt8) * scale).astype(BF16)
    zeros = jnp.zeros((2 * LANES - KV_LORA - B_ROPE, ATT_TQ), BF16)
    for c in range(nqb):
        tok = slice(c * ATT_TQ, (c + 1) * ATT_TQ)
        for h in range(B_HEADS):
            col = slice(h * ATT_TQ, (h + 1) * ATT_TQ)
            qabst_ref[c, 0:KV_LORA, col] = q_lat_t[h * KV_LORA:(h + 1) * KV_LORA, tok]
            qabst_ref[c, KV_LORA:KV_LORA + B_ROPE, col] = q_r_t[h * B_ROPE:(h + 1) * B_ROPE, tok]
            qabst_ref[c, KV_LORA + B_ROPE:2 * LANES, col] = zeros

    kvc = proj_n(_C_KVC, KV_LORA)
    kvn = kvc * lax.rsqrt(jnp.mean(kvc * kvc, axis=1, keepdims=True) + RMS_EPS) * kvg_ref[...]
    kr = proj_n(_C_KR, 128) * cr + proj_n(_C_KRR, 128) * sr
    ckv_ref[...] = jnp.concatenate([kvn, kr], axis=1).astype(BF16)
    kvc_t = proj_t(_R_KVC, KV_LORA)
    lat_t = (kvc_t * lax.rsqrt(jnp.mean(kvc_t * kvc_t, axis=0, keepdims=True) + RMS_EPS) * kvgt_ref[...]).astype(BF16)
    latt_ref[...] = jnp.concatenate([lat_t, ones_rows], axis=0)


def _proj(x, tabs, w_t, w_n, wq_t, bd_uk_t, ik_g, ik_b, q_g, kv_g):
    bsz, seq, d = x.shape
    tm = PROJ_TM
    nqb = tm // ATT_TQ
    tok = lambda b, i: (b, i, 0)
    feat = lambda b, i: (b, 0, i)
    qblk = lambda b, i: (b, i, 0, 0)
    const2 = lambda b, i: (0, 0)
    nat_spec = pl.BlockSpec((None, tm, LANES), tok)
    col = lambda v: jnp.broadcast_to(v.reshape(-1, 1), (v.shape[0], tm))
    ik_g = jnp.pad(ik_g, (0, LANES - IDX_DIM)).reshape(1, LANES)
    ik_b = jnp.pad(ik_b, (0, LANES - IDX_DIM)).reshape(1, LANES)
    consts = (w_t, w_n, wq_t, bd_uk_t, ik_g, ik_b, col(q_g), kv_g.reshape(1, KV_LORA), col(kv_g))
    return pl.pallas_call(
        _proj_kernel,
        grid=(bsz, seq // tm),
        in_specs=[pl.BlockSpec((None, tm, d), tok), nat_spec, nat_spec, nat_spec, nat_spec,
                  pl.BlockSpec((None, 64, tm), feat), pl.BlockSpec((None, 64, tm), feat),
                  pl.BlockSpec((None, 32, tm), feat), pl.BlockSpec((None, 32, tm), feat)]
                 + [pl.BlockSpec(c.shape, const2) for c in consts],
        out_specs=[pl.BlockSpec((None, nqb, LANES, A_HEADS * ATT_TQ), qblk),
                   pl.BlockSpec((None, nqb, LANES, IDX_HEADS * ATT_TQ), qblk),
                   pl.BlockSpec((None, nqb, IDX_HEADS, ATT_TQ), qblk),
                   pl.BlockSpec((None, tm, LANES), tok),
                   pl.BlockSpec((None, A_HEAD_DIM + ONES_ROWS, tm), feat),
                   pl.BlockSpec((None, tm, LANES), tok),
                   pl.BlockSpec((None, nqb, 2 * LANES, B_HEADS * ATT_TQ), qblk),
                   pl.BlockSpec((None, tm, 2 * LANES), tok),
                   pl.BlockSpec((None, KV_LORA + ONES_ROWS, tm), feat)],
        out_shape=[jax.ShapeDtypeStruct((bsz, seq // ATT_TQ, LANES, A_HEADS * ATT_TQ), BF16),
                   jax.ShapeDtypeStruct((bsz, seq // ATT_TQ, LANES, IDX_HEADS * ATT_TQ), BF16),
                   jax.ShapeDtypeStruct((bsz, seq // ATT_TQ, IDX_HEADS, ATT_TQ), F32),
                   jax.ShapeDtypeStruct((bsz, seq, LANES), BF16),
                   jax.ShapeDtypeStruct((bsz, A_HEAD_DIM + ONES_ROWS, seq), BF16),
                   jax.ShapeDtypeStruct((bsz, seq, LANES), BF16),
                   jax.ShapeDtypeStruct((bsz, seq // ATT_TQ, 2 * LANES, B_HEADS * ATT_TQ), BF16),
                   jax.ShapeDtypeStruct((bsz, seq, 2 * LANES), BF16),
                   jax.ShapeDtypeStruct((bsz, KV_LORA + ONES_ROWS, seq), BF16)],
        compiler_params=pltpu.CompilerParams(
            dimension_semantics=("arbitrary", "arbitrary"), vmem_limit_bytes=VMEM_LIMIT),
        name="proj",
    )(x, *tabs, *consts)


def _attend_init(nh, val_rows):
    return ([jnp.full((1, LANES), M_FLOOR, F32) for _ in range(nh)],
            jnp.zeros((val_rows + ONES_ROWS, nh * LANES), F32))


def _attend_update(lg_ref, bias, vt_blk, state):
    ms, acc = state
    new_ms, ps, alphas = [], [], []
    for h in range(len(ms)):
        lh = lg_ref[:, h * LANES:(h + 1) * LANES]
        if bias is not None:
            lh = lh + bias
        m_new = jnp.maximum(ms[h], _rows_reduce(_fold(lh, jnp.maximum), "max"))
        alphas.append(jnp.exp2(ms[h] - m_new))
        ps.append(jnp.exp2(lh - m_new).astype(BF16))
        new_ms.append(m_new)
    return new_ms, jnp.concatenate(alphas, axis=1) * acc + _dot(vt_blk, jnp.concatenate(ps, axis=1))


def _attend_pipelined(n_blocks, scores, update, lg_a, lg_b, state):
    last = n_blocks - 1
    n_pairs = (n_blocks + 1) // 2

    def pair(i, st, lookahead):
        kb0 = 2 * i
        lg_b[...] = scores(jnp.minimum(kb0 + 1, last))
        st = update(kb0, lg_a, True, not lookahead, st)
        if lookahead:
            lg_a[...] = scores(kb0 + 2)
        return update(jnp.minimum(kb0 + 1, last), lg_b, kb0 + 1 <= last, not lookahead, st)

    lg_a[...] = scores(0)
    state = lax.fori_loop(0, n_pairs - 1, lambda i, st: pair(i, st, True), state)
    return pair(n_pairs - 1, state, False)


def _attend_finish(state, val_rows):
    acc = state[1]
    return acc[0:val_rows] / acc[val_rows:val_rows + 1]


def _dsa_kernel(aqt_ref, iqt_ref, iwt_ref, akv_ref, avt_ref, ik_ref, out_ref, s_sc, lg_a, lg_b):
    nh = iwt_ref.shape[0]
    tq, tk = ATT_TQ, ATT_TK
    seq = akv_ref.shape[0]
    j = pl.program_id(1)
    nkb = (j * tq) // tk + 1
    iqt = iqt_ref[...]
    w = iwt_ref[...]

    ts = SCORE_TK
    krow = lax.broadcasted_iota(jnp.int32, (ts, tq), 0)
    qcol = j * tq + lax.broadcasted_iota(jnp.int32, (ts, tq), 1)

    def score_body(kb, carry):
        cmax, cmin = carry
        k0 = pl.multiple_of(kb * ts, ts)
        y = _dot(ik_ref[pl.ds(k0, ts), :], iqt)
        s = jnp.zeros((ts, tq), F32)
        for h in range(nh):
            s = s + w[h:h + 1, :] * jnp.maximum(y[:, h * tq:(h + 1) * tq], 0.0)
        causal = (k0 + krow) <= qcol
        s_lo = jnp.where(causal, s, -jnp.inf)
        s_sc[pl.ds(k0, ts), :] = s_lo
        cmax = jnp.maximum(cmax, _fold(s_lo, jnp.maximum))
        cmin = jnp.minimum(cmin, _fold(jnp.where(causal, s, jnp.inf), jnp.minimum))
        return cmax, cmin

    cmax, cmin = lax.fori_loop(0, (j * tq) // ts + 1, score_body,
                               (jnp.full((64, tq), -jnp.inf, F32), jnp.full((64, tq), jnp.inf, F32)))
    rmax = _rows_reduce(cmax, "max")
    rmin = _rows_reduce(cmin, "min")

    n_causal = (j * tq + lax.broadcasted_iota(jnp.int32, (1, tq), 1) + 1).astype(F32)
    kt = jnp.minimum(n_causal, float(TOPK_MAX))

    tc = SEL_TK
    ncb = (j * tq) // tc + 1
    crow_f = lax.broadcasted_iota(jnp.int32, (tc, tq), 0).astype(F32)

    def count(pred):
        def body(cb, acc):
            k0 = pl.multiple_of(cb * tc, tc)
            return acc + _fold(jnp.where(pred(s_sc[pl.ds(k0, tc), :], k0), 1.0, 0.0), jnp.add)
        return _rows_reduce(lax.fori_loop(0, ncb, body, jnp.zeros((64, tq), F32)), "sum")

    def propose(it, lo, hi, clo):
        mid = 0.5 * lo + 0.5 * hi
        mid = jnp.where(jnp.logical_and(it == 0, jnp.logical_and(lo < 0.0, hi > 0.0)), 0.0, mid)
        mid = jnp.where(jnp.logical_and(it == 1, jnp.logical_and(lo == 0.0, hi > TINY)), TINY, mid)
        active = jnp.logical_and(clo != kt, jnp.logical_and(mid > lo, mid < hi))
        return mid, active

    def sel_step(it, lo, hi, clo, chi):
        mid, active = propose(it, lo, hi, clo)
        cm = count(lambda blk, k0: blk >= mid)
        ge = cm >= kt
        up = jnp.logical_and(active, ge)
        dn = jnp.logical_and(active, jnp.logical_not(ge))
        return jnp.where(up, mid, lo), jnp.where(dn, mid, hi), jnp.where(up, cm, clo), jnp.where(dn, cm, chi)

    def sel_body(c):
        it, _, lo, hi, clo, chi = c
        for u in range(SEL_UNROLL):
            lo, hi, clo, chi = sel_step(it + u, lo, hi, clo, chi)
        _, active = propose(it + SEL_UNROLL, lo, hi, clo)
        go = jnp.logical_and(jnp.max(jnp.where(active, 1.0, 0.0)) > 0.0, it + SEL_UNROLL < SEL_MAX_ITERS)
        return it + SEL_UNROLL, go, lo, hi, clo, chi

    lo0 = rmin
    hi0 = rmax + jnp.abs(rmax) + 1e-30
    clo0, chi0 = n_causal, jnp.zeros((1, tq), F32)
    _, active0 = propose(0, lo0, hi0, clo0)
    go0 = jnp.max(jnp.where(active0, 1.0, 0.0)) > 0.0
    _, _, lo, hi, clo, chi = lax.while_loop(lambda c: c[1], sel_body, (jnp.int32(0), go0, lo0, hi0, clo0, chi0))

    tie = clo > kt

    @pl.when(jnp.max(jnp.where(tie, 1.0, 0.0)) > 0.0)
    def _():
        need = kt - chi

        def idx_body(_, c):
            jl, jh = c
            jm = jnp.floor((jl + jh) * 0.5)
            ok = count(lambda blk, k0: jnp.logical_and(blk == lo, crow_f + k0.astype(F32) <= jm)) >= need
            return jnp.where(ok, jl, jm), jnp.where(ok, jm, jh)

        n_steps = int(np.ceil(np.log2(seq))) + 1
        _, jh = lax.fori_loop(0, n_steps, idx_body,
                              (jnp.full((1, tq), -1.0, F32), jnp.full((1, tq), float(seq - 1), F32)))
        jh = jnp.where(tie, jh, float(seq))

        def fix_body(cb, _):
            k0 = pl.multiple_of(cb * tc, tc)
            blk = s_sc[pl.ds(k0, tc), :]
            drop = jnp.logical_and(blk == lo, crow_f + k0.astype(F32) > jh)
            s_sc[pl.ds(k0, tc), :] = jnp.where(drop, -jnp.inf, blk)
            return 0

        lax.fori_loop(0, ncb, fix_body, 0)

    def att_scores(kb):
        return _dot(akv_ref[pl.ds(pl.multiple_of(kb * tk, tk), tk), :], aqt_ref[...])

    def att_update(kb, lg_ref, live, tail, state):
        k0 = pl.multiple_of(kb * tk, tk)
        keep = s_sc[pl.ds(k0, tk), :] >= jnp.where(live, lo, jnp.inf)
        return _attend_update(lg_ref, jnp.where(keep, 0.0, MASKED), avt_ref[:, pl.ds(k0, tk)], state)

    state = _attend_pipelined(nkb, att_scores, att_update, lg_a, lg_b, _attend_init(nh, A_HEAD_DIM))
    o_t = _attend_finish(state, A_HEAD_DIM)
    for p in range(nh // 2):
        pair = jnp.concatenate([o_t[:, (2 * p) * tq:(2 * p + 1) * tq], o_t[:, (2 * p + 1) * tq:(2 * p + 2) * tq]], axis=0)
        out_ref[:, p * LANES:(p + 1) * LANES] = pair.T.astype(out_ref.dtype)


def _dsa(aqt, iqt, iwt, akv, avt, ikp):
    bsz, nqb, _, hq = aqt.shape
    nh = iwt.shape[2]
    seq = akv.shape[1]
    tq = ATT_TQ
    qblk = lambda b, j: (b, j, 0, 0)
    full = lambda b, j: (b, 0, 0)
    return pl.pallas_call(
        _dsa_kernel,
        grid=(bsz, nqb),
        in_specs=[pl.BlockSpec((None, None, LANES, hq), qblk), pl.BlockSpec((None, None, LANES, hq), qblk),
                  pl.BlockSpec((None, None, nh, tq), qblk),
                  pl.BlockSpec((None, seq, LANES), full), pl.BlockSpec((None, A_HEAD_DIM + ONES_ROWS, seq), full),
                  pl.BlockSpec((None, seq, LANES), full)],
        out_specs=pl.BlockSpec((None, tq, nh * A_HEAD_DIM), lambda b, j: (b, j, 0)),
        out_shape=jax.ShapeDtypeStruct((bsz, seq, nh * A_HEAD_DIM), BF16),
        scratch_shapes=[pltpu.VMEM((seq, tq), F32), pltpu.VMEM((ATT_TK, hq), F32), pltpu.VMEM((ATT_TK, hq), F32)],
        compiler_params=pltpu.CompilerParams(
            dimension_semantics=("arbitrary", "arbitrary"), vmem_limit_bytes=VMEM_LIMIT),
        name="dsa",
    )(aqt, iqt, iwt, akv, avt, ikp)


def _mla_kernel(qt_ref, ckv_ref, latt_ref, wuvt_ref, out_ref, lg_a, lg_b):
    nh = wuvt_ref.shape[0]
    tq, tk = ATT_TQ, ATT_TK
    j = pl.program_id(1)
    krow = lax.broadcasted_iota(jnp.int32, (tk, tq), 0)
    qcol = j * tq + lax.broadcasted_iota(jnp.int32, (1, tq), 1)

    def scores(kb):
        return _dot(ckv_ref[pl.ds(pl.multiple_of(kb * tk, tk), tk), :], qt_ref[...])

    def update(kb, lg_ref, live, tail, state):
        k0 = pl.multiple_of(kb * tk, tk)
        bias = None
        if tail:
            keep = (k0 + krow) <= jnp.where(live, qcol, -1)
            bias = jnp.where(keep, 0.0, MASKED)
        return _attend_update(lg_ref, bias, latt_ref[:, pl.ds(k0, tk)], state)

    state = _attend_pipelined((j * tq) // tk + 1, scores, update, lg_a, lg_b, _attend_init(nh, KV_LORA))
    o_lat_t = _attend_finish(state, KV_LORA).astype(BF16)
    for p in range(nh // 2):
        pair = jnp.concatenate([_dot(wuvt_ref[h], o_lat_t[:, h * tq:(h + 1) * tq]) for h in (2 * p, 2 * p + 1)], axis=0)
        out_ref[:, p * LANES:(p + 1) * LANES] = pair.T.astype(out_ref.dtype)


def _mla(qabst, ckv, latt, w_uv_t):
    bsz, nqb, dq, hq = qabst.shape
    nh = w_uv_t.shape[0]
    seq = ckv.shape[1]
    tq = ATT_TQ
    return pl.pallas_call(
        _mla_kernel,
        grid=(bsz, nqb),
        in_specs=[pl.BlockSpec((None, None, dq, hq), lambda b, j: (b, j, 0, 0)),
                  pl.BlockSpec((None, seq, dq), lambda b, j: (b, 0, 0)),
                  pl.BlockSpec((None, KV_LORA + ONES_ROWS, seq), lambda b, j: (b, 0, 0)),
                  pl.BlockSpec(w_uv_t.shape, lambda b, j: (0, 0, 0))],
        out_specs=pl.BlockSpec((None, tq, nh * B_V), lambda b, j: (b, j, 0)),
        out_shape=jax.ShapeDtypeStruct((bsz, seq, nh * B_V), BF16),
        scratch_shapes=[pltpu.VMEM((ATT_TK, hq), F32), pltpu.VMEM((ATT_TK, hq), F32)],
        compiler_params=pltpu.CompilerParams(
            dimension_semantics=("arbitrary", "arbitrary"), vmem_limit_bytes=VMEM_LIMIT),
        name="mla",
    )(qabst, ckv, latt, w_uv_t)


def _layer_norm(z, g, b):
    mu = jnp.mean(z, axis=1, keepdims=True)
    d = z - mu
    var = jnp.mean(d * d, axis=1, keepdims=True)
    return d * lax.rsqrt(var + LN_EPS) * g + b


def _split_bf16(v):
    hi = v.astype(BF16)
    return hi, (v - hi.astype(F32)).astype(BF16)


def _to_row_tiles(v, ref):
    for c in range(v.shape[1] // LANES):
        ref[:, c, :] = v[:, c * LANES:(c + 1) * LANES]


def _from_row_tiles(ref):
    return jnp.concatenate([ref[:, c, :] for c in range(ref.shape[1])], axis=1)


def _mix_kernel(x_ref, a_ref, b_ref, wout_ref, g_ref, beta_ref, wr_ref, br_ref,
                h_ref, ti_ref, tg_ref, rank_ref, cnt_ref, carry_sc, *, alpha):
    tm = x_ref.shape[0]
    half = a_ref.shape[1]
    i = pl.program_id(0)

    @pl.when(i == 0)
    def _():
        carry_sc[...] = jnp.zeros(carry_sc.shape, F32)

    mix = _dot(a_ref[...], wout_ref[0:half, :]) + _dot(b_ref[...], wout_ref[half:2 * half, :])
    h = _layer_norm(alpha * x_ref[...] + mix, g_ref[...], beta_ref[...])
    _to_row_tiles(h, h_ref)
    h_hi, h_lo = _split_bf16(h)
    w_hi, w_lo = _split_bf16(wr_ref[...])
    logits = _dot(h_hi, w_hi) + (_dot(h_lo, w_hi) + _dot(h_hi, w_lo)) + br_ref[...]
    lane = _lane_iota((tm, LANES))
    logits = jnp.where(lane < N_EXPERTS, logits, -jnp.inf)
    ids = jnp.zeros((tm, LANES), jnp.int32)
    vals, hots = [], []
    for k in range(TOP_K):
        m = jnp.max(logits, axis=1, keepdims=True)
        idx = jnp.min(jnp.where(logits == m, lane, LANES), axis=1, keepdims=True)
        hot = lane == idx
        ids = jnp.where(lane == k, idx, ids)
        logits = jnp.where(hot, -jnp.inf, logits)
        vals.append(m)
        hots.append(hot)
    es = [jnp.exp(v - vals[0]) for v in vals]
    den = (es[0] + es[1]) + (es[2] + es[3])
    ti_ref[...] = ids[:, :TOP_K]
    for k in range(TOP_K):
        tg_ref[:, k, :] = jnp.broadcast_to(es[k] / den, (tm, LANES))

    onehot = jnp.where(jnp.logical_or(jnp.logical_or(hots[0], hots[1]), jnp.logical_or(hots[2], hots[3])), 1.0, 0.0)
    r_i = lax.broadcasted_iota(jnp.int32, (tm, tm), 0)
    c_i = lax.broadcasted_iota(jnp.int32, (tm, tm), 1)
    before = _dot(jnp.where(c_i < r_i, 1.0, 0.0).astype(BF16), onehot.astype(BF16)) + carry_sc[...]
    ranks = jnp.zeros((tm, LANES), F32)
    for k in range(TOP_K):
        ranks = jnp.where(lane == k, jnp.sum(jnp.where(hots[k], before, 0.0), axis=1, keepdims=True), ranks)
    rank_ref[...] = ranks[:, :TOP_K].astype(jnp.int32)
    carry_sc[...] = carry_sc[...] + jnp.sum(onehot, axis=0, keepdims=True)
    cnt_ref[...] = carry_sc[...].astype(jnp.int32)


def _mix(x2, a_out, b_out, w_out, g, beta, w_router, b_router, alpha):
    n, d = x2.shape
    tm = MIX_TM
    half = a_out.shape[1]
    row = lambda i: (i, 0)
    const = lambda i: (0, 0)
    wr = jnp.pad(w_router, ((0, 0), (0, LANES - N_EXPERTS)))
    br = jnp.pad(b_router, (0, LANES - N_EXPERTS)).reshape(1, LANES)
    return pl.pallas_call(
        functools.partial(_mix_kernel, alpha=alpha),
        grid=(n // tm,),
        in_specs=[pl.BlockSpec((tm, d), row), pl.BlockSpec((tm, half), row), pl.BlockSpec((tm, half), row),
                  pl.BlockSpec(w_out.shape, const), pl.BlockSpec((1, d), const), pl.BlockSpec((1, d), const),
                  pl.BlockSpec(wr.shape, const), pl.BlockSpec((1, LANES), const)],
        out_specs=[pl.BlockSpec((tm, d // LANES, LANES), lambda i: (i, 0, 0)), pl.BlockSpec((tm, TOP_K), row),
                   pl.BlockSpec((tm, TOP_K, LANES), lambda i: (i, 0, 0)), pl.BlockSpec((tm, TOP_K), row),
                   pl.BlockSpec((1, LANES), const)],
        out_shape=[jax.ShapeDtypeStruct((n, d // LANES, LANES), F32), jax.ShapeDtypeStruct((n, TOP_K), jnp.int32),
                   jax.ShapeDtypeStruct((n, TOP_K, LANES), F32), jax.ShapeDtypeStruct((n, TOP_K), jnp.int32),
                   jax.ShapeDtypeStruct((1, LANES), jnp.int32)],
        scratch_shapes=[pltpu.VMEM((1, LANES), F32)],
        compiler_params=pltpu.CompilerParams(dimension_semantics=("arbitrary",), vmem_limit_bytes=VMEM_LIMIT),
        name="mix",
    )(x2, a_out, b_out, w_out.astype(BF16), g.reshape(1, d), beta.reshape(1, d), wr, br)


def _route(top_i, rank, counts, tm):
    n = top_i.shape[0]
    n_tiles = (n * TOP_K) // tm + N_EXPERTS
    counts = counts[0, :N_EXPERTS]
    tiles_per = (counts + tm - 1) // tm
    tile_end = jnp.cumsum(tiles_per)
    pstart = (tile_end - tiles_per) * tm
    pos = rank + jnp.sum(jnp.where(top_i[..., None] == jnp.arange(N_EXPERTS, dtype=jnp.int32), pstart, 0), axis=-1)
    tile_ids = jnp.arange(n_tiles, dtype=jnp.int32)
    n_valid = tile_end[-1]
    tile_expert = jnp.sum(tile_end[None, :] <= jnp.minimum(tile_ids, n_valid - 1)[:, None], axis=1).astype(jnp.int32)
    tile_src = jnp.minimum(tile_ids, n_valid - 1)
    tile_valid = (tile_ids < n_valid).astype(jnp.int32)
    pad_start = jnp.concatenate([pstart + counts, n_valid[None]]).astype(jnp.int32)
    return pos.astype(jnp.int32), tile_expert, tile_src.astype(jnp.int32), tile_valid, pad_start, n_tiles


def _dispatch_kernel(pad_ref, h_ref, pos_hbm, xs_hbm, pos_s, zbuf, sem_idx, sem_row):
    i = pl.program_id(0)
    tm = h_ref.shape[0]

    @pl.when(i == 0)
    def _():
        zt = zbuf.shape[0]
        zbuf[...] = jnp.zeros(zbuf.shape, zbuf.dtype)
        for e in range(N_EXPERTS):
            cpe = pltpu.make_async_copy(zbuf, xs_hbm.at[pl.ds(pad_ref[e], zt)], sem_row)
            cpe.start()
            cpe.wait()

        def fill(t, _):
            cpz = pltpu.make_async_copy(zbuf, xs_hbm.at[pl.ds(pl.multiple_of(t * zt, zt), zt)], sem_row)
            cpz.start()
            cpz.wait()
            return 0

        lax.fori_loop(pad_ref[N_EXPERTS], xs_hbm.shape[0] // zt, fill, 0)

    cp = pltpu.make_async_copy(pos_hbm.at[i], pos_s, sem_idx)
    cp.start()
    cp.wait()

    def body(r, _):
        for s in range(TOP_K):
            pltpu.make_async_copy(h_ref.at[r], xs_hbm.at[pos_s[r * TOP_K + s]], sem_row).start(priority=s % 2)
        return 0

    lax.fori_loop(0, tm, body, 0, unroll=8)
    for s in range(TOP_K):
        pltpu.make_async_copy(h_ref, xs_hbm.at[pl.ds(0, tm)], sem_row).wait()


def _dispatch(h3, pos, pad_start, n_rows, zrows):
    n, nt8, _ = h3.shape
    tm = ROW_TM
    return pl.pallas_call(
        _dispatch_kernel,
        grid_spec=pltpu.PrefetchScalarGridSpec(
            num_scalar_prefetch=1,
            grid=(n // tm,),
            in_specs=[pl.BlockSpec((tm, nt8, LANES), lambda i, pad: (i, 0, 0)), pl.BlockSpec(memory_space=pl.ANY)],
            out_specs=pl.BlockSpec(memory_space=pl.ANY),
            scratch_shapes=[pltpu.SMEM((tm * TOP_K,), jnp.int32), pltpu.VMEM((zrows, nt8, LANES), F32),
                            pltpu.SemaphoreType.DMA, pltpu.SemaphoreType.DMA]),
        out_shape=jax.ShapeDtypeStruct((n_rows + zrows, nt8, LANES), F32),
        compiler_params=pltpu.CompilerParams(dimension_semantics=("arbitrary",), vmem_limit_bytes=VMEM_LIMIT),
        name="dispatch",
    )(pad_start, h3, pos.reshape(n // tm, tm * TOP_K))


def _moe_kernel(te_ref, ts_ref, tv_ref, xs_ref, wg_ref, bg_ref, wu_ref, bu_ref, wd_ref, bd_ref, ys_ref):
    del te_ref, ts_ref
    i = pl.program_id(0)

    @pl.when(tv_ref[i] > 0)
    def _():
        x = _from_row_tiles(xs_ref).astype(BF16)
        g = jnp.minimum(_dot(x, wg_ref[0]) + bg_ref[0], SWIGLU_LIMIT)
        u = jnp.clip(_dot(x, wu_ref[0]) + bu_ref[0], -SWIGLU_LIMIT, SWIGLU_LIMIT)
        act = g * (1.0 / (1.0 + jnp.exp(-SWIGLU_ALPHA * g))) * (u + 1.0)
        _to_row_tiles(_dot(act.astype(BF16), wd_ref[0]) + bd_ref[0], ys_ref)

    @pl.when(tv_ref[i] == 0)
    def _():
        ys_ref[...] = jnp.zeros(ys_ref.shape, ys_ref.dtype)


def _moe(xs, tile_expert, tile_src, tile_valid, n_tiles, w_gate, b_gate, w_up, b_up, w_down, b_down):
    _, nt8, _ = xs.shape
    tm = MOE_TM
    ne, d, dff = w_gate.shape
    wmap = lambda i, te, ts, tv: (te[i], 0, 0)
    grid_spec = pltpu.PrefetchScalarGridSpec(
        num_scalar_prefetch=3,
        grid=(n_tiles,),
        in_specs=[pl.BlockSpec((tm, nt8, LANES), lambda i, te, ts, tv: (ts[i], 0, 0)),
                  pl.BlockSpec((1, d, dff), wmap), pl.BlockSpec((1, 1, dff), wmap),
                  pl.BlockSpec((1, d, dff), wmap), pl.BlockSpec((1, 1, dff), wmap),
                  pl.BlockSpec((1, dff, d), wmap), pl.BlockSpec((1, 1, d), wmap)],
        out_specs=pl.BlockSpec((tm, nt8, LANES), lambda i, te, ts, tv: (i, 0, 0)))
    return pl.pallas_call(
        _moe_kernel,
        grid_spec=grid_spec,
        out_shape=jax.ShapeDtypeStruct((n_tiles * tm, nt8, LANES), F32),
        compiler_params=pltpu.CompilerParams(dimension_semantics=("arbitrary",), vmem_limit_bytes=VMEM_LIMIT),
        name="moe",
    )(tile_expert, tile_src, tile_valid, xs,
      w_gate.astype(BF16), b_gate.reshape(ne, 1, dff), w_up.astype(BF16), b_up.reshape(ne, 1, dff),
      w_down.astype(BF16), b_down.reshape(ne, 1, d))


def _combine_kernel(h_ref, gate_ref, pos_hbm, ys_hbm, g_ref, beta_ref, o_ref, pos_s, ybuf, sem_idx, sem_row, *, alpha):
    i = pl.program_id(0)
    tm = h_ref.shape[0]
    cp = pltpu.make_async_copy(pos_hbm.at[i], pos_s, sem_idx)
    cp.start()
    cp.wait()

    def body(r, _):
        for s in range(TOP_K):
            pltpu.make_async_copy(ys_hbm.at[pos_s[r * TOP_K + s]], ybuf.at[s, r], sem_row).start(priority=s % 2)
        return 0

    lax.fori_loop(0, tm, body, 0, unroll=8)
    for s in range(TOP_K):
        pltpu.make_async_copy(ys_hbm.at[pl.ds(0, tm)], ybuf.at[s], sem_row).wait()

    z = alpha * h_ref[...]
    for s in range(TOP_K):
        z = z + gate_ref[:, s:s + 1, :] * ybuf[s]
    inv_d = 1.0 / (z.shape[1] * z.shape[2])
    mu = jnp.sum(z, axis=(1, 2), keepdims=True) * inv_d
    dz = z - mu
    var = jnp.sum(dz * dz, axis=(1, 2), keepdims=True) * inv_d
    o_ref[...] = dz * lax.rsqrt(var + LN_EPS) * g_ref[...] + beta_ref[...]


def _combine(h3, gates, pos, ys, g, beta, alpha):
    n, nt8, _ = h3.shape
    tm = ROW_TM
    tile = lambda i: (i, 0, 0)
    const = lambda i: (0, 0, 0)
    return pl.pallas_call(
        functools.partial(_combine_kernel, alpha=alpha),
        grid=(n // tm,),
        in_specs=[pl.BlockSpec((tm, nt8, LANES), tile), pl.BlockSpec((tm, TOP_K, LANES), tile),
                  pl.BlockSpec(memory_space=pl.ANY), pl.BlockSpec(memory_space=pl.ANY),
                  pl.BlockSpec((1, nt8, LANES), const), pl.BlockSpec((1, nt8, LANES), const)],
        out_specs=pl.BlockSpec((tm, nt8, LANES), tile),
        out_shape=jax.ShapeDtypeStruct((n, nt8, LANES), F32),
        scratch_shapes=[pltpu.SMEM((tm * TOP_K,), jnp.int32), pltpu.VMEM((TOP_K, tm, nt8, LANES), F32),
                        pltpu.SemaphoreType.DMA, pltpu.SemaphoreType.DMA],
        compiler_params=pltpu.CompilerParams(dimension_semantics=("arbitrary",), vmem_limit_bytes=VMEM_LIMIT),
        name="combine",
    )(h3, gates, pos.reshape(n // tm, tm * TOP_K), ys, g.reshape(1, nt8, LANES), beta.reshape(1, nt8, LANES))


def _layer(x, tabs, alpha, w_in, ik_g, ik_b, q_g, w_q_up, kv_g, w_kv_up, w_out, ln1_g, ln1_b,
           w_router, b_router, w_gate, b_gate, w_up, b_up, w_down, b_down, ln2_g, ln2_b):
    bsz, seq, d = x.shape
    w_t, w_n, wq_t, bd_uk_t, w_uv_t = _pack_weights(w_in, w_q_up, w_kv_up)
    aqt, iqt, iwt, akv, avt, ikp, qabst, ckv, latt = _proj(x, tabs, w_t, w_n, wq_t, bd_uk_t, ik_g, ik_b, q_g, kv_g)
    a_out = _dsa(aqt, iqt, iwt, akv, avt, ikp)
    b_out = _mla(qabst, ckv, latt, w_uv_t)
    n = bsz * seq
    h3, top_i, gates, rank, counts = _mix(x.reshape(n, d), a_out.reshape(n, -1), b_out.reshape(n, -1), w_out,
                                          ln1_g, ln1_b, w_router, b_router, alpha)
    pos, tile_expert, tile_src, tile_valid, pad_start, n_tiles = _route(top_i, rank, counts, MOE_TM)
    xs = _dispatch(h3, pos, pad_start, n_tiles * MOE_TM, MOE_TM)
    ys = _moe(xs, tile_expert, tile_src, tile_valid, n_tiles, w_gate, b_gate, w_up, b_up, w_down, b_down)
    return _combine(h3, gates, pos, ys, ln2_g, ln2_b, alpha).reshape(bsz, seq, d)


def kernel(x, positions, w_mix_in, idx_k_norm_g, idx_k_norm_b, mla_q_norm_g, mla_w_q_up, mla_kv_norm_g, mla_w_kv_up, w_mix_out, ln1_g, ln1_b, w_router, b_router, w_gate, b_gate, w_up, b_up, w_down, b_down, ln2_g, ln2_b):
    depth = w_mix_in.shape[0]
    alpha = float((2 * depth) ** 0.25)
    tabs = _rope_tables(positions)
    for l in range(depth):
        x = _layer(x, tabs, alpha, w_mix_in[l], idx_k_norm_g[l], idx_k_norm_b[l], mla_q_norm_g[l], mla_w_q_up[l],
                   mla_kv_norm_g[l], mla_w_kv_up[l], w_mix_out[l], ln1_g[l], ln1_b[l], w_router[l], b_router[l],
                   w_gate[l], b_gate[l], w_up[l], b_up[l], w_down[l], b_down[l], ln2_g[l], ln2_b[l])
    return x
```

```python
import functools

import jax
import jax.numpy as jnp
import numpy as np
from jax import lax
from jax.experimental import pallas as pl
from jax.experimental.pallas import tpu as pltpu

F32 = jnp.float32
BF16 = jnp.bfloat16

A_HEADS, A_HEAD_DIM = 8, 64
IDX_HEADS, IDX_DIM = 8, 64
TOPK_MAX = 256
B_HEADS, B_NOPE, B_ROPE, B_V = 8, 64, 32, 64
Q_LORA, KV_LORA = 256, 128
N_EXPERTS, TOP_K = 32, 4
SWIGLU_LIMIT, SWIGLU_ALPHA = 7.0, 1.702
ROPE_THETA = 10000.0
LN_EPS, RMS_EPS = 1e-5, 1e-6

LANES = 128
SUBLANES = 8
VMEM_LIMIT = 56 * 1024 * 1024

PROJ_TM = 512
ATT_TQ = LANES
ATT_TK = 512
SCORE_TK = 1024
SEL_TK = 512
ONES_ROWS = 16
LOG2_E = 1.4426950408889634
MIX_TM = 512
MOE_TM = 256
ROW_TM = 256
SEL_MAX_ITERS = 400
SEL_UNROLL = 4

MASKED = -1e30
M_FLOOR = -1e29
TINY = float(np.finfo(np.float32).tiny)


def _dot(a, b):
    return jnp.dot(a, b, preferred_element_type=F32)


def _dot_t(a, b):
    return lax.dot_general(a, b, (((1,), (1,)), ((), ())), preferred_element_type=F32)


def _lane_iota(shape):
    return lax.broadcasted_iota(jnp.int32, shape, len(shape) - 1)


def _fold(v, op, rows=64):
    n = v.shape[0] // rows
    acc = v[0:rows]
    for i in range(1, n):
        acc = op(acc, v[i * rows:(i + 1) * rows])
    return acc


def _rows_reduce(v, op):
    if op == "sum":
        return jnp.sum(v, axis=0, keepdims=True)
    if op == "max":
        return jnp.max(v, axis=0, keepdims=True)
    return jnp.min(v, axis=0, keepdims=True)


_R_AQ, _R_AQR, _R_IQ, _R_IQR, _R_V, _R_IW, _R_QC, _R_KVC, _R_TOTAL = 0, 512, 1024, 1536, 2048, 2112, 2128, 2384, 2512
_C_KV, _C_KVR, _C_IK, _C_KVC, _C_KR, _C_KRR, _C_TOTAL = 0, 128, 256, 384, 512, 640, 768


def _rot_cols(w, dim):
    k, n = w.shape
    w3 = w.reshape(k, n // dim, dim)
    return jnp.concatenate([-w3[..., dim // 2:], w3[..., :dim // 2]], axis=-1).reshape(k, n)


def _pad_cols(w, width):
    return jnp.pad(w, ((0, 0), (0, width - w.shape[1])))


def _pack_weights(w_in, w_q_up, w_kv_up):
    o = np.cumsum((0, 512, 64, 64, 512, 64, 8, 256, 128, 32))
    a_q, a_k, a_v, i_q, i_k, i_w, b_qc, b_kvc, b_kr = [w_in[:, o[i]:o[i + 1]] for i in range(9)]
    w_t = jnp.concatenate([a_q, _rot_cols(a_q, 64), i_q, _rot_cols(i_q, 64), a_v, _pad_cols(i_w, 16),
                           b_qc, b_kvc], axis=1).T
    assert w_t.shape[0] == _R_TOTAL
    w_n = jnp.concatenate([a_k, a_v, _rot_cols(a_k, 64), jnp.zeros_like(a_k), _pad_cols(i_k, 128), b_kvc,
                           _pad_cols(b_kr, 128), _pad_cols(_rot_cols(b_kr, 32), 128)], axis=1)
    assert w_n.shape[1] == _C_TOTAL
    wq = w_q_up.reshape(Q_LORA, B_HEADS, B_NOPE + B_ROPE)
    wq_rope = wq[..., B_NOPE:].reshape(Q_LORA, B_HEADS * B_ROPE)
    wq_t = jnp.concatenate([wq[..., :B_NOPE].reshape(Q_LORA, B_HEADS * B_NOPE), wq_rope,
                            _rot_cols(wq_rope, B_ROPE)], axis=1).T
    wkv = w_kv_up.reshape(KV_LORA, B_HEADS, B_NOPE + B_V)
    eye = jnp.eye(B_HEADS, dtype=w_kv_up.dtype)
    bd_uk_t = jnp.einsum('lhd,hg->glhd', wkv[..., :B_NOPE], eye).reshape(B_HEADS * KV_LORA, B_HEADS * B_NOPE)
    w_uv_t = jnp.transpose(wkv[..., B_NOPE:], (1, 2, 0))
    return w_t.astype(BF16), w_n.astype(BF16), wq_t.astype(BF16), bd_uk_t.astype(BF16), w_uv_t.astype(BF16)


def _rope_tables(positions):
    pos = positions.astype(F32)[..., None]
    inv64 = ROPE_THETA ** (-jnp.arange(0, 64, 2, dtype=F32) / 64)
    inv32 = ROPE_THETA ** (-jnp.arange(0, 32, 2, dtype=F32) / 32)
    c64, s64 = jnp.cos(pos * inv64), jnp.sin(pos * inv64)
    c32, s32 = jnp.cos(pos * inv32), jnp.sin(pos * inv32)
    nat = (jnp.tile(c64, 4), jnp.tile(s64, 4), jnp.tile(c32, 8), jnp.tile(s32, 8))
    tr = tuple(jnp.swapaxes(jnp.tile(t, 2), 1, 2) for t in (c64, s64, c32, s32))
    return nat + tr


def _proj_kernel(x_ref, ca_ref, sa_ref, cr_ref, sr_ref, cat_ref, sat_ref, crt_ref, srt_ref,
                 wt_ref, wn_ref, wqt_ref, bdukt_ref, ikg_ref, ikb_ref, qg_ref, kvg_ref, kvgt_ref,
                 aqt_ref, iqt_ref, iwt_ref, akv_ref, avt_ref, ik_ref, qabst_ref, ckv_ref, latt_ref):
    tm = x_ref.shape[0]
    nqb = tm // ATT_TQ
    xb = x_ref[...].astype(BF16)
    ca, sa, cr, sr = ca_ref[...], sa_ref[...], cr_ref[...], sr_ref[...]
    lane = _lane_iota((tm, LANES))
    low64 = lane < 64

    def proj_t(r0, rows):
        return _dot_t(wt_ref[r0:r0 + rows, :], xb)

    def proj_n(c0, width):
        return _dot(xb, wn_ref[:, c0:c0 + width])

    cat8 = jnp.concatenate([cat_ref[...]] * 8, axis=0)
    sat8 = jnp.concatenate([sat_ref[...]] * 8, axis=0)

    def split_heads(v, out_ref):
        vb = v.astype(BF16)
        zeros = jnp.zeros((64, ATT_TQ), BF16)
        for c in range(nqb):
            for h in range(8):
                out_ref[c, 0:64, h * ATT_TQ:(h + 1) * ATT_TQ] = vb[64 * h:64 * h + 64, c * ATT_TQ:(c + 1) * ATT_TQ]
                out_ref[c, 64:128, h * ATT_TQ:(h + 1) * ATT_TQ] = zeros

    aq_t = (proj_t(_R_AQ, 512) * cat8 + proj_t(_R_AQR, 512) * sat8) * (A_HEAD_DIM ** -0.5 * LOG2_E)
    split_heads(aq_t, aqt_ref)
    iq_t = proj_t(_R_IQ, 512) * cat8 + proj_t(_R_IQR, 512) * sat8
    split_heads(iq_t, iqt_ref)
    iw_t = proj_t(_R_IW, 16)[0:IDX_HEADS] * ((IDX_HEADS * IDX_DIM) ** -0.5)
    for c in range(nqb):
        iwt_ref[c] = iw_t[:, c * ATT_TQ:(c + 1) * ATT_TQ]
    ones_rows = jnp.where(lax.broadcasted_iota(jnp.int32, (ONES_ROWS, tm), 0) == 0, 1.0, 0.0).astype(BF16)
    avt_ref[...] = jnp.concatenate([proj_t(_R_V, 64).astype(BF16), ones_rows], axis=0)

    kc = jnp.where(low64, ca, 1.0)
    ks = jnp.where(low64, sa, 0.0)
    akv_ref[...] = (proj_n(_C_KV, 128) * kc + proj_n(_C_KVR, 128) * ks).astype(BF16)

    y = proj_n(_C_IK, 128)
    mu = jnp.sum(y, axis=1, keepdims=True) * (1.0 / IDX_DIM)
    d = jnp.where(low64, y - mu, 0.0)
    var = jnp.sum(d * d, axis=1, keepdims=True) * (1.0 / IDX_DIM)
    yn = d * lax.rsqrt(var + LN_EPS) * ikg_ref[...] + ikb_ref[...]
    rot = jnp.where(lane < 32, -pltpu.roll(yn, LANES - 32, 1), pltpu.roll(yn, 32, 1))
    ik_ref[...] = jnp.where(low64, yn * ca + rot * sa, 0.0).astype(BF16)

    qc_t = proj_t(_R_QC, Q_LORA)
    qn_t = (qc_t * lax.rsqrt(jnp.mean(qc_t * qc_t, axis=0, keepdims=True) + RMS_EPS) * qg_ref[...]).astype(BF16)
    q_t = _dot(wqt_ref[...], qn_t)
    scale = (B_NOPE + B_ROPE) ** -0.5 * LOG2_E
    q_lat_t = (_dot(bdukt_ref[...], q_t[0:512].astype(BF16)) * scale).astype(BF16)
    crt8 = jnp.concatenate([crt_ref[...]] * 8, axis=0)
    srt8 = jnp.concatenate([srt_ref[...]] * 8, axis=0)
    q_r_t = ((q_t[512:768] * crt8 + q_t[768:1024] * srt8) * scale).astype(BF16)
    zeros = jnp.zeros((2 * LANES - KV_LORA - B_ROPE, ATT_TQ), BF16)
    for c in range(nqb):
        tok = slice(c * ATT_TQ, (c + 1) * ATT_TQ)
        for h in range(B_HEADS):
            col = slice(h * ATT_TQ, (h + 1) * ATT_TQ)
            qabst_ref[c, 0:KV_LORA, col] = q_lat_t[h * KV_LORA:(h + 1) * KV_LORA, tok]
            qabst_ref[c, KV_LORA:KV_LORA + B_ROPE, col] = q_r_t[h * B_ROPE:(h + 1) * B_ROPE, tok]
            qabst_ref[c, KV_LORA + B_ROPE:2 * LANES, col] = zeros

    kvc = proj_n(_C_KVC, KV_LORA)
    kvn = kvc * lax.rsqrt(jnp.mean(kvc * kvc, axis=1, keepdims=True) + RMS_EPS) * kvg_ref[...]
    kr = proj_n(_C_KR, 128) * cr + proj_n(_C_KRR, 128) * sr
    ckv_ref[...] = jnp.concatenate([kvn, kr], axis=1).astype(BF16)
    kvc_t = proj_t(_R_KVC, KV_LORA)
    lat_t = (kvc_t * lax.rsqrt(jnp.mean(kvc_t * kvc_t, axis=0, keepdims=True) + RMS_EPS) * kvgt_ref[...]).astype(BF16)
    latt_ref[...] = jnp.concatenate([lat_t, ones_rows], axis=0)


def _proj(x, tabs, w_t, w_n, wq_t, bd_uk_t, ik_g, ik_b, q_g, kv_g):
    bsz, seq, d = x.shape
    tm = PROJ_TM
    nqb = tm // ATT_TQ
    tok = lambda b, i: (b, i, 0)
    feat = lambda b, i: (b, 0, i)
    qblk = lambda b, i: (b, i, 0, 0)
    const2 = lambda b, i: (0, 0)
    nat_spec = pl.BlockSpec((None, tm, LANES), tok)
    col = lambda v: jnp.broadcast_to(v.reshape(-1, 1), (v.shape[0], tm))
    ik_g = jnp.pad(ik_g, (0, LANES - IDX_DIM)).reshape(1, LANES)
    ik_b = jnp.pad(ik_b, (0, LANES - IDX_DIM)).reshape(1, LANES)
    consts = (w_t, w_n, wq_t, bd_uk_t, ik_g, ik_b, col(q_g), kv_g.reshape(1, KV_LORA), col(kv_g))
    return pl.pallas_call(
        _proj_kernel,
        grid=(bsz, seq // tm),
        in_specs=[pl.BlockSpec((None, tm, d), tok), nat_spec, nat_spec, nat_spec, nat_spec,
                  pl.BlockSpec((None, 64, tm), feat), pl.BlockSpec((None, 64, tm), feat),
                  pl.BlockSpec((None, 32, tm), feat), pl.BlockSpec((None, 32, tm), feat)]
                 + [pl.BlockSpec(c.shape, const2) for c in consts],
        out_specs=[pl.BlockSpec((None, nqb, LANES, A_HEADS * ATT_TQ), qblk),
                   pl.BlockSpec((None, nqb, LANES, IDX_HEADS * ATT_TQ), qblk),
                   pl.BlockSpec((None, nqb, IDX_HEADS, ATT_TQ), qblk),
                   pl.BlockSpec((None, tm, LANES), tok),
                   pl.BlockSpec((None, A_HEAD_DIM + ONES_ROWS, tm), feat),
                   pl.BlockSpec((None, tm, LANES), tok),
                   pl.BlockSpec((None, nqb, 2 * LANES, B_HEADS * ATT_TQ), qblk),
                   pl.BlockSpec((None, tm, 2 * LANES), tok),
                   pl.BlockSpec((None, KV_LORA + ONES_ROWS, tm), feat)],
        out_shape=[jax.ShapeDtypeStruct((bsz, seq // ATT_TQ, LANES, A_HEADS * ATT_TQ), BF16),
                   jax.ShapeDtypeStruct((bsz, seq // ATT_TQ, LANES, IDX_HEADS * ATT_TQ), BF16),
                   jax.ShapeDtypeStruct((bsz, seq // ATT_TQ, IDX_HEADS, ATT_TQ), F32),
                   jax.ShapeDtypeStruct((bsz, seq, LANES), BF16),
                   jax.ShapeDtypeStruct((bsz, A_HEAD_DIM + ONES_ROWS, seq), BF16),
                   jax.ShapeDtypeStruct((bsz, seq, LANES), BF16),
                   jax.ShapeDtypeStruct((bsz, seq // ATT_TQ, 2 * LANES, B_HEADS * ATT_TQ), BF16),
                   jax.ShapeDtypeStruct((bsz, seq, 2 * LANES), BF16),
                   jax.ShapeDtypeStruct((bsz, KV_LORA + ONES_ROWS, seq), BF16)],
        compiler_params=pltpu.CompilerParams(
            dimension_semantics=("arbitrary", "arbitrary"), vmem_limit_bytes=VMEM_LIMIT),
        name="proj",
    )(x, *tabs, *consts)


def _attend_init(nh, val_rows):
    return ([jnp.full((1, LANES), M_FLOOR, F32) for _ in range(nh)],
            jnp.zeros((val_rows + ONES_ROWS, nh * LANES), F32))


def _attend_update(lg_ref, bias, vt_blk, state):
    ms, acc = state
    new_ms, ps, alphas = [], [], []
    for h in range(len(ms)):
        lh = lg_ref[:, h * LANES:(h + 1) * LANES]
        if bias is not None:
            lh = lh + bias
        m_new = jnp.maximum(ms[h], _rows_reduce(_fold(lh, jnp.maximum), "max"))
        alphas.append(jnp.exp2(ms[h] - m_new))
        ps.append(jnp.exp2(lh - m_new).astype(BF16))
        new_ms.append(m_new)
    return new_ms, jnp.concatenate(alphas, axis=1) * acc + _dot(vt_blk, jnp.concatenate(ps, axis=1))


def _attend_pipelined(n_blocks, scores, update, lg_a, lg_b, state):
    last = n_blocks - 1
    n_pairs = (n_blocks + 1) // 2

    def pair(i, st, lookahead):
        kb0 = 2 * i
        lg_b[...] = scores(jnp.minimum(kb0 + 1, last))
        st = update(kb0, lg_a, True, not lookahead, st)
        if lookahead:
            lg_a[...] = scores(kb0 + 2)
        return update(jnp.minimum(kb0 + 1, last), lg_b, kb0 + 1 <= last, not lookahead, st)

    lg_a[...] = scores(0)
    state = lax.fori_loop(0, n_pairs - 1, lambda i, st: pair(i, st, True), state)
    return pair(n_pairs - 1, state, False)


def _attend_finish(state, val_rows):
    acc = state[1]
    return acc[0:val_rows] / acc[val_rows:val_rows + 1]


def _dsa_kernel(aqt_ref, iqt_ref, iwt_ref, akv_ref, avt_ref, ik_ref, out_ref, s_sc, lg_a, lg_b):
    nh = iwt_ref.shape[0]
    tq, tk = ATT_TQ, ATT_TK
    seq = akv_ref.shape[0]
    j = pl.program_id(1)
    nkb = (j * tq) // tk + 1
    iqt = iqt_ref[...]
    w = iwt_ref[...]

    ts = SCORE_TK
    krow = lax.broadcasted_iota(jnp.int32, (ts, tq), 0)
    qcol = j * tq + lax.broadcasted_iota(jnp.int32, (ts, tq), 1)

    def score_body(kb, carry):
        cmax, cmin = carry
        k0 = pl.multiple_of(kb * ts, ts)
        y = _dot(ik_ref[pl.ds(k0, ts), :], iqt)
        s = jnp.zeros((ts, tq), F32)
        for h in range(nh):
            s = s + w[h:h + 1, :] * jnp.maximum(y[:, h * tq:(h + 1) * tq], 0.0)
        causal = (k0 + krow) <= qcol
        s_lo = jnp.where(causal, s, -jnp.inf)
        s_sc[pl.ds(k0, ts), :] = s_lo
        cmax = jnp.maximum(cmax, _fold(s_lo, jnp.maximum))
        cmin = jnp.minimum(cmin, _fold(jnp.where(causal, s, jnp.inf), jnp.minimum))
        return cmax, cmin

    cmax, cmin = lax.fori_loop(0, (j * tq) // ts + 1, score_body,
                               (jnp.full((64, tq), -jnp.inf, F32), jnp.full((64, tq), jnp.inf, F32)))
    rmax = _rows_reduce(cmax, "max")
    rmin = _rows_reduce(cmin, "min")

    n_causal = (j * tq + lax.broadcasted_iota(jnp.int32, (1, tq), 1) + 1).astype(F32)
    kt = jnp.minimum(n_causal, float(TOPK_MAX))

    tc = SEL_TK
    ncb = (j * tq) // tc + 1
    crow_f = lax.broadcasted_iota(jnp.int32, (tc, tq), 0).astype(F32)

    def count(pred):
        def body(cb, acc):
            k0 = pl.multiple_of(cb * tc, tc)
            return acc + _fold(jnp.where(pred(s_sc[pl.ds(k0, tc), :], k0), 1.0, 0.0), jnp.add)
        return _rows_reduce(lax.fori_loop(0, ncb, body, jnp.zeros((64, tq), F32)), "sum")

    def propose(it, lo, hi, clo):
        mid = 0.5 * lo + 0.5 * hi
        mid = jnp.where(jnp.logical_and(it == 0, jnp.logical_and(lo < 0.0, hi > 0.0)), 0.0, mid)
        mid = jnp.where(jnp.logical_and(it == 1, jnp.logical_and(lo == 0.0, hi > TINY)), TINY, mid)
        active = jnp.logical_and(clo != kt, jnp.logical_and(mid > lo, mid < hi))
        return mid, active

    def sel_step(it, lo, hi, clo, chi):
        mid, active = propose(it, lo, hi, clo)
        cm = count(lambda blk, k0: blk >= mid)
        ge = cm >= kt
        up = jnp.logical_and(active, ge)
        dn = jnp.logical_and(active, jnp.logical_not(ge))
        return jnp.where(up, mid, lo), jnp.where(dn, mid, hi), jnp.where(up, cm, clo), jnp.where(dn, cm, chi)

    def sel_body(c):
        it, _, lo, hi, clo, chi = c
        for u in range(SEL_UNROLL):
            lo, hi, clo, chi = sel_step(it + u, lo, hi, clo, chi)
        _, active = propose(it + SEL_UNROLL, lo, hi, clo)
        go = jnp.logical_and(jnp.max(jnp.where(active, 1.0, 0.0)) > 0.0, it + SEL_UNROLL < SEL_MAX_ITERS)
        return it + SEL_UNROLL, go, lo, hi, clo, chi

    lo0 = rmin
    hi0 = rmax + jnp.abs(rmax) + 1e-30
    clo0, chi0 = n_causal, jnp.zeros((1, tq), F32)
    _, active0 = propose(0, lo0, hi0, clo0)
    go0 = jnp.max(jnp.where(active0, 1.0, 0.0)) > 0.0
    _, _, lo, hi, clo, chi = lax.while_loop(lambda c: c[1], sel_body, (jnp.int32(0), go0, lo0, hi0, clo0, chi0))

    tie = clo > kt

    @pl.when(jnp.max(jnp.where(tie, 1.0, 0.0)) > 0.0)
    def _():
        need = kt - chi

        def idx_body(_, c):
            jl, jh = c
            jm = jnp.floor((jl + jh) * 0.5)
            ok = count(lambda blk, k0: jnp.logical_and(blk == lo, crow_f + k0.astype(F32) <= jm)) >= need
            return jnp.where(ok, jl, jm), jnp.where(ok, jm, jh)

        n_steps = int(np.ceil(np.log2(seq))) + 1
        _, jh = lax.fori_loop(0, n_steps, idx_body,
                              (jnp.full((1, tq), -1.0, F32), jnp.full((1, tq), float(seq - 1), F32)))
        jh = jnp.where(tie, jh, float(seq))

        def fix_body(cb, _):
            k0 = pl.multiple_of(cb * tc, tc)
            blk = s_sc[pl.ds(k0, tc), :]
            drop = jnp.logical_and(blk == lo, crow_f + k0.astype(F32) > jh)
            s_sc[pl.ds(k0, tc), :] = jnp.where(drop, -jnp.inf, blk)
            return 0

        lax.fori_loop(0, ncb, fix_body, 0)

    def att_scores(kb):
        return _dot(akv_ref[pl.ds(pl.multiple_of(kb * tk, tk), tk), :], aqt_ref[...])

    def att_update(kb, lg_ref, live, tail, state):
        k0 = pl.multiple_of(kb * tk, tk)
        keep = s_sc[pl.ds(k0, tk), :] >= jnp.where(live, lo, jnp.inf)
        return _attend_update(lg_ref, jnp.where(keep, 0.0, MASKED), avt_ref[:, pl.ds(k0, tk)], state)

    state = _attend_pipelined(nkb, att_scores, att_update, lg_a, lg_b, _attend_init(nh, A_HEAD_DIM))
    o_t = _attend_finish(state, A_HEAD_DIM)
    for p in range(nh // 2):
        pair = jnp.concatenate([o_t[:, (2 * p) * tq:(2 * p + 1) * tq], o_t[:, (2 * p + 1) * tq:(2 * p + 2) * tq]], axis=0)
        out_ref[:, p * LANES:(p + 1) * LANES] = pair.T.astype(out_ref.dtype)


def _dsa(aqt, iqt, iwt, akv, avt, ikp):
    bsz, nqb, _, hq = aqt.shape
    nh = iwt.shape[2]
    seq = akv.shape[1]
    tq = ATT_TQ
    qblk = lambda b, j: (b, j, 0, 0)
    full = lambda b, j: (b, 0, 0)
    return pl.pallas_call(
        _dsa_kernel,
        grid=(bsz, nqb),
        in_specs=[pl.BlockSpec((None, None, LANES, hq), qblk), pl.BlockSpec((None, None, LANES, hq), qblk),
                  pl.BlockSpec((None, None, nh, tq), qblk),
                  pl.BlockSpec((None, seq, LANES), full), pl.BlockSpec((None, A_HEAD_DIM + ONES_ROWS, seq), full),
                  pl.BlockSpec((None, seq, LANES), full)],
        out_specs=pl.BlockSpec((None, tq, nh * A_HEAD_DIM), lambda b, j: (b, j, 0)),
        out_shape=jax.ShapeDtypeStruct((bsz, seq, nh * A_HEAD_DIM), BF16),
        scratch_shapes=[pltpu.VMEM((seq, tq), F32), pltpu.VMEM((ATT_TK, hq), F32), pltpu.VMEM((ATT_TK, hq), F32)],
        compiler_params=pltpu.CompilerParams(
            dimension_semantics=("arbitrary", "arbitrary"), vmem_limit_bytes=VMEM_LIMIT),
        name="dsa",
    )(aqt, iqt, iwt, akv, avt, ikp)


def _mla_kernel(qt_ref, ckv_ref, latt_ref, wuvt_ref, out_ref, lg_a, lg_b):
    nh = wuvt_ref.shape[0]
    tq, tk = ATT_TQ, ATT_TK
    j = pl.program_id(1)
    krow = lax.broadcasted_iota(jnp.int32, (tk, tq), 0)
    qcol = j * tq + lax.broadcasted_iota(jnp.int32, (1, tq), 1)

    def scores(kb):
        return _dot(ckv_ref[pl.ds(pl.multiple_of(kb * tk, tk), tk), :], qt_ref[...])

    def update(kb, lg_ref, live, tail, state):
        k0 = pl.multiple_of(kb * tk, tk)
        bias = None
        if tail:
            keep = (k0 + krow) <= jnp.where(live, qcol, -1)
            bias = jnp.where(keep, 0.0, MASKED)
        return _attend_update(lg_ref, bias, latt_ref[:, pl.ds(k0, tk)], state)

    state = _attend_pipelined((j * tq) // tk + 1, scores, update, lg_a, lg_b, _attend_init(nh, KV_LORA))
    o_lat_t = _attend_finish(state, KV_LORA).astype(BF16)
    for p in range(nh // 2):
        pair = jnp.concatenate([_dot(wuvt_ref[h], o_lat_t[:, h * tq:(h + 1) * tq]) for h in (2 * p, 2 * p + 1)], axis=0)
        out_ref[:, p * LANES:(p + 1) * LANES] = pair.T.astype(out_ref.dtype)


def _mla(qabst, ckv, latt, w_uv_t):
    bsz, nqb, dq, hq = qabst.shape
    nh = w_uv_t.shape[0]
    seq = ckv.shape[1]
    tq = ATT_TQ
    return pl.pallas_call(
        _mla_kernel,
        grid=(bsz, nqb),
        in_specs=[pl.BlockSpec((None, None, dq, hq), lambda b, j: (b, j, 0, 0)),
                  pl.BlockSpec((None, seq, dq), lambda b, j: (b, 0, 0)),
                  pl.BlockSpec((None, KV_LORA + ONES_ROWS, seq), lambda b, j: (b, 0, 0)),
                  pl.BlockSpec(w_uv_t.shape, lambda b, j: (0, 0, 0))],
        out_specs=pl.BlockSpec((None, tq, nh * B_V), lambda b, j: (b, j, 0)),
        out_shape=jax.ShapeDtypeStruct((bsz, seq, nh * B_V), BF16),
        scratch_shapes=[pltpu.VMEM((ATT_TK, hq), F32), pltpu.VMEM((ATT_TK, hq), F32)],
        compiler_params=pltpu.CompilerParams(
            dimension_semantics=("arbitrary", "arbitrary"), vmem_limit_bytes=VMEM_LIMIT),
        name="mla",
    )(qabst, ckv, latt, w_uv_t)


def _layer_norm(z, g, b):
    mu = jnp.mean(z, axis=1, keepdims=True)
    d = z - mu
    var = jnp.mean(d * d, axis=1, keepdims=True)
    return d * lax.rsqrt(var + LN_EPS) * g + b


def _split_bf16(v):
    hi = v.astype(BF16)
    return hi, (v - hi.astype(F32)).astype(BF16)


def _to_row_tiles(v, ref):
    for c in range(v.shape[1] // LANES):
        ref[:, c, :] = v[:, c * LANES:(c + 1) * LANES]


def _from_row_tiles(ref):
    return jnp.concatenate([ref[:, c, :] for c in range(ref.shape[1])], axis=1)


def _mix_kernel(x_ref, a_ref, b_ref, wout_ref, g_ref, beta_ref, wr_ref, br_ref,
                h_ref, ti_ref, tg_ref, rank_ref, cnt_ref, carry_sc, *, alpha):
    tm = x_ref.shape[0]
    half = a_ref.shape[1]
    i = pl.program_id(0)

    @pl.when(i == 0)
    def _():
        carry_sc[...] = jnp.zeros(carry_sc.shape, F32)

    mix = _dot(a_ref[...], wout_ref[0:half, :]) + _dot(b_ref[...], wout_ref[half:2 * half, :])
    h = _layer_norm(alpha * x_ref[...] + mix, g_ref[...], beta_ref[...])
    _to_row_tiles(h, h_ref)
    h_hi, h_lo = _split_bf16(h)
    w_hi, w_lo = _split_bf16(wr_ref[...])
    logits = _dot(h_hi, w_hi) + (_dot(h_lo, w_hi) + _dot(h_hi, w_lo)) + br_ref[...]
    lane = _lane_iota((tm, LANES))
    logits = jnp.where(lane < N_EXPERTS, logits, -jnp.inf)
    ids = jnp.zeros((tm, LANES), jnp.int32)
    vals, hots = [], []
    for k in range(TOP_K):
        m = jnp.max(logits, axis=1, keepdims=True)
        idx = jnp.min(jnp.where(logits == m, lane, LANES), axis=1, keepdims=True)
        hot = lane == idx
        ids = jnp.where(lane == k, idx, ids)
        logits = jnp.where(hot, -jnp.inf, logits)
        vals.append(m)
        hots.append(hot)
    es = [jnp.exp(v - vals[0]) for v in vals]
    den = (es[0] + es[1]) + (es[2] + es[3])
    ti_ref[...] = ids[:, :TOP_K]
    for k in range(TOP_K):
        tg_ref[:, k, :] = jnp.broadcast_to(es[k] / den, (tm, LANES))

    onehot = jnp.where(jnp.logical_or(jnp.logical_or(hots[0], hots[1]), jnp.logical_or(hots[2], hots[3])), 1.0, 0.0)
    r_i = lax.broadcasted_iota(jnp.int32, (tm, tm), 0)
    c_i = lax.broadcasted_iota(jnp.int32, (tm, tm), 1)
    before = _dot(jnp.where(c_i < r_i, 1.0, 0.0).astype(BF16), onehot.astype(BF16)) + carry_sc[...]
    ranks = jnp.zeros((tm, LANES), F32)
    for k in range(TOP_K):
        ranks = jnp.where(lane == k, jnp.sum(jnp.where(hots[k], before, 0.0), axis=1, keepdims=True), ranks)
    rank_ref[...] = ranks[:, :TOP_K].astype(jnp.int32)
    carry_sc[...] = carry_sc[...] + jnp.sum(onehot, axis=0, keepdims=True)
    cnt_ref[...] = carry_sc[...].astype(jnp.int32)


def _mix(x2, a_out, b_out, w_out, g, beta, w_router, b_router, alpha):
    n, d = x2.shape
    tm = MIX_TM
    half = a_out.shape[1]
    row = lambda i: (i, 0)
    const = lambda i: (0, 0)
    wr = jnp.pad(w_router, ((0, 0), (0, LANES - N_EXPERTS)))
    br = jnp.pad(b_router, (0, LANES - N_EXPERTS)).reshape(1, LANES)
    return pl.pallas_call(
        functools.partial(_mix_kernel, alpha=alpha),
        grid=(n // tm,),
        in_specs=[pl.BlockSpec((tm, d), row), pl.BlockSpec((tm, half), row), pl.BlockSpec((tm, half), row),
                  pl.BlockSpec(w_out.shape, const), pl.BlockSpec((1, d), const), pl.BlockSpec((1, d), const),
                  pl.BlockSpec(wr.shape, const), pl.BlockSpec((1, LANES), const)],
        out_specs=[pl.BlockSpec((tm, d // LANES, LANES), lambda i: (i, 0, 0)), pl.BlockSpec((tm, TOP_K), row),
                   pl.BlockSpec((tm, TOP_K, LANES), lambda i: (i, 0, 0)), pl.BlockSpec((tm, TOP_K), row),
                   pl.BlockSpec((1, LANES), const)],
        out_shape=[jax.ShapeDtypeStruct((n, d // LANES, LANES), F32), jax.ShapeDtypeStruct((n, TOP_K), jnp.int32),
                   jax.ShapeDtypeStruct((n, TOP_K, LANES), F32), jax.ShapeDtypeStruct((n, TOP_K), jnp.int32),
                   jax.ShapeDtypeStruct((1, LANES), jnp.int32)],
        scratch_shapes=[pltpu.VMEM((1, LANES), F32)],
        compiler_params=pltpu.CompilerParams(dimension_semantics=("arbitrary",), vmem_limit_bytes=VMEM_LIMIT),
        name="mix",
    )(x2, a_out, b_out, w_out.astype(BF16), g.reshape(1, d), beta.reshape(1, d), wr, br)


def _route(top_i, rank, counts, tm):
    n = top_i.shape[0]
    n_tiles = (n * TOP_K) // tm + N_EXPERTS
    counts = counts[0, :N_EXPERTS]
    tiles_per = (counts + tm - 1) // tm
    tile_end = jnp.cumsum(tiles_per)
    pstart = (tile_end - tiles_per) * tm
    pos = rank + jnp.sum(jnp.where(top_i[..., None] == jnp.arange(N_EXPERTS, dtype=jnp.int32), pstart, 0), axis=-1)
    tile_ids = jnp.arange(n_tiles, dtype=jnp.int32)
    n_valid = tile_end[-1]
    tile_expert = jnp.sum(tile_end[None, :] <= jnp.minimum(tile_ids, n_valid - 1)[:, None], axis=1).astype(jnp.int32)
    n_valid = n_valid[None].astype(jnp.int32)
    pad_start = jnp.concatenate([pstart + counts, n_valid]).astype(jnp.int32)
    return pos.astype(jnp.int32), tile_expert, n_valid, pad_start, n_tiles


def _dispatch_kernel(pad_ref, h_ref, pos_hbm, xs_hbm, pos_s, zbuf, sem_idx, sem_row):
    i = pl.program_id(0)
    tm = h_ref.shape[0]

    def pos_copy(t, slot):
        return pltpu.make_async_copy(pos_hbm.at[t], pos_s.at[pl.ds(slot * (tm * TOP_K), tm * TOP_K)], sem_idx.at[slot])

    @pl.when(i == 0)
    def _():
        zt = zbuf.shape[0]
        zbuf[...] = jnp.zeros(zbuf.shape, zbuf.dtype)
        for e in range(N_EXPERTS):
            cpe = pltpu.make_async_copy(zbuf, xs_hbm.at[pl.ds(pad_ref[e], zt)], sem_row)
            cpe.start()
            cpe.wait()

        def fill(t, _):
            cpz = pltpu.make_async_copy(zbuf, xs_hbm.at[pl.ds(pl.multiple_of(t * zt, zt), zt)], sem_row)
            cpz.start()
            cpz.wait()
            return 0

        lax.fori_loop(pad_ref[N_EXPERTS], xs_hbm.shape[0] // zt, fill, 0)
        pos_copy(0, 0).start()

    slot = i % 2
    pos_copy(i, slot).wait()

    @pl.when(i + 1 < pl.num_programs(0))
    def _():
        pos_copy(i + 1, 1 - slot).start()

    base = slot * (tm * TOP_K)

    def body(r, _):
        for s in range(TOP_K):
            pltpu.make_async_copy(h_ref.at[r], xs_hbm.at[pos_s[base + r * TOP_K + s]], sem_row).start(priority=s % 2)
        return 0

    lax.fori_loop(0, tm, body, 0, unroll=8)
    for s in range(TOP_K):
        pltpu.make_async_copy(h_ref, xs_hbm.at[pl.ds(0, tm)], sem_row).wait()


def _dispatch(h3, pos, pad_start, n_rows, zrows):
    n, nt8, _ = h3.shape
    tm = ROW_TM
    return pl.pallas_call(
        _dispatch_kernel,
        grid_spec=pltpu.PrefetchScalarGridSpec(
            num_scalar_prefetch=1,
            grid=(n // tm,),
            in_specs=[pl.BlockSpec((tm, nt8, LANES), lambda i, pad: (i, 0, 0)), pl.BlockSpec(memory_space=pl.ANY)],
            out_specs=pl.BlockSpec(memory_space=pl.ANY),
            scratch_shapes=[pltpu.SMEM((2 * tm * TOP_K,), jnp.int32), pltpu.VMEM((zrows, nt8, LANES), F32),
                            pltpu.SemaphoreType.DMA((2,)), pltpu.SemaphoreType.DMA]),
        out_shape=jax.ShapeDtypeStruct((n_rows + zrows, nt8, LANES), F32),
        compiler_params=pltpu.CompilerParams(dimension_semantics=("arbitrary",), vmem_limit_bytes=VMEM_LIMIT),
        name="dispatch",
    )(pad_start, h3, pos.reshape(n // tm, tm * TOP_K))


def _moe_kernel(te_ref, nv_ref, xs_hbm, wg_ref, bg_ref, wu_ref, bu_ref, wd_ref, bd_ref, ys_hbm,
                xbuf, ybuf, sem_in, sem_out):
    del te_ref
    i = pl.program_id(0)
    n_valid = nv_ref[0]
    tm = ybuf.shape[0]
    nt8 = xs_hbm.shape[1]

    def in_copy(t, slot, c):
        return pltpu.make_async_copy(xs_hbm.at[pl.ds(t * tm, tm), c, :], xbuf.at[slot, :, pl.ds(c * LANES, LANES)],
                                     sem_in.at[slot])

    def out_copy(t, c):
        return pltpu.make_async_copy(ybuf.at[:, pl.ds(c * LANES, LANES)], ys_hbm.at[pl.ds(t * tm, tm), c, :], sem_out)

    def fetch(t, slot):
        for c in range(nt8):
            in_copy(t, slot, c).start()

    def store(t):
        for c in range(nt8):
            out_copy(t, c).start()

    def wait_store(t):
        for c in range(nt8):
            out_copy(t, c).wait()

    @pl.when(i == 0)
    def _():
        fetch(0, 0)

    @pl.when(i < n_valid)
    def _():
        slot = i % 2
        for c in range(nt8):
            in_copy(i, slot, c).wait()

        @pl.when(i + 1 < n_valid)
        def _():
            fetch(i + 1, 1 - slot)

        x = xbuf[slot].astype(BF16)
        g = jnp.minimum(_dot(x, wg_ref[0]) + bg_ref[0], SWIGLU_LIMIT)
        u = jnp.clip(_dot(x, wu_ref[0]) + bu_ref[0], -SWIGLU_LIMIT, SWIGLU_LIMIT)
        act = g * (1.0 / (1.0 + jnp.exp(-SWIGLU_ALPHA * g))) * (u + 1.0)
        y = _dot(act.astype(BF16), wd_ref[0]) + bd_ref[0]

        @pl.when(i > 0)
        def _():
            wait_store(i - 1)

        ybuf[...] = y
        store(i)

        @pl.when(i == n_valid - 1)
        def _():
            wait_store(i)

    @pl.when(i >= n_valid)
    def _():
        ybuf[...] = jnp.zeros(ybuf.shape, ybuf.dtype)
        store(i)
        wait_store(i)


def _moe(xs, tile_expert, n_valid, n_tiles, w_gate, b_gate, w_up, b_up, w_down, b_down):
    _, nt8, _ = xs.shape
    tm = MOE_TM
    ne, d, dff = w_gate.shape
    wmap = lambda i, te, nv: (te[i], 0, 0)
    any_spec = pl.BlockSpec(memory_space=pl.ANY)
    grid_spec = pltpu.PrefetchScalarGridSpec(
        num_scalar_prefetch=2,
        grid=(n_tiles,),
        in_specs=[any_spec,
                  pl.BlockSpec((1, d, dff), wmap), pl.BlockSpec((1, 1, dff), wmap),
                  pl.BlockSpec((1, d, dff), wmap), pl.BlockSpec((1, 1, dff), wmap),
                  pl.BlockSpec((1, dff, d), wmap), pl.BlockSpec((1, 1, d), wmap)],
        out_specs=any_spec,
        scratch_shapes=[pltpu.VMEM((2, tm, d), F32), pltpu.VMEM((tm, d), F32),
                        pltpu.SemaphoreType.DMA((2,)), pltpu.SemaphoreType.DMA])
    return pl.pallas_call(
        _moe_kernel,
        grid_spec=grid_spec,
        out_shape=jax.ShapeDtypeStruct((n_tiles * tm, nt8, LANES), F32),
        compiler_params=pltpu.CompilerParams(dimension_semantics=("arbitrary",), vmem_limit_bytes=VMEM_LIMIT),
        name="moe",
    )(tile_expert, n_valid, xs,
      w_gate.astype(BF16), b_gate.reshape(ne, 1, dff), w_up.astype(BF16), b_up.reshape(ne, 1, dff),
      w_down.astype(BF16), b_down.reshape(ne, 1, d))


def _combine_kernel(h_ref, gate_ref, pos_hbm, ys_hbm, g_ref, beta_ref, o_ref, pos_s, ybuf, sem_idx, sem_row, *, alpha):
    i = pl.program_id(0)
    n = pl.num_programs(0)
    tm = h_ref.shape[0]
    slot = i % 2

    def pos_copy(t, sl):
        return pltpu.make_async_copy(pos_hbm.at[t], pos_s.at[pl.ds(sl * (tm * TOP_K), tm * TOP_K)], sem_idx.at[sl])

    def gather(sl):
        base = sl * (tm * TOP_K)

        def body(r, _):
            for s in range(TOP_K):
                pltpu.make_async_copy(ys_hbm.at[pos_s[base + r * TOP_K + s]], ybuf.at[sl, s, r],
                                      sem_row.at[sl]).start(priority=s % 2)
            return 0

        lax.fori_loop(0, tm, body, 0, unroll=8)

    @pl.when(i == 0)
    def _():
        first = pos_copy(0, 0)
        first.start()
        first.wait()
        gather(0)

        @pl.when(n > 1)
        def _():
            pos_copy(1, 1).start()

    @pl.when(i + 1 < n)
    def _():
        pos_copy(i + 1, 1 - slot).wait()
        gather(1 - slot)

        @pl.when(i + 2 < n)
        def _():
            pos_copy(i + 2, slot).start()

    for s in range(TOP_K):
        pltpu.make_async_copy(ys_hbm.at[pl.ds(0, tm)], ybuf.at[slot, s], sem_row.at[slot]).wait()

    z = alpha * h_ref[...]
    for s in range(TOP_K):
        z = z + gate_ref[:, s:s + 1, :] * ybuf[slot, s]
    inv_d = 1.0 / (z.shape[1] * z.shape[2])
    mu = jnp.sum(z, axis=(1, 2), keepdims=True) * inv_d
    dz = z - mu
    var = jnp.sum(dz * dz, axis=(1, 2), keepdims=True) * inv_d
    o_ref[...] = dz * lax.rsqrt(var + LN_EPS) * g_ref[...] + beta_ref[...]


def _combine(h3, gates, pos, ys, g, beta, alpha):
    n, nt8, _ = h3.shape
    tm = ROW_TM
    tile = lambda i: (i, 0, 0)
    const = lambda i: (0, 0, 0)
    return pl.pallas_call(
        functools.partial(_combine_kernel, alpha=alpha),
        grid=(n // tm,),
        in_specs=[pl.BlockSpec((tm, nt8, LANES), tile), pl.BlockSpec((tm, TOP_K, LANES), tile),
                  pl.BlockSpec(memory_space=pl.ANY), pl.BlockSpec(memory_space=pl.ANY),
                  pl.BlockSpec((1, nt8, LANES), const), pl.BlockSpec((1, nt8, LANES), const)],
        out_specs=pl.BlockSpec((tm, nt8, LANES), tile),
        out_shape=jax.ShapeDtypeStruct((n, nt8, LANES), F32),
        scratch_shapes=[pltpu.SMEM((2 * tm * TOP_K,), jnp.int32), pltpu.VMEM((2, TOP_K, tm, nt8, LANES), F32),
                        pltpu.SemaphoreType.DMA((2,)), pltpu.SemaphoreType.DMA((2,))],
        compiler_params=pltpu.CompilerParams(dimension_semantics=("arbitrary",), vmem_limit_bytes=VMEM_LIMIT),
        name="combine",
    )(h3, gates, pos.reshape(n // tm, tm * TOP_K), ys, g.reshape(1, nt8, LANES), beta.reshape(1, nt8, LANES))


def _layer(x, tabs, alpha, w_in, ik_g, ik_b, q_g, w_q_up, kv_g, w_kv_up, w_out, ln1_g, ln1_b,
           w_router, b_router, w_gate, b_gate, w_up, b_up, w_down, b_down, ln2_g, ln2_b):
    bsz, seq, d = x.shape
    w_t, w_n, wq_t, bd_uk_t, w_uv_t = _pack_weights(w_in, w_q_up, w_kv_up)
    aqt, iqt, iwt, akv, avt, ikp, qabst, ckv, latt = _proj(x, tabs, w_t, w_n, wq_t, bd_uk_t, ik_g, ik_b, q_g, kv_g)
    a_out = _dsa(aqt, iqt, iwt, akv, avt, ikp)
    b_out = _mla(qabst, ckv, latt, w_uv_t)
    n = bsz * seq
    h3, top_i, gates, rank, counts = _mix(x.reshape(n, d), a_out.reshape(n, -1), b_out.reshape(n, -1), w_out,
                                          ln1_g, ln1_b, w_router, b_router, alpha)
    pos, tile_expert, n_valid, pad_start, n_tiles = _route(top_i, rank, counts, MOE_TM)
    xs = _dispatch(h3, pos, pad_start, n_tiles * MOE_TM, MOE_TM)
    ys = _moe(xs, tile_expert, n_valid, n_tiles, w_gate, b_gate, w_up, b_up, w_down, b_down)
    return _combine(h3, gates, pos, ys, ln2_g, ln2_b, alpha).reshape(bsz, seq, d)


def kernel(x, positions, w_mix_in, idx_k_norm_g, idx_k_norm_b, mla_q_norm_g, mla_w_q_up, mla_kv_norm_g, mla_w_kv_up, w_mix_out, ln1_g, ln1_b, w_router, b_router, w_gate, b_gate, w_up, b_up, w_down, b_down, ln2_g, ln2_b):
    depth = w_mix_in.shape[0]
    alpha = float((2 * depth) ** 0.25)
    tabs = _rope_tables(positions)
    for l in range(depth):
        x = _layer(x, tabs, alpha, w_mix_in[l], idx_k_norm_g[l], idx_k_norm_b[l], mla_q_norm_g[l], mla_w_q_up[l],
                   mla_kv_norm_g[l], mla_w_kv_up[l], w_mix_out[l], ln1_g[l], ln1_b[l], w_router[l], b_router[l],
                   w_gate[l], b_gate[l], w_up[l], b_up[l], w_down[l], b_down[l], ln2_g[l], ln2_b[l])
    return x
```

```python
import functools

import jax
import jax.numpy as jnp
import numpy as np
from jax import lax
from jax.experimental import pallas as pl
from jax.experimental.pallas import tpu as pltpu

F32 = jnp.float32
BF16 = jnp.bfloat16

A_HEADS, A_HEAD_DIM = 8, 64
IDX_HEADS, IDX_DIM = 8, 64
TOPK_MAX = 256
B_HEADS, B_NOPE, B_ROPE, B_V = 8, 64, 32, 64
Q_LORA, KV_LORA = 256, 128
N_EXPERTS, TOP_K = 32, 4
SWIGLU_LIMIT, SWIGLU_ALPHA = 7.0, 1.702
ROPE_THETA = 10000.0
LN_EPS, RMS_EPS = 1e-5, 1e-6

LANES = 128
SUBLANES = 8
VMEM_LIMIT = 56 * 1024 * 1024

PROJ_TM = 512
ATT_TQ = LANES
ATT_TK = 512
SCORE_TK = 1024
SEL_TK = 512
ONES_ROWS = 16
LOG2_E = 1.4426950408889634
MIX_TM = 512
MOE_TM = 256
ROW_TM = 256
SEL_MAX_ITERS = 400
SEL_UNROLL = 4

MASKED = -1e30
M_FLOOR = -1e29
TINY = float(np.finfo(np.float32).tiny)


def _dot(a, b):
    return jnp.dot(a, b, preferred_element_type=F32)


def _dot_t(a, b):
    return lax.dot_general(a, b, (((1,), (1,)), ((), ())), preferred_element_type=F32)


def _lane_iota(shape):
    return lax.broadcasted_iota(jnp.int32, shape, len(shape) - 1)


def _fold(v, op, rows=64):
    n = v.shape[0] // rows
    acc = v[0:rows]
    for i in range(1, n):
        acc = op(acc, v[i * rows:(i + 1) * rows])
    return acc


def _rows_reduce(v, op):
    if op == "sum":
        return jnp.sum(v, axis=0, keepdims=True)
    if op == "max":
        return jnp.max(v, axis=0, keepdims=True)
    return jnp.min(v, axis=0, keepdims=True)


_R_AQ, _R_AQR, _R_IQ, _R_IQR, _R_V, _R_IW, _R_QC, _R_KVC, _R_TOTAL = 0, 512, 1024, 1536, 2048, 2112, 2128, 2384, 2512
_C_KV, _C_KVR, _C_IK, _C_KVC, _C_KR, _C_KRR, _C_TOTAL = 0, 128, 256, 384, 512, 640, 768


def _rot_cols(w, dim):
    k, n = w.shape
    w3 = w.reshape(k, n // dim, dim)
    return jnp.concatenate([-w3[..., dim // 2:], w3[..., :dim // 2]], axis=-1).reshape(k, n)


def _pad_cols(w, width):
    return jnp.pad(w, ((0, 0), (0, width - w.shape[1])))


def _pack_weights(w_in, w_q_up, w_kv_up):
    o = np.cumsum((0, 512, 64, 64, 512, 64, 8, 256, 128, 32))
    a_q, a_k, a_v, i_q, i_k, i_w, b_qc, b_kvc, b_kr = [w_in[:, o[i]:o[i + 1]] for i in range(9)]
    w_t = jnp.concatenate([a_q, _rot_cols(a_q, 64), i_q, _rot_cols(i_q, 64), a_v, _pad_cols(i_w, 16),
                           b_qc, b_kvc], axis=1).T
    assert w_t.shape[0] == _R_TOTAL
    w_n = jnp.concatenate([a_k, a_v, _rot_cols(a_k, 64), jnp.zeros_like(a_k), _pad_cols(i_k, 128), b_kvc,
                           _pad_cols(b_kr, 128), _pad_cols(_rot_cols(b_kr, 32), 128)], axis=1)
    assert w_n.shape[1] == _C_TOTAL
    wq = w_q_up.reshape(Q_LORA, B_HEADS, B_NOPE + B_ROPE)
    wq_rope = wq[..., B_NOPE:].reshape(Q_LORA, B_HEADS * B_ROPE)
    wq_t = jnp.concatenate([wq[..., :B_NOPE].reshape(Q_LORA, B_HEADS * B_NOPE), wq_rope,
                            _rot_cols(wq_rope, B_ROPE)], axis=1).T
    wkv = w_kv_up.reshape(KV_LORA, B_HEADS, B_NOPE + B_V)
    eye = jnp.eye(B_HEADS, dtype=w_kv_up.dtype)
    bd_uk_t = jnp.einsum('lhd,hg->glhd', wkv[..., :B_NOPE], eye).reshape(B_HEADS * KV_LORA, B_HEADS * B_NOPE)
    w_uv_t = jnp.transpose(wkv[..., B_NOPE:], (1, 2, 0))
    return w_t.astype(BF16), w_n.astype(BF16), wq_t.astype(BF16), bd_uk_t.astype(BF16), w_uv_t.astype(BF16)


def _rope_tables(positions):
    pos = positions.astype(F32)[..., None]
    inv64 = ROPE_THETA ** (-jnp.arange(0, 64, 2, dtype=F32) / 64)
    inv32 = ROPE_THETA ** (-jnp.arange(0, 32, 2, dtype=F32) / 32)
    c64, s64 = jnp.cos(pos * inv64), jnp.sin(pos * inv64)
    c32, s32 = jnp.cos(pos * inv32), jnp.sin(pos * inv32)
    nat = (jnp.tile(c64, 4), jnp.tile(s64, 4), jnp.tile(c32, 8), jnp.tile(s32, 8))
    tr = tuple(jnp.swapaxes(jnp.tile(t, 2), 1, 2) for t in (c64, s64, c32, s32))
    return nat + tr


def _proj_kernel(x_ref, ca_ref, sa_ref, cr_ref, sr_ref, cat_ref, sat_ref, crt_ref, srt_ref,
                 wt_ref, wn_ref, wqt_ref, bdukt_ref, ikg_ref, ikb_ref, qg_ref, kvg_ref, kvgt_ref,
                 aqt_ref, iqt_ref, iwt_ref, akv_ref, avt_ref, ik_ref, qabst_ref, ckv_ref, latt_ref):
    tm = x_ref.shape[0]
    nqb = tm // ATT_TQ
    xb = x_ref[...].astype(BF16)
    ca, sa, cr, sr = ca_ref[...], sa_ref[...], cr_ref[...], sr_ref[...]
    lane = _lane_iota((tm, LANES))
    low64 = lane < 64

    def proj_t(r0, rows):
        return _dot_t(wt_ref[r0:r0 + rows, :], xb)

    def proj_n(c0, width):
        return _dot(xb, wn_ref[:, c0:c0 + width])

    cat8 = jnp.concatenate([cat_ref[...]] * 8, axis=0)
    sat8 = jnp.concatenate([sat_ref[...]] * 8, axis=0)

    def split_heads(v, out_ref):
        vb = v.astype(BF16)
        zeros = jnp.zeros((64, ATT_TQ), BF16)
        for c in range(nqb):
            for h in range(8):
                out_ref[c, 0:64, h * ATT_TQ:(h + 1) * ATT_TQ] = vb[64 * h:64 * h + 64, c * ATT_TQ:(c + 1) * ATT_TQ]
                out_ref[c, 64:128, h * ATT_TQ:(h + 1) * ATT_TQ] = zeros

    aq_t = (proj_t(_R_AQ, 512) * cat8 + proj_t(_R_AQR, 512) * sat8) * (A_HEAD_DIM ** -0.5 * LOG2_E)
    split_heads(aq_t, aqt_ref)
    iq_t = proj_t(_R_IQ, 512) * cat8 + proj_t(_R_IQR, 512) * sat8
    split_heads(iq_t, iqt_ref)
    iw_t = proj_t(_R_IW, 16)[0:IDX_HEADS] * ((IDX_HEADS * IDX_DIM) ** -0.5)
    for c in range(nqb):
        iwt_ref[c] = iw_t[:, c * ATT_TQ:(c + 1) * ATT_TQ]
    ones_rows = jnp.where(lax.broadcasted_iota(jnp.int32, (ONES_ROWS, tm), 0) == 0, 1.0, 0.0).astype(BF16)
    avt_ref[...] = jnp.concatenate([proj_t(_R_V, 64).astype(BF16), ones_rows], axis=0)

    kc = jnp.where(low64, ca, 1.0)
    ks = jnp.where(low64, sa, 0.0)
    akv_ref[...] = (proj_n(_C_KV, 128) * kc + proj_n(_C_KVR, 128) * ks).astype(BF16)

    y = proj_n(_C_IK, 128)
    mu = jnp.sum(y, axis=1, keepdims=True) * (1.0 / IDX_DIM)
    d = jnp.where(low64, y - mu, 0.0)
    var = jnp.sum(d * d, axis=1, keepdims=True) * (1.0 / IDX_DIM)
    yn = d * lax.rsqrt(var + LN_EPS) * ikg_ref[...] + ikb_ref[...]
    rot = jnp.where(lane < 32, -pltpu.roll(yn, LANES - 32, 1), pltpu.roll(yn, 32, 1))
    ik_ref[...] = jnp.where(low64, yn * ca + rot * sa, 0.0).astype(BF16)

    qc_t = proj_t(_R_QC, Q_LORA)
    qn_t = (qc_t * lax.rsqrt(jnp.mean(qc_t * qc_t, axis=0, keepdims=True) + RMS_EPS) * qg_ref[...]).astype(BF16)
    q_t = _dot(wqt_ref[...], qn_t)
    scale = (B_NOPE + B_ROPE) ** -0.5 * LOG2_E
    q_lat_t = (_dot(bdukt_ref[...], q_t[0:512].astype(BF16)) * scale).astype(BF16)
    crt8 = jnp.concatenate([crt_ref[...]] * 8, axis=0)
    srt8 = jnp.concatenate([srt_ref[...]] * 8, axis=0)
    q_r_t = ((q_t[512:768] * crt8 + q_t[768:1024] * srt8) * scale).astype(BF16)
    zeros = jnp.zeros((2 * LANES - KV_LORA - B_ROPE, ATT_TQ), BF16)
    for c in range(nqb):
        tok = slice(c * ATT_TQ, (c + 1) * ATT_TQ)
        for h in range(B_HEADS):
            col = slice(h * ATT_TQ, (h + 1) * ATT_TQ)
            qabst_ref[c, 0:KV_LORA, col] = q_lat_t[h * KV_LORA:(h + 1) * KV_LORA, tok]
            qabst_ref[c, KV_LORA:KV_LORA + B_ROPE, col] = q_r_t[h * B_ROPE:(h + 1) * B_ROPE, tok]
            qabst_ref[c, KV_LORA + B_ROPE:2 * LANES, col] = zeros

    kvc = proj_n(_C_KVC, KV_LORA)
    kvn = kvc * lax.rsqrt(jnp.mean(kvc * kvc, axis=1, keepdims=True) + RMS_EPS) * kvg_ref[...]
    kr = proj_n(_C_KR, 128) * cr + proj_n(_C_KRR, 128) * sr
    ckv_ref[...] = jnp.concatenate([kvn, kr], axis=1).astype(BF16)
    kvc_t = proj_t(_R_KVC, KV_LORA)
    lat_t = (kvc_t * lax.rsqrt(jnp.mean(kvc_t * kvc_t, axis=0, keepdims=True) + RMS_EPS) * kvgt_ref[...]).astype(BF16)
    latt_ref[...] = jnp.concatenate([lat_t, ones_rows], axis=0)


def _proj(x, tabs, w_t, w_n, wq_t, bd_uk_t, ik_g, ik_b, q_g, kv_g):
    bsz, seq, d = x.shape
    tm = PROJ_TM
    nqb = tm // ATT_TQ
    tok = lambda b, i: (b, i, 0)
    feat = lambda b, i: (b, 0, i)
    qblk = lambda b, i: (b, i, 0, 0)
    const2 = lambda b, i: (0, 0)
    nat_spec = pl.BlockSpec((None, tm, LANES), tok)
    col = lambda v: jnp.broadcast_to(v.reshape(-1, 1), (v.shape[0], tm))
    ik_g = jnp.pad(ik_g, (0, LANES - IDX_DIM)).reshape(1, LANES)
    ik_b = jnp.pad(ik_b, (0, LANES - IDX_DIM)).reshape(1, LANES)
    consts = (w_t, w_n, wq_t, bd_uk_t, ik_g, ik_b, col(q_g), kv_g.reshape(1, KV_LORA), col(kv_g))
    return pl.pallas_call(
        _proj_kernel,
        grid=(bsz, seq // tm),
        in_specs=[pl.BlockSpec((None, tm, d), tok), nat_spec, nat_spec, nat_spec, nat_spec,
                  pl.BlockSpec((None, 64, tm), feat), pl.BlockSpec((None, 64, tm), feat),
                  pl.BlockSpec((None, 32, tm), feat), pl.BlockSpec((None, 32, tm), feat)]
                 + [pl.BlockSpec(c.shape, const2) for c in consts],
        out_specs=[pl.BlockSpec((None, nqb, LANES, A_HEADS * ATT_TQ), qblk),
                   pl.BlockSpec((None, nqb, LANES, IDX_HEADS * ATT_TQ), qblk),
                   pl.BlockSpec((None, nqb, IDX_HEADS, ATT_TQ), qblk),
                   pl.BlockSpec((None, tm, LANES), tok),
                   pl.BlockSpec((None, A_HEAD_DIM + ONES_ROWS, tm), feat),
                   pl.BlockSpec((None, tm, LANES), tok),
                   pl.BlockSpec((None, nqb, 2 * LANES, B_HEADS * ATT_TQ), qblk),
                   pl.BlockSpec((None, tm, 2 * LANES), tok),
                   pl.BlockSpec((None, KV_LORA + ONES_ROWS, tm), feat)],
        out_shape=[jax.ShapeDtypeStruct((bsz, seq // ATT_TQ, LANES, A_HEADS * ATT_TQ), BF16),
                   jax.ShapeDtypeStruct((bsz, seq // ATT_TQ, LANES, IDX_HEADS * ATT_TQ), BF16),
                   jax.ShapeDtypeStruct((bsz, seq // ATT_TQ, IDX_HEADS, ATT_TQ), F32),
                   jax.ShapeDtypeStruct((bsz, seq, LANES), BF16),
                   jax.ShapeDtypeStruct((bsz, A_HEAD_DIM + ONES_ROWS, seq), BF16),
                   jax.ShapeDtypeStruct((bsz, seq, LANES), BF16),
                   jax.ShapeDtypeStruct((bsz, seq // ATT_TQ, 2 * LANES, B_HEADS * ATT_TQ), BF16),
                   jax.ShapeDtypeStruct((bsz, seq, 2 * LANES), BF16),
                   jax.ShapeDtypeStruct((bsz, KV_LORA + ONES_ROWS, seq), BF16)],
        compiler_params=pltpu.CompilerParams(
            dimension_semantics=("arbitrary", "arbitrary"), vmem_limit_bytes=VMEM_LIMIT),
        name="proj",
    )(x, *tabs, *consts)


def _attend_init(nh, val_rows):
    return ([jnp.full((1, LANES), M_FLOOR, F32) for _ in range(nh)],
            jnp.zeros((val_rows + ONES_ROWS, nh * LANES), F32))


def _attend_update(lg_ref, bias, vt_blk, state):
    ms, acc = state
    new_ms, ps, alphas = [], [], []
    for h in range(len(ms)):
        lh = lg_ref[:, h * LANES:(h + 1) * LANES]
        if bias is not None:
            lh = lh + bias
        m_new = jnp.maximum(ms[h], _rows_reduce(_fold(lh, jnp.maximum), "max"))
        alphas.append(jnp.exp2(ms[h] - m_new))
        ps.append(jnp.exp2(lh - m_new).astype(BF16))
        new_ms.append(m_new)
    return new_ms, jnp.concatenate(alphas, axis=1) * acc + _dot(vt_blk, jnp.concatenate(ps, axis=1))


def _attend_pipelined(n_blocks, scores, update, lg_a, lg_b, state):
    n_look = (n_blocks - 1) // 2

    def pair(i, st, lookahead):
        kb0 = 2 * i
        lg_b[...] = scores(kb0 + 1)
        st = update(kb0, lg_a, not lookahead, st)
        if lookahead:
            lg_a[...] = scores(kb0 + 2)
        return update(kb0 + 1, lg_b, not lookahead, st)

    lg_a[...] = scores(0)
    state = lax.fori_loop(0, n_look, lambda i, st: pair(i, st, True), state)
    return lax.cond(n_blocks % 2 == 0,
                    lambda st: pair(n_look, st, False),
                    lambda st: update(2 * n_look, lg_a, True, st), state)


def _attend_finish(state, val_rows):
    acc = state[1]
    return acc[0:val_rows] / acc[val_rows:val_rows + 1]


def _dsa_kernel(aqt_ref, iqt_ref, iwt_ref, akv_ref, avt_ref, ik_ref, out_ref, s_sc, lg_a, lg_b):
    nh = iwt_ref.shape[0]
    tq, tk = ATT_TQ, ATT_TK
    seq = akv_ref.shape[0]
    j = pl.program_id(1)
    nkb = (j * tq) // tk + 1
    iqt = iqt_ref[...]
    w = iwt_ref[...]

    ts = SCORE_TK
    krow = lax.broadcasted_iota(jnp.int32, (ts, tq), 0)
    qcol = j * tq + lax.broadcasted_iota(jnp.int32, (ts, tq), 1)

    def score_body(kb, carry):
        cmax, cmin = carry
        k0 = pl.multiple_of(kb * ts, ts)
        y = _dot(ik_ref[pl.ds(k0, ts), :], iqt)
        s = jnp.zeros((ts, tq), F32)
        for h in range(nh):
            s = s + w[h:h + 1, :] * jnp.maximum(y[:, h * tq:(h + 1) * tq], 0.0)
        causal = (k0 + krow) <= qcol
        s_lo = jnp.where(causal, s, -jnp.inf)
        s_sc[pl.ds(k0, ts), :] = s_lo
        cmax = jnp.maximum(cmax, _fold(s_lo, jnp.maximum))
        cmin = jnp.minimum(cmin, _fold(jnp.where(causal, s, jnp.inf), jnp.minimum))
        return cmax, cmin

    cmax, cmin = lax.fori_loop(0, (j * tq) // ts + 1, score_body,
                               (jnp.full((64, tq), -jnp.inf, F32), jnp.full((64, tq), jnp.inf, F32)))
    rmax = _rows_reduce(cmax, "max")
    rmin = _rows_reduce(cmin, "min")

    n_causal = (j * tq + lax.broadcasted_iota(jnp.int32, (1, tq), 1) + 1).astype(F32)
    kt = jnp.minimum(n_causal, float(TOPK_MAX))

    tc = SEL_TK
    ncb = (j * tq) // tc + 1
    crow_f = lax.broadcasted_iota(jnp.int32, (tc, tq), 0).astype(F32)

    def count(pred):
        def body(cb, acc):
            k0 = pl.multiple_of(cb * tc, tc)
            return acc + _fold(jnp.where(pred(s_sc[pl.ds(k0, tc), :], k0), 1.0, 0.0), jnp.add)
        return _rows_reduce(lax.fori_loop(0, ncb, body, jnp.zeros((64, tq), F32)), "sum")

    def propose(it, lo, hi, clo):
        mid = 0.5 * lo + 0.5 * hi
        mid = jnp.where(jnp.logical_and(it == 0, jnp.logical_and(lo < 0.0, hi > 0.0)), 0.0, mid)
        mid = jnp.where(jnp.logical_and(it == 1, jnp.logical_and(lo == 0.0, hi > TINY)), TINY, mid)
        active = jnp.logical_and(clo != kt, jnp.logical_and(mid > lo, mid < hi))
        return mid, active

    def sel_step(it, lo, hi, clo, chi):
        mid, active = propose(it, lo, hi, clo)
        cm = count(lambda blk, k0: blk >= mid)
        ge = cm >= kt
        up = jnp.logical_and(active, ge)
        dn = jnp.logical_and(active, jnp.logical_not(ge))
        return jnp.where(up, mid, lo), jnp.where(dn, mid, hi), jnp.where(up, cm, clo), jnp.where(dn, cm, chi)

    def sel_body(c):
        it, _, lo, hi, clo, chi = c
        for u in range(SEL_UNROLL):
            lo, hi, clo, chi = sel_step(it + u, lo, hi, clo, chi)
        _, active = propose(it + SEL_UNROLL, lo, hi, clo)
        go = jnp.logical_and(jnp.max(jnp.where(active, 1.0, 0.0)) > 0.0, it + SEL_UNROLL < SEL_MAX_ITERS)
        return it + SEL_UNROLL, go, lo, hi, clo, chi

    lo0 = rmin
    hi0 = rmax + jnp.abs(rmax) + 1e-30
    clo0, chi0 = n_causal, jnp.zeros((1, tq), F32)
    _, active0 = propose(0, lo0, hi0, clo0)
    go0 = jnp.max(jnp.where(active0, 1.0, 0.0)) > 0.0
    _, _, lo, hi, clo, chi = lax.while_loop(lambda c: c[1], sel_body, (jnp.int32(0), go0, lo0, hi0, clo0, chi0))

    tie = clo > kt

    @pl.when(jnp.max(jnp.where(tie, 1.0, 0.0)) > 0.0)
    def _():
        need = kt - chi

        def idx_body(_, c):
            jl, jh = c
            jm = jnp.floor((jl + jh) * 0.5)
            ok = count(lambda blk, k0: jnp.logical_and(blk == lo, crow_f + k0.astype(F32) <= jm)) >= need
            return jnp.where(ok, jl, jm), jnp.where(ok, jm, jh)

        n_steps = int(np.ceil(np.log2(seq))) + 1
        _, jh = lax.fori_loop(0, n_steps, idx_body,
                              (jnp.full((1, tq), -1.0, F32), jnp.full((1, tq), float(seq - 1), F32)))
        jh = jnp.where(tie, jh, float(seq))

        def fix_body(cb, _):
            k0 = pl.multiple_of(cb * tc, tc)
            blk = s_sc[pl.ds(k0, tc), :]
            drop = jnp.logical_and(blk == lo, crow_f + k0.astype(F32) > jh)
            s_sc[pl.ds(k0, tc), :] = jnp.where(drop, -jnp.inf, blk)
            return 0

        lax.fori_loop(0, ncb, fix_body, 0)

    def att_scores(kb):
        return _dot(akv_ref[pl.ds(pl.multiple_of(kb * tk, tk), tk), :], aqt_ref[...])

    def att_update(kb, lg_ref, tail, state):
        k0 = pl.multiple_of(kb * tk, tk)
        bias = jnp.where(s_sc[pl.ds(k0, tk), :] >= lo, 0.0, MASKED)
        return _attend_update(lg_ref, bias, avt_ref[:, pl.ds(k0, tk)], state)

    state = _attend_pipelined(nkb, att_scores, att_update, lg_a, lg_b, _attend_init(nh, A_HEAD_DIM))
    o_t = _attend_finish(state, A_HEAD_DIM)
    for p in range(nh // 2):
        pair = jnp.concatenate([o_t[:, (2 * p) * tq:(2 * p + 1) * tq], o_t[:, (2 * p + 1) * tq:(2 * p + 2) * tq]], axis=0)
        out_ref[:, p * LANES:(p + 1) * LANES] = pair.T.astype(out_ref.dtype)


def _dsa(aqt, iqt, iwt, akv, avt, ikp):
    bsz, nqb, _, hq = aqt.shape
    nh = iwt.shape[2]
    seq = akv.shape[1]
    tq = ATT_TQ
    qblk = lambda b, j: (b, j, 0, 0)
    full = lambda b, j: (b, 0, 0)
    return pl.pallas_call(
        _dsa_kernel,
        grid=(bsz, nqb),
        in_specs=[pl.BlockSpec((None, None, LANES, hq), qblk), pl.BlockSpec((None, None, LANES, hq), qblk),
                  pl.BlockSpec((None, None, nh, tq), qblk),
                  pl.BlockSpec((None, seq, LANES), full), pl.BlockSpec((None, A_HEAD_DIM + ONES_ROWS, seq), full),
                  pl.BlockSpec((None, seq, LANES), full)],
        out_specs=pl.BlockSpec((None, tq, nh * A_HEAD_DIM), lambda b, j: (b, j, 0)),
        out_shape=jax.ShapeDtypeStruct((bsz, seq, nh * A_HEAD_DIM), BF16),
        scratch_shapes=[pltpu.VMEM((seq, tq), F32), pltpu.VMEM((ATT_TK, hq), F32), pltpu.VMEM((ATT_TK, hq), F32)],
        compiler_params=pltpu.CompilerParams(
            dimension_semantics=("arbitrary", "arbitrary"), vmem_limit_bytes=VMEM_LIMIT),
        name="dsa",
    )(aqt, iqt, iwt, akv, avt, ikp)


def _mla_kernel(qt_ref, ckv_ref, latt_ref, wuvt_ref, out_ref, lg_a, lg_b):
    nh = wuvt_ref.shape[0]
    tq, tk = ATT_TQ, ATT_TK
    j = pl.program_id(1)
    krow = lax.broadcasted_iota(jnp.int32, (tk, tq), 0)
    qcol = j * tq + lax.broadcasted_iota(jnp.int32, (1, tq), 1)

    def scores(kb):
        return _dot(ckv_ref[pl.ds(pl.multiple_of(kb * tk, tk), tk), :], qt_ref[...])

    def update(kb, lg_ref, tail, state):
        k0 = pl.multiple_of(kb * tk, tk)
        bias = None
        if tail:
            bias = jnp.where((k0 + krow) <= qcol, 0.0, MASKED)
        return _attend_update(lg_ref, bias, latt_ref[:, pl.ds(k0, tk)], state)

    state = _attend_pipelined((j * tq) // tk + 1, scores, update, lg_a, lg_b, _attend_init(nh, KV_LORA))
    o_lat_t = _attend_finish(state, KV_LORA).astype(BF16)
    for p in range(nh // 2):
        pair = jnp.concatenate([_dot(wuvt_ref[h], o_lat_t[:, h * tq:(h + 1) * tq]) for h in (2 * p, 2 * p + 1)], axis=0)
        out_ref[:, p * LANES:(p + 1) * LANES] = pair.T.astype(out_ref.dtype)


def _mla(qabst, ckv, latt, w_uv_t):
    bsz, nqb, dq, hq = qabst.shape
    nh = w_uv_t.shape[0]
    seq = ckv.shape[1]
    tq = ATT_TQ
    return pl.pallas_call(
        _mla_kernel,
        grid=(bsz, nqb),
        in_specs=[pl.BlockSpec((None, None, dq, hq), lambda b, j: (b, j, 0, 0)),
                  pl.BlockSpec((None, seq, dq), lambda b, j: (b, 0, 0)),
                  pl.BlockSpec((None, KV_LORA + ONES_ROWS, seq), lambda b, j: (b, 0, 0)),
                  pl.BlockSpec(w_uv_t.shape, lambda b, j: (0, 0, 0))],
        out_specs=pl.BlockSpec((None, tq, nh * B_V), lambda b, j: (b, j, 0)),
        out_shape=jax.ShapeDtypeStruct((bsz, seq, nh * B_V), BF16),
        scratch_shapes=[pltpu.VMEM((ATT_TK, hq), F32), pltpu.VMEM((ATT_TK, hq), F32)],
        compiler_params=pltpu.CompilerParams(
            dimension_semantics=("arbitrary", "arbitrary"), vmem_limit_bytes=VMEM_LIMIT),
        name="mla",
    )(qabst, ckv, latt, w_uv_t)


def _layer_norm(z, g, b):
    mu = jnp.mean(z, axis=1, keepdims=True)
    d = z - mu
    var = jnp.mean(d * d, axis=1, keepdims=True)
    return d * lax.rsqrt(var + LN_EPS) * g + b


def _split_bf16(v):
    hi = v.astype(BF16)
    return hi, (v - hi.astype(F32)).astype(BF16)


def _to_row_tiles(v, ref):
    for c in range(v.shape[1] // LANES):
        ref[:, c, :] = v[:, c * LANES:(c + 1) * LANES]


def _from_row_tiles(ref):
    return jnp.concatenate([ref[:, c, :] for c in range(ref.shape[1])], axis=1)


def _mix_kernel(x_ref, a_ref, b_ref, wout_ref, g_ref, beta_ref, wr_ref, br_ref,
                h_ref, ti_ref, tg_ref, rank_ref, cnt_ref, carry_sc, *, alpha):
    tm = x_ref.shape[0]
    half = a_ref.shape[1]
    i = pl.program_id(0)

    @pl.when(i == 0)
    def _():
        carry_sc[...] = jnp.zeros(carry_sc.shape, F32)

    mix = _dot(a_ref[...], wout_ref[0:half, :]) + _dot(b_ref[...], wout_ref[half:2 * half, :])
    h = _layer_norm(alpha * x_ref[...] + mix, g_ref[...], beta_ref[...])
    _to_row_tiles(h, h_ref)
    h_hi, h_lo = _split_bf16(h)
    w_hi, w_lo = _split_bf16(wr_ref[...])
    logits = _dot(h_hi, w_hi) + (_dot(h_lo, w_hi) + _dot(h_hi, w_lo)) + br_ref[...]
    lane = _lane_iota((tm, LANES))
    logits = jnp.where(lane < N_EXPERTS, logits, -jnp.inf)
    ids = jnp.zeros((tm, LANES), jnp.int32)
    vals, hots = [], []
    for k in range(TOP_K):
        m = jnp.max(logits, axis=1, keepdims=True)
        idx = jnp.min(jnp.where(logits == m, lane, LANES), axis=1, keepdims=True)
        hot = lane == idx
        ids = jnp.where(lane == k, idx, ids)
        logits = jnp.where(hot, -jnp.inf, logits)
        vals.append(m)
        hots.append(hot)
    es = [jnp.exp(v - vals[0]) for v in vals]
    den = (es[0] + es[1]) + (es[2] + es[3])
    ti_ref[...] = ids[:, :TOP_K]
    for k in range(TOP_K):
        tg_ref[:, k, :] = jnp.broadcast_to(es[k] / den, (tm, LANES))

    onehot = jnp.where(jnp.logical_or(jnp.logical_or(hots[0], hots[1]), jnp.logical_or(hots[2], hots[3])), 1.0, 0.0)
    r_i = lax.broadcasted_iota(jnp.int32, (tm, tm), 0)
    c_i = lax.broadcasted_iota(jnp.int32, (tm, tm), 1)
    before = _dot(jnp.where(c_i < r_i, 1.0, 0.0).astype(BF16), onehot.astype(BF16)) + carry_sc[...]
    ranks = jnp.zeros((tm, LANES), F32)
    for k in range(TOP_K):
        ranks = jnp.where(lane == k, jnp.sum(jnp.where(hots[k], before, 0.0), axis=1, keepdims=True), ranks)
    rank_ref[...] = ranks[:, :TOP_K].astype(jnp.int32)
    carry_sc[...] = carry_sc[...] + jnp.sum(onehot, axis=0, keepdims=True)
    cnt_ref[...] = carry_sc[...].astype(jnp.int32)


def _mix(x2, a_out, b_out, w_out, g, beta, w_router, b_router, alpha):
    n, d = x2.shape
    tm = MIX_TM
    half = a_out.shape[1]
    row = lambda i: (i, 0)
    const = lambda i: (0, 0)
    wr = jnp.pad(w_router, ((0, 0), (0, LANES - N_EXPERTS)))
    br = jnp.pad(b_router, (0, LANES - N_EXPERTS)).reshape(1, LANES)
    return pl.pallas_call(
        functools.partial(_mix_kernel, alpha=alpha),
        grid=(n // tm,),
        in_specs=[pl.BlockSpec((tm, d), row), pl.BlockSpec((tm, half), row), pl.BlockSpec((tm, half), row),
                  pl.BlockSpec(w_out.shape, const), pl.BlockSpec((1, d), const), pl.BlockSpec((1, d), const),
                  pl.BlockSpec(wr.shape, const), pl.BlockSpec((1, LANES), const)],
        out_specs=[pl.BlockSpec((tm, d // LANES, LANES), lambda i: (i, 0, 0)), pl.BlockSpec((tm, TOP_K), row),
                   pl.BlockSpec((tm, TOP_K, LANES), lambda i: (i, 0, 0)), pl.BlockSpec((tm, TOP_K), row),
                   pl.BlockSpec((1, LANES), const)],
        out_shape=[jax.ShapeDtypeStruct((n, d // LANES, LANES), F32), jax.ShapeDtypeStruct((n, TOP_K), jnp.int32),
                   jax.ShapeDtypeStruct((n, TOP_K, LANES), F32), jax.ShapeDtypeStruct((n, TOP_K), jnp.int32),
                   jax.ShapeDtypeStruct((1, LANES), jnp.int32)],
        scratch_shapes=[pltpu.VMEM((1, LANES), F32)],
        compiler_params=pltpu.CompilerParams(dimension_semantics=("arbitrary",), vmem_limit_bytes=VMEM_LIMIT),
        name="mix",
    )(x2, a_out, b_out, w_out.astype(BF16), g.reshape(1, d), beta.reshape(1, d), wr, br)


def _route(top_i, rank, counts, tm):
    n = top_i.shape[0]
    n_tiles = (n * TOP_K) // tm + N_EXPERTS
    counts = counts[0, :N_EXPERTS]
    tiles_per = (counts + tm - 1) // tm
    tile_end = jnp.cumsum(tiles_per)
    pstart = (tile_end - tiles_per) * tm
    pos = rank + jnp.sum(jnp.where(top_i[..., None] == jnp.arange(N_EXPERTS, dtype=jnp.int32), pstart, 0), axis=-1)
    tile_ids = jnp.arange(n_tiles, dtype=jnp.int32)
    n_valid = tile_end[-1]
    tile_expert = jnp.sum(tile_end[None, :] <= jnp.minimum(tile_ids, n_valid - 1)[:, None], axis=1).astype(jnp.int32)
    n_valid = n_valid[None].astype(jnp.int32)
    pad_start = jnp.concatenate([pstart + counts, n_valid]).astype(jnp.int32)
    return pos.astype(jnp.int32), tile_expert, n_valid, pad_start, n_tiles


def _dispatch_kernel(pad_ref, h_ref, pos_hbm, xs_hbm, pos_s, zbuf, sem_idx, sem_row):
    i = pl.program_id(0)
    tm = h_ref.shape[0]

    def pos_copy(t, slot):
        return pltpu.make_async_copy(pos_hbm.at[t], pos_s.at[pl.ds(slot * (tm * TOP_K), tm * TOP_K)], sem_idx.at[slot])

    @pl.when(i == 0)
    def _():
        zt = zbuf.shape[0]
        zbuf[...] = jnp.zeros(zbuf.shape, zbuf.dtype)
        for e in range(N_EXPERTS):
            cpe = pltpu.make_async_copy(zbuf, xs_hbm.at[pl.ds(pad_ref[e], zt)], sem_row)
            cpe.start()
            cpe.wait()

        def fill(t, _):
            cpz = pltpu.make_async_copy(zbuf, xs_hbm.at[pl.ds(pl.multiple_of(t * zt, zt), zt)], sem_row)
            cpz.start()
            cpz.wait()
            return 0

        lax.fori_loop(pad_ref[N_EXPERTS], xs_hbm.shape[0] // zt, fill, 0)
        pos_copy(0, 0).start()

    slot = i % 2
    pos_copy(i, slot).wait()

    @pl.when(i + 1 < pl.num_programs(0))
    def _():
        pos_copy(i + 1, 1 - slot).start()

    base = slot * (tm * TOP_K)

    def body(r, _):
        for s in range(TOP_K):
            pltpu.make_async_copy(h_ref.at[r], xs_hbm.at[pos_s[base + r * TOP_K + s]], sem_row).start(priority=s % 2)
        return 0

    lax.fori_loop(0, tm, body, 0, unroll=8)
    for s in range(TOP_K):
        pltpu.make_async_copy(h_ref, xs_hbm.at[pl.ds(0, tm)], sem_row).wait()


def _dispatch(h3, pos, pad_start, n_rows, zrows):
    n, nt8, _ = h3.shape
    tm = ROW_TM
    return pl.pallas_call(
        _dispatch_kernel,
        grid_spec=pltpu.PrefetchScalarGridSpec(
            num_scalar_prefetch=1,
            grid=(n // tm,),
            in_specs=[pl.BlockSpec((tm, nt8, LANES), lambda i, pad: (i, 0, 0)), pl.BlockSpec(memory_space=pl.ANY)],
            out_specs=pl.BlockSpec(memory_space=pl.ANY),
            scratch_shapes=[pltpu.SMEM((2 * tm * TOP_K,), jnp.int32), pltpu.VMEM((zrows, nt8, LANES), F32),
                            pltpu.SemaphoreType.DMA((2,)), pltpu.SemaphoreType.DMA]),
        out_shape=jax.ShapeDtypeStruct((n_rows + zrows, nt8, LANES), F32),
        compiler_params=pltpu.CompilerParams(dimension_semantics=("arbitrary",), vmem_limit_bytes=VMEM_LIMIT),
        name="dispatch",
    )(pad_start, h3, pos.reshape(n // tm, tm * TOP_K))


def _moe_kernel(te_ref, nv_ref, xs_hbm, wg_ref, bg_ref, wu_ref, bu_ref, wd_ref, bd_ref, ys_hbm,
                xbuf, ybuf, wg_bf, wu_bf, wd_bf, sem_in, sem_out):
    i = pl.program_id(0)
    n_valid = nv_ref[0]
    tm = ybuf.shape[0]
    nt8 = xs_hbm.shape[1]

    def in_copy(t, slot, c):
        return pltpu.make_async_copy(xs_hbm.at[pl.ds(t * tm, tm), c, :], xbuf.at[slot, :, pl.ds(c * LANES, LANES)],
                                     sem_in.at[slot])

    def out_copy(t, c):
        return pltpu.make_async_copy(ybuf.at[:, pl.ds(c * LANES, LANES)], ys_hbm.at[pl.ds(t * tm, tm), c, :], sem_out)

    def fetch(t, slot):
        for c in range(nt8):
            in_copy(t, slot, c).start()

    def store(t):
        for c in range(nt8):
            out_copy(t, c).start()

    def wait_store(t):
        for c in range(nt8):
            out_copy(t, c).wait()

    @pl.when(i == 0)
    def _():
        fetch(0, 0)

    @pl.when(i < n_valid)
    def _():
        slot = i % 2
        for c in range(nt8):
            in_copy(i, slot, c).wait()

        @pl.when(i + 1 < n_valid)
        def _():
            fetch(i + 1, 1 - slot)

        @pl.when(jnp.logical_or(i == 0, te_ref[i] != te_ref[jnp.maximum(i - 1, 0)]))
        def _():
            wg_bf[...] = wg_ref[0].astype(BF16)
            wu_bf[...] = wu_ref[0].astype(BF16)
            wd_bf[...] = wd_ref[0].astype(BF16)

        x = xbuf[slot].astype(BF16)
        g = jnp.minimum(_dot(x, wg_bf[...]) + bg_ref[0], SWIGLU_LIMIT)
        u = jnp.clip(_dot(x, wu_bf[...]) + bu_ref[0], -SWIGLU_LIMIT, SWIGLU_LIMIT)
        act = g * (1.0 / (1.0 + jnp.exp(-SWIGLU_ALPHA * g))) * (u + 1.0)
        y = _dot(act.astype(BF16), wd_bf[...]) + bd_ref[0]

        @pl.when(i > 0)
        def _():
            wait_store(i - 1)

        ybuf[...] = y
        store(i)

        @pl.when(i == n_valid - 1)
        def _():
            wait_store(i)

    @pl.when(i >= n_valid)
    def _():
        ybuf[...] = jnp.zeros(ybuf.shape, ybuf.dtype)
        store(i)
        wait_store(i)


def _moe(xs, tile_expert, n_valid, n_tiles, w_gate, b_gate, w_up, b_up, w_down, b_down):
    _, nt8, _ = xs.shape
    tm = MOE_TM
    ne, d, dff = w_gate.shape
    wmap = lambda i, te, nv: (te[i], 0, 0)
    any_spec = pl.BlockSpec(memory_space=pl.ANY)
    grid_spec = pltpu.PrefetchScalarGridSpec(
        num_scalar_prefetch=2,
        grid=(n_tiles,),
        in_specs=[any_spec,
                  pl.BlockSpec((1, d, dff), wmap), pl.BlockSpec((1, 1, dff), wmap),
                  pl.BlockSpec((1, d, dff), wmap), pl.BlockSpec((1, 1, dff), wmap),
                  pl.BlockSpec((1, dff, d), wmap), pl.BlockSpec((1, 1, d), wmap)],
        out_specs=any_spec,
        scratch_shapes=[pltpu.VMEM((2, tm, d), F32), pltpu.VMEM((tm, d), F32),
                        pltpu.VMEM((d, dff), BF16), pltpu.VMEM((d, dff), BF16), pltpu.VMEM((dff, d), BF16),
                        pltpu.SemaphoreType.DMA((2,)), pltpu.SemaphoreType.DMA])
    return pl.pallas_call(
        _moe_kernel,
        grid_spec=grid_spec,
        out_shape=jax.ShapeDtypeStruct((n_tiles * tm, nt8, LANES), F32),
        compiler_params=pltpu.CompilerParams(dimension_semantics=("arbitrary",), vmem_limit_bytes=VMEM_LIMIT),
        name="moe",
    )(tile_expert, n_valid, xs,
      w_gate, b_gate.reshape(ne, 1, dff), w_up, b_up.reshape(ne, 1, dff), w_down, b_down.reshape(ne, 1, d))


def _combine_kernel(h_ref, gate_ref, pos_hbm, ys_hbm, g_ref, beta_ref, o_ref, pos_s, ybuf, sem_idx, sem_row, *, alpha):
    i = pl.program_id(0)
    n = pl.num_programs(0)
    tm = h_ref.shape[0]
    slot = i % 2

    def pos_copy(t, sl):
        return pltpu.make_async_copy(pos_hbm.at[t], pos_s.at[pl.ds(sl * (tm * TOP_K), tm * TOP_K)], sem_idx.at[sl])

    def gather(sl):
        base = sl * (tm * TOP_K)

        def body(r, _):
            for s in range(TOP_K):
                pltpu.make_async_copy(ys_hbm.at[pos_s[base + r * TOP_K + s]], ybuf.at[sl, s, r],
                                      sem_row.at[sl]).start(priority=s % 2)
            return 0

        lax.fori_loop(0, tm, body, 0, unroll=8)

    @pl.when(i == 0)
    def _():
        first = pos_copy(0, 0)
        first.start()
        first.wait()
        gather(0)

        @pl.when(n > 1)
        def _():
            pos_copy(1, 1).start()

    @pl.when(i + 1 < n)
    def _():
        pos_copy(i + 1, 1 - slot).wait()
        gather(1 - slot)

        @pl.when(i + 2 < n)
        def _():
            pos_copy(i + 2, slot).start()

    for s in range(TOP_K):
        pltpu.make_async_copy(ys_hbm.at[pl.ds(0, tm)], ybuf.at[slot, s], sem_row.at[slot]).wait()

    z = alpha * h_ref[...]
    for s in range(TOP_K):
        z = z + gate_ref[:, s:s + 1, :] * ybuf[slot, s]
    inv_d = 1.0 / (z.shape[1] * z.shape[2])
    mu = jnp.sum(z, axis=(1, 2), keepdims=True) * inv_d
    dz = z - mu
    var = jnp.sum(dz * dz, axis=(1, 2), keepdims=True) * inv_d
    o_ref[...] = dz * lax.rsqrt(var + LN_EPS) * g_ref[...] + beta_ref[...]


def _combine(h3, gates, pos, ys, g, beta, alpha):
    n, nt8, _ = h3.shape
    tm = ROW_TM
    tile = lambda i: (i, 0, 0)
    const = lambda i: (0, 0, 0)
    return pl.pallas_call(
        functools.partial(_combine_kernel, alpha=alpha),
        grid=(n // tm,),
        in_specs=[pl.BlockSpec((tm, nt8, LANES), tile), pl.BlockSpec((tm, TOP_K, LANES), tile),
                  pl.BlockSpec(memory_space=pl.ANY), pl.BlockSpec(memory_space=pl.ANY),
                  pl.BlockSpec((1, nt8, LANES), const), pl.BlockSpec((1, nt8, LANES), const)],
        out_specs=pl.BlockSpec((tm, nt8, LANES), tile),
        out_shape=jax.ShapeDtypeStruct((n, nt8, LANES), F32),
        scratch_shapes=[pltpu.SMEM((2 * tm * TOP_K,), jnp.int32), pltpu.VMEM((2, TOP_K, tm, nt8, LANES), F32),
                        pltpu.SemaphoreType.DMA((2,)), pltpu.SemaphoreType.DMA((2,))],
        compiler_params=pltpu.CompilerParams(dimension_semantics=("arbitrary",), vmem_limit_bytes=VMEM_LIMIT),
        name="combine",
    )(h3, gates, pos.reshape(n // tm, tm * TOP_K), ys, g.reshape(1, nt8, LANES), beta.reshape(1, nt8, LANES))


def _layer(x, tabs, alpha, w_in, ik_g, ik_b, q_g, w_q_up, kv_g, w_kv_up, w_out, ln1_g, ln1_b,
           w_router, b_router, w_gate, b_gate, w_up, b_up, w_down, b_down, ln2_g, ln2_b):
    bsz, seq, d = x.shape
    w_t, w_n, wq_t, bd_uk_t, w_uv_t = _pack_weights(w_in, w_q_up, w_kv_up)
    aqt, iqt, iwt, akv, avt, ikp, qabst, ckv, latt = _proj(x, tabs, w_t, w_n, wq_t, bd_uk_t, ik_g, ik_b, q_g, kv_g)
    a_out = _dsa(aqt, iqt, iwt, akv, avt, ikp)
    b_out = _mla(qabst, ckv, latt, w_uv_t)
    n = bsz * seq
    h3, top_i, gates, rank, counts = _mix(x.reshape(n, d), a_out.reshape(n, -1), b_out.reshape(n, -1), w_out,
                                          ln1_g, ln1_b, w_router, b_router, alpha)
    pos, tile_expert, n_valid, pad_start, n_tiles = _route(top_i, rank, counts, MOE_TM)
    xs = _dispatch(h3, pos, pad_start, n_tiles * MOE_TM, MOE_TM)
    ys = _moe(xs, tile_expert, n_valid, n_tiles, w_gate, b_gate, w_up, b_up, w_down, b_down)
    return _combine(h3, gates, pos, ys, ln2_g, ln2_b, alpha).reshape(bsz, seq, d)


def kernel(x, positions, w_mix_in, idx_k_norm_g, idx_k_norm_b, mla_q_norm_g, mla_w_q_up, mla_kv_norm_g, mla_w_kv_up, w_mix_out, ln1_g, ln1_b, w_router, b_router, w_gate, b_gate, w_up, b_up, w_down, b_down, ln2_g, ln2_b):
    depth = w_mix_in.shape[0]
    alpha = float((2 * depth) ** 0.25)
    tabs = _rope_tables(positions)
    for l in range(depth):
        x = _layer(x, tabs, alpha, w_mix_in[l], idx_k_norm_g[l], idx_k_norm_b[l], mla_q_norm_g[l], mla_w_q_up[l],
                   mla_kv_norm_g[l], mla_w_kv_up[l], w_mix_out[l], ln1_g[l], ln1_b[l], w_router[l], b_router[l],
                   w_gate[l], b_gate[l], w_up[l], b_up[l], w_down[l], b_down[l], ln2_g[l], ln2_b[l])
    return x
```

```python
import functools

import jax
import jax.numpy as jnp
import numpy as np
from jax import lax
from jax.experimental import pallas as pl
from jax.experimental.pallas import tpu as pltpu

F32 = jnp.float32
BF16 = jnp.bfloat16

A_HEADS, A_HEAD_DIM = 8, 64
IDX_HEADS, IDX_DIM = 8, 64
TOPK_MAX = 256
B_HEADS, B_NOPE, B_ROPE, B_V = 8, 64, 32, 64
Q_LORA, KV_LORA = 256, 128
N_EXPERTS, TOP_K = 32, 4
SWIGLU_LIMIT, SWIGLU_ALPHA = 7.0, 1.702
ROPE_THETA = 10000.0
LN_EPS, RMS_EPS = 1e-5, 1e-6

LANES = 128
SUBLANES = 8
VMEM_LIMIT = 56 * 1024 * 1024

PROJ_TM = 512
ATT_TQ = LANES
ATT_TK = 512
SCORE_TK = 1024
SEL_TK = 512
ONES_ROWS = 16
LOG2_E = 1.4426950408889634
MIX_TM = 512
MOE_TM = 256
ROW_TM = 256
SEL_MAX_ITERS = 400
SEL_UNROLL = 4

MASKED = -1e30
M_FLOOR = -1e29
TINY = float(np.finfo(np.float32).tiny)


def _dot(a, b):
    return jnp.dot(a, b, preferred_element_type=F32)


def _dot_t(a, b):
    return lax.dot_general(a, b, (((1,), (1,)), ((), ())), preferred_element_type=F32)


def _lane_iota(shape):
    return lax.broadcasted_iota(jnp.int32, shape, len(shape) - 1)


def _fold(v, op, rows=64):
    n = v.shape[0] // rows
    acc = v[0:rows]
    for i in range(1, n):
        acc = op(acc, v[i * rows:(i + 1) * rows])
    return acc


def _rows_reduce(v, op):
    if op == "sum":
        return jnp.sum(v, axis=0, keepdims=True)
    if op == "max":
        return jnp.max(v, axis=0, keepdims=True)
    return jnp.min(v, axis=0, keepdims=True)


_R_AQ, _R_AQR, _R_IQ, _R_IQR, _R_V, _R_IW, _R_QC, _R_KVC, _R_TOTAL = 0, 512, 1024, 1536, 2048, 2112, 2128, 2384, 2512
_C_KV, _C_KVR, _C_IK, _C_KVC, _C_KR, _C_KRR, _C_TOTAL = 0, 128, 256, 384, 512, 640, 768


def _rot_cols(w, dim):
    k, n = w.shape
    w3 = w.reshape(k, n // dim, dim)
    return jnp.concatenate([-w3[..., dim // 2:], w3[..., :dim // 2]], axis=-1).reshape(k, n)


def _pad_cols(w, width):
    return jnp.pad(w, ((0, 0), (0, width - w.shape[1])))


def _pack_weights(w_in, w_q_up, w_kv_up):
    o = np.cumsum((0, 512, 64, 64, 512, 64, 8, 256, 128, 32))
    a_q, a_k, a_v, i_q, i_k, i_w, b_qc, b_kvc, b_kr = [w_in[:, o[i]:o[i + 1]] for i in range(9)]
    w_t = jnp.concatenate([a_q, _rot_cols(a_q, 64), i_q, _rot_cols(i_q, 64), a_v, _pad_cols(i_w, 16),
                           b_qc, b_kvc], axis=1).T
    assert w_t.shape[0] == _R_TOTAL
    w_n = jnp.concatenate([a_k, a_v, _rot_cols(a_k, 64), jnp.zeros_like(a_k), _pad_cols(i_k, 128), b_kvc,
                           _pad_cols(b_kr, 128), _pad_cols(_rot_cols(b_kr, 32), 128)], axis=1)
    assert w_n.shape[1] == _C_TOTAL
    wq = w_q_up.reshape(Q_LORA, B_HEADS, B_NOPE + B_ROPE)
    wq_rope = wq[..., B_NOPE:].reshape(Q_LORA, B_HEADS * B_ROPE)
    wq_t = jnp.concatenate([wq[..., :B_NOPE].reshape(Q_LORA, B_HEADS * B_NOPE), wq_rope,
                            _rot_cols(wq_rope, B_ROPE)], axis=1).T
    wkv = w_kv_up.reshape(KV_LORA, B_HEADS, B_NOPE + B_V)
    eye = jnp.eye(B_HEADS, dtype=w_kv_up.dtype)
    bd_uk_t = jnp.einsum('lhd,hg->glhd', wkv[..., :B_NOPE], eye).reshape(B_HEADS * KV_LORA, B_HEADS * B_NOPE)
    w_uv_t = jnp.transpose(wkv[..., B_NOPE:], (1, 2, 0))
    return w_t.astype(BF16), w_n.astype(BF16), wq_t.astype(BF16), bd_uk_t.astype(BF16), w_uv_t.astype(BF16)


def _rope_tables(positions):
    pos = positions.astype(F32)[..., None]
    inv64 = ROPE_THETA ** (-jnp.arange(0, 64, 2, dtype=F32) / 64)
    inv32 = ROPE_THETA ** (-jnp.arange(0, 32, 2, dtype=F32) / 32)
    c64, s64 = jnp.cos(pos * inv64), jnp.sin(pos * inv64)
    c32, s32 = jnp.cos(pos * inv32), jnp.sin(pos * inv32)
    nat = (jnp.tile(c64, 4), jnp.tile(s64, 4), jnp.tile(c32, 8), jnp.tile(s32, 8))
    tr = tuple(jnp.swapaxes(jnp.tile(t, 2), 1, 2) for t in (c64, s64, c32, s32))
    return nat + tr


def _proj_kernel(x_ref, ca_ref, sa_ref, cr_ref, sr_ref, cat_ref, sat_ref, crt_ref, srt_ref,
                 wt_ref, wn_ref, wqt_ref, bdukt_ref, ikg_ref, ikb_ref, qg_ref, kvg_ref, kvgt_ref,
                 aqt_ref, iqt_ref, iwt_ref, akv_ref, avt_ref, ik_ref, qabst_ref, ckv_ref, latt_ref):
    tm = x_ref.shape[0]
    nqb = tm // ATT_TQ
    xb = x_ref[...].astype(BF16)
    ca, sa, cr, sr = ca_ref[...], sa_ref[...], cr_ref[...], sr_ref[...]
    lane = _lane_iota((tm, LANES))
    low64 = lane < 64

    def proj_t(r0, rows):
        return _dot_t(wt_ref[r0:r0 + rows, :], xb)

    def proj_n(c0, width):
        return _dot(xb, wn_ref[:, c0:c0 + width])

    cat8 = jnp.concatenate([cat_ref[...]] * 8, axis=0)
    sat8 = jnp.concatenate([sat_ref[...]] * 8, axis=0)

    def split_heads(v, out_ref):
        vb = v.astype(BF16)
        zeros = jnp.zeros((64, ATT_TQ), BF16)
        for c in range(nqb):
            for h in range(8):
                out_ref[c, 0:64, h * ATT_TQ:(h + 1) * ATT_TQ] = vb[64 * h:64 * h + 64, c * ATT_TQ:(c + 1) * ATT_TQ]
                out_ref[c, 64:128, h * ATT_TQ:(h + 1) * ATT_TQ] = zeros

    aq_t = (proj_t(_R_AQ, 512) * cat8 + proj_t(_R_AQR, 512) * sat8) * (A_HEAD_DIM ** -0.5 * LOG2_E)
    split_heads(aq_t, aqt_ref)
    iq_t = proj_t(_R_IQ, 512) * cat8 + proj_t(_R_IQR, 512) * sat8
    split_heads(iq_t, iqt_ref)
    iw_t = proj_t(_R_IW, 16)[0:IDX_HEADS] * ((IDX_HEADS * IDX_DIM) ** -0.5)
    for c in range(nqb):
        iwt_ref[c] = iw_t[:, c * ATT_TQ:(c + 1) * ATT_TQ]
    ones_rows = jnp.where(lax.broadcasted_iota(jnp.int32, (ONES_ROWS, tm), 0) == 0, 1.0, 0.0).astype(BF16)
    avt_ref[...] = jnp.concatenate([proj_t(_R_V, 64).astype(BF16), ones_rows], axis=0)

    kc = jnp.where(low64, ca, 1.0)
    ks = jnp.where(low64, sa, 0.0)
    akv_ref[...] = (proj_n(_C_KV, 128) * kc + proj_n(_C_KVR, 128) * ks).astype(BF16)

    y = proj_n(_C_IK, 128)
    mu = jnp.sum(y, axis=1, keepdims=True) * (1.0 / IDX_DIM)
    d = jnp.where(low64, y - mu, 0.0)
    var = jnp.sum(d * d, axis=1, keepdims=True) * (1.0 / IDX_DIM)
    yn = d * lax.rsqrt(var + LN_EPS) * ikg_ref[...] + ikb_ref[...]
    rot = jnp.where(lane < 32, -pltpu.roll(yn, LANES - 32, 1), pltpu.roll(yn, 32, 1))
    ik_ref[...] = jnp.where(low64, yn * ca + rot * sa, 0.0).astype(BF16)

    qc_t = proj_t(_R_QC, Q_LORA)
    qn_t = (qc_t * lax.rsqrt(jnp.mean(qc_t * qc_t, axis=0, keepdims=True) + RMS_EPS) * qg_ref[...]).astype(BF16)
    q_t = _dot(wqt_ref[...], qn_t)
    scale = (B_NOPE + B_ROPE) ** -0.5 * LOG2_E
    q_lat_t = (_dot(bdukt_ref[...], q_t[0:512].astype(BF16)) * scale).astype(BF16)
    crt8 = jnp.concatenate([crt_ref[...]] * 8, axis=0)
    srt8 = jnp.concatenate([srt_ref[...]] * 8, axis=0)
    q_r_t = ((q_t[512:768] * crt8 + q_t[768:1024] * srt8) * scale).astype(BF16)
    zeros = jnp.zeros((2 * LANES - KV_LORA - B_ROPE, ATT_TQ), BF16)
    for c in range(nqb):
        tok = slice(c * ATT_TQ, (c + 1) * ATT_TQ)
        for h in range(B_HEADS):
            col = slice(h * ATT_TQ, (h + 1) * ATT_TQ)
            qabst_ref[c, 0:KV_LORA, col] = q_lat_t[h * KV_LORA:(h + 1) * KV_LORA, tok]
            qabst_ref[c, KV_LORA:KV_LORA + B_ROPE, col] = q_r_t[h * B_ROPE:(h + 1) * B_ROPE, tok]
            qabst_ref[c, KV_LORA + B_ROPE:2 * LANES, col] = zeros

    kvc = proj_n(_C_KVC, KV_LORA)
    kvn = kvc * lax.rsqrt(jnp.mean(kvc * kvc, axis=1, keepdims=True) + RMS_EPS) * kvg_ref[...]
    kr = proj_n(_C_KR, 128) * cr + proj_n(_C_KRR, 128) * sr
    ckv_ref[...] = jnp.concatenate([kvn, kr], axis=1).astype(BF16)
    kvc_t = proj_t(_R_KVC, KV_LORA)
    lat_t = (kvc_t * lax.rsqrt(jnp.mean(kvc_t * kvc_t, axis=0, keepdims=True) + RMS_EPS) * kvgt_ref[...]).astype(BF16)
    latt_ref[...] = jnp.concatenate([lat_t, ones_rows], axis=0)


def _proj(x, tabs, w_t, w_n, wq_t, bd_uk_t, ik_g, ik_b, q_g, kv_g):
    bsz, seq, d = x.shape
    tm = PROJ_TM
    nqb = tm // ATT_TQ
    tok = lambda b, i: (b, i, 0)
    feat = lambda b, i: (b, 0, i)
    qblk = lambda b, i: (b, i, 0, 0)
    const2 = lambda b, i: (0, 0)
    nat_spec = pl.BlockSpec((None, tm, LANES), tok)
    col = lambda v: jnp.broadcast_to(v.reshape(-1, 1), (v.shape[0], tm))
    ik_g = jnp.pad(ik_g, (0, LANES - IDX_DIM)).reshape(1, LANES)
    ik_b = jnp.pad(ik_b, (0, LANES - IDX_DIM)).reshape(1, LANES)
    consts = (w_t, w_n, wq_t, bd_uk_t, ik_g, ik_b, col(q_g), kv_g.reshape(1, KV_LORA), col(kv_g))
    return pl.pallas_call(
        _proj_kernel,
        grid=(bsz, seq // tm),
        in_specs=[pl.BlockSpec((None, tm, d), tok), nat_spec, nat_spec, nat_spec, nat_spec,
                  pl.BlockSpec((None, 64, tm), feat), pl.BlockSpec((None, 64, tm), feat),
                  pl.BlockSpec((None, 32, tm), feat), pl.BlockSpec((None, 32, tm), feat)]
                 + [pl.BlockSpec(c.shape, const2) for c in consts],
        out_specs=[pl.BlockSpec((None, nqb, LANES, A_HEADS * ATT_TQ), qblk),
                   pl.BlockSpec((None, nqb, LANES, IDX_HEADS * ATT_TQ), qblk),
                   pl.BlockSpec((None, nqb, IDX_HEADS, ATT_TQ), qblk),
                   pl.BlockSpec((None, tm, LANES), tok),
                   pl.BlockSpec((None, A_HEAD_DIM + ONES_ROWS, tm), feat),
                   pl.BlockSpec((None, tm, LANES), tok),
                   pl.BlockSpec((None, nqb, 2 * LANES, B_HEADS * ATT_TQ), qblk),
                   pl.BlockSpec((None, tm, 2 * LANES), tok),
                   pl.BlockSpec((None, KV_LORA + ONES_ROWS, tm), feat)],
        out_shape=[jax.ShapeDtypeStruct((bsz, seq // ATT_TQ, LANES, A_HEADS * ATT_TQ), BF16),
                   jax.ShapeDtypeStruct((bsz, seq // ATT_TQ, LANES, IDX_HEADS * ATT_TQ), BF16),
                   jax.ShapeDtypeStruct((bsz, seq // ATT_TQ, IDX_HEADS, ATT_TQ), F32),
                   jax.ShapeDtypeStruct((bsz, seq, LANES), BF16),
                   jax.ShapeDtypeStruct((bsz, A_HEAD_DIM + ONES_ROWS, seq), BF16),
                   jax.ShapeDtypeStruct((bsz, seq, LANES), BF16),
                   jax.ShapeDtypeStruct((bsz, seq // ATT_TQ, 2 * LANES, B_HEADS * ATT_TQ), BF16),
                   jax.ShapeDtypeStruct((bsz, seq, 2 * LANES), BF16),
                   jax.ShapeDtypeStruct((bsz, KV_LORA + ONES_ROWS, seq), BF16)],
        compiler_params=pltpu.CompilerParams(
            dimension_semantics=("arbitrary", "arbitrary"), vmem_limit_bytes=VMEM_LIMIT),
        name="proj",
    )(x, *tabs, *consts)


def _attend_init(nh, val_rows):
    return ([jnp.full((1, LANES), M_FLOOR, F32) for _ in range(nh)],
            jnp.zeros((val_rows + ONES_ROWS, nh * LANES), F32))


def _attend_update(lg_ref, bias, vt_blk, state):
    ms, acc = state
    new_ms, ps, alphas = [], [], []
    for h in range(len(ms)):
        lh = lg_ref[:, h * LANES:(h + 1) * LANES]
        if bias is not None:
            lh = lh + bias
        m_new = jnp.maximum(ms[h], _rows_reduce(_fold(lh, jnp.maximum), "max").astype(F32))
        alphas.append(jnp.exp2(ms[h] - m_new))
        ps.append(jnp.exp2(lh - m_new.astype(BF16)))
        new_ms.append(m_new)
    return new_ms, jnp.concatenate(alphas, axis=1) * acc + _dot(vt_blk, jnp.concatenate(ps, axis=1))


def _attend_pipelined(n_blocks, scores, update, lg_a, lg_b, state):
    n_look = (n_blocks - 1) // 2

    def pair(i, st, lookahead):
        kb0 = 2 * i
        lg_b[...] = scores(kb0 + 1).astype(lg_b.dtype)
        st = update(kb0, lg_a, not lookahead, st)
        if lookahead:
            lg_a[...] = scores(kb0 + 2).astype(lg_a.dtype)
        return update(kb0 + 1, lg_b, not lookahead, st)

    lg_a[...] = scores(0).astype(lg_a.dtype)
    state = lax.fori_loop(0, n_look, lambda i, st: pair(i, st, True), state)
    return lax.cond(n_blocks % 2 == 0,
                    lambda st: pair(n_look, st, False),
                    lambda st: update(2 * n_look, lg_a, True, st), state)


def _attend_finish(state, val_rows):
    acc = state[1]
    return acc[0:val_rows] / acc[val_rows:val_rows + 1]


def _dsa_kernel(aqt_ref, iqt_ref, iwt_ref, akv_ref, avt_ref, ik_ref, out_ref, s_sc, lg_a, lg_b):
    nh = iwt_ref.shape[0]
    tq, tk = ATT_TQ, ATT_TK
    seq = akv_ref.shape[0]
    j = pl.program_id(1)
    nkb = (j * tq) // tk + 1
    iqt = iqt_ref[...]
    w = iwt_ref[...]

    ts = SCORE_TK
    krow = lax.broadcasted_iota(jnp.int32, (ts, tq), 0)
    qcol = j * tq + lax.broadcasted_iota(jnp.int32, (ts, tq), 1)

    def score_body(kb, carry):
        cmax, cmin = carry
        k0 = pl.multiple_of(kb * ts, ts)
        y = _dot(ik_ref[pl.ds(k0, ts), :], iqt)
        s = jnp.zeros((ts, tq), F32)
        for h in range(nh):
            s = s + w[h:h + 1, :] * jnp.maximum(y[:, h * tq:(h + 1) * tq], 0.0)
        causal = (k0 + krow) <= qcol
        s_lo = jnp.where(causal, s, -jnp.inf)
        s_sc[pl.ds(k0, ts), :] = s_lo
        cmax = jnp.maximum(cmax, _fold(s_lo, jnp.maximum))
        cmin = jnp.minimum(cmin, _fold(jnp.where(causal, s, jnp.inf), jnp.minimum))
        return cmax, cmin

    cmax, cmin = lax.fori_loop(0, (j * tq) // ts + 1, score_body,
                               (jnp.full((64, tq), -jnp.inf, F32), jnp.full((64, tq), jnp.inf, F32)))
    rmax = _rows_reduce(cmax, "max")
    rmin = _rows_reduce(cmin, "min")

    n_causal = (j * tq + lax.broadcasted_iota(jnp.int32, (1, tq), 1) + 1).astype(F32)
    kt = jnp.minimum(n_causal, float(TOPK_MAX))

    tc = SEL_TK
    ncb = (j * tq) // tc + 1
    crow_f = lax.broadcasted_iota(jnp.int32, (tc, tq), 0).astype(F32)

    def count(pred):
        def body(cb, acc):
            k0 = pl.multiple_of(cb * tc, tc)
            return acc + _fold(jnp.where(pred(s_sc[pl.ds(k0, tc), :], k0), 1.0, 0.0), jnp.add)
        return _rows_reduce(lax.fori_loop(0, ncb, body, jnp.zeros((64, tq), F32)), "sum")

    def propose(it, lo, hi, clo):
        mid = 0.5 * lo + 0.5 * hi
        mid = jnp.where(jnp.logical_and(it == 0, jnp.logical_and(lo < 0.0, hi > 0.0)), 0.0, mid)
        mid = jnp.where(jnp.logical_and(it == 1, jnp.logical_and(lo == 0.0, hi > TINY)), TINY, mid)
        active = jnp.logical_and(clo != kt, jnp.logical_and(mid > lo, mid < hi))
        return mid, active

    def sel_step(it, lo, hi, clo, chi):
        mid, active = propose(it, lo, hi, clo)
        cm = count(lambda blk, k0: blk >= mid)
        ge = cm >= kt
        up = jnp.logical_and(active, ge)
        dn = jnp.logical_and(active, jnp.logical_not(ge))
        return jnp.where(up, mid, lo), jnp.where(dn, mid, hi), jnp.where(up, cm, clo), jnp.where(dn, cm, chi)

    def sel_body(c):
        it, _, lo, hi, clo, chi = c
        for u in range(SEL_UNROLL):
            lo, hi, clo, chi = sel_step(it + u, lo, hi, clo, chi)
        _, active = propose(it + SEL_UNROLL, lo, hi, clo)
        go = jnp.logical_and(jnp.max(jnp.where(active, 1.0, 0.0)) > 0.0, it + SEL_UNROLL < SEL_MAX_ITERS)
        return it + SEL_UNROLL, go, lo, hi, clo, chi

    lo0 = rmin
    hi0 = rmax + jnp.abs(rmax) + 1e-30
    clo0, chi0 = n_causal, jnp.zeros((1, tq), F32)
    _, active0 = propose(0, lo0, hi0, clo0)
    go0 = jnp.max(jnp.where(active0, 1.0, 0.0)) > 0.0
    _, _, lo, hi, clo, chi = lax.while_loop(lambda c: c[1], sel_body, (jnp.int32(0), go0, lo0, hi0, clo0, chi0))

    tie = clo > kt

    @pl.when(jnp.max(jnp.where(tie, 1.0, 0.0)) > 0.0)
    def _():
        need = kt - chi

        def idx_body(_, c):
            jl, jh = c
            jm = jnp.floor((jl + jh) * 0.5)
            ok = count(lambda blk, k0: jnp.logical_and(blk == lo, crow_f + k0.astype(F32) <= jm)) >= need
            return jnp.where(ok, jl, jm), jnp.where(ok, jm, jh)

        n_steps = int(np.ceil(np.log2(seq))) + 1
        _, jh = lax.fori_loop(0, n_steps, idx_body,
                              (jnp.full((1, tq), -1.0, F32), jnp.full((1, tq), float(seq - 1), F32)))
        jh = jnp.where(tie, jh, float(seq))

        def fix_body(cb, _):
            k0 = pl.multiple_of(cb * tc, tc)
            blk = s_sc[pl.ds(k0, tc), :]
            drop = jnp.logical_and(blk == lo, crow_f + k0.astype(F32) > jh)
            s_sc[pl.ds(k0, tc), :] = jnp.where(drop, -jnp.inf, blk)
            return 0

        lax.fori_loop(0, ncb, fix_body, 0)

    def att_scores(kb):
        return _dot(akv_ref[pl.ds(pl.multiple_of(kb * tk, tk), tk), :], aqt_ref[...])

    def att_update(kb, lg_ref, tail, state):
        k0 = pl.multiple_of(kb * tk, tk)
        bias = jnp.where(s_sc[pl.ds(k0, tk), :] >= lo, 0.0, MASKED).astype(BF16)
        return _attend_update(lg_ref, bias, avt_ref[:, pl.ds(k0, tk)], state)

    state = _attend_pipelined(nkb, att_scores, att_update, lg_a, lg_b, _attend_init(nh, A_HEAD_DIM))
    o_t = _attend_finish(state, A_HEAD_DIM)
    for p in range(nh // 2):
        pair = jnp.concatenate([o_t[:, (2 * p) * tq:(2 * p + 1) * tq], o_t[:, (2 * p + 1) * tq:(2 * p + 2) * tq]], axis=0)
        out_ref[:, p * LANES:(p + 1) * LANES] = pair.T.astype(out_ref.dtype)


def _dsa(aqt, iqt, iwt, akv, avt, ikp):
    bsz, nqb, _, hq = aqt.shape
    nh = iwt.shape[2]
    seq = akv.shape[1]
    tq = ATT_TQ
    qblk = lambda b, j: (b, j, 0, 0)
    full = lambda b, j: (b, 0, 0)
    return pl.pallas_call(
        _dsa_kernel,
        grid=(bsz, nqb),
        in_specs=[pl.BlockSpec((None, None, LANES, hq), qblk), pl.BlockSpec((None, None, LANES, hq), qblk),
                  pl.BlockSpec((None, None, nh, tq), qblk),
                  pl.BlockSpec((None, seq, LANES), full), pl.BlockSpec((None, A_HEAD_DIM + ONES_ROWS, seq), full),
                  pl.BlockSpec((None, seq, LANES), full)],
        out_specs=pl.BlockSpec((None, tq, nh * A_HEAD_DIM), lambda b, j: (b, j, 0)),
        out_shape=jax.ShapeDtypeStruct((bsz, seq, nh * A_HEAD_DIM), BF16),
        scratch_shapes=[pltpu.VMEM((seq, tq), F32), pltpu.VMEM((ATT_TK, hq), BF16), pltpu.VMEM((ATT_TK, hq), BF16)],
        compiler_params=pltpu.CompilerParams(
            dimension_semantics=("arbitrary", "arbitrary"), vmem_limit_bytes=VMEM_LIMIT),
        name="dsa",
    )(aqt, iqt, iwt, akv, avt, ikp)


def _mla_kernel(qt_ref, ckv_ref, latt_ref, wuvt_ref, out_ref, lg_a, lg_b):
    nh = wuvt_ref.shape[0]
    tq, tk = ATT_TQ, ATT_TK
    j = pl.program_id(1)
    krow = lax.broadcasted_iota(jnp.int32, (tk, tq), 0)
    qcol = j * tq + lax.broadcasted_iota(jnp.int32, (1, tq), 1)

    def scores(kb):
        return _dot(ckv_ref[pl.ds(pl.multiple_of(kb * tk, tk), tk), :], qt_ref[...])

    def update(kb, lg_ref, tail, state):
        k0 = pl.multiple_of(kb * tk, tk)
        bias = None
        if tail:
            bias = jnp.where((k0 + krow) <= qcol, 0.0, MASKED).astype(BF16)
        return _attend_update(lg_ref, bias, latt_ref[:, pl.ds(k0, tk)], state)

    state = _attend_pipelined((j * tq) // tk + 1, scores, update, lg_a, lg_b, _attend_init(nh, KV_LORA))
    o_lat_t = _attend_finish(state, KV_LORA).astype(BF16)
    for p in range(nh // 2):
        pair = jnp.concatenate([_dot(wuvt_ref[h], o_lat_t[:, h * tq:(h + 1) * tq]) for h in (2 * p, 2 * p + 1)], axis=0)
        out_ref[:, p * LANES:(p + 1) * LANES] = pair.T.astype(out_ref.dtype)


def _mla(qabst, ckv, latt, w_uv_t):
    bsz, nqb, dq, hq = qabst.shape
    nh = w_uv_t.shape[0]
    seq = ckv.shape[1]
    tq = ATT_TQ
    return pl.pallas_call(
        _mla_kernel,
        grid=(bsz, nqb),
        in_specs=[pl.BlockSpec((None, None, dq, hq), lambda b, j: (b, j, 0, 0)),
                  pl.BlockSpec((None, seq, dq), lambda b, j: (b, 0, 0)),
                  pl.BlockSpec((None, KV_LORA + ONES_ROWS, seq), lambda b, j: (b, 0, 0)),
                  pl.BlockSpec(w_uv_t.shape, lambda b, j: (0, 0, 0))],
        out_specs=pl.BlockSpec((None, tq, nh * B_V), lambda b, j: (b, j, 0)),
        out_shape=jax.ShapeDtypeStruct((bsz, seq, nh * B_V), BF16),
        scratch_shapes=[pltpu.VMEM((ATT_TK, hq), BF16), pltpu.VMEM((ATT_TK, hq), BF16)],
        compiler_params=pltpu.CompilerParams(
            dimension_semantics=("arbitrary", "arbitrary"), vmem_limit_bytes=VMEM_LIMIT),
        name="mla",
    )(qabst, ckv, latt, w_uv_t)


def _layer_norm(z, g, b):
    mu = jnp.mean(z, axis=1, keepdims=True)
    d = z - mu
    var = jnp.mean(d * d, axis=1, keepdims=True)
    return d * lax.rsqrt(var + LN_EPS) * g + b


def _split_bf16(v):
    hi = v.astype(BF16)
    return hi, (v - hi.astype(F32)).astype(BF16)


def _to_row_tiles(v, ref):
    for c in range(v.shape[1] // LANES):
        ref[:, c, :] = v[:, c * LANES:(c + 1) * LANES]


def _from_row_tiles(ref):
    return jnp.concatenate([ref[:, c, :] for c in range(ref.shape[1])], axis=1)


def _mix_kernel(x_ref, a_ref, b_ref, wout_ref, g_ref, beta_ref, wr_ref, br_ref,
                h_ref, ti_ref, tg_ref, rank_ref, cnt_ref, carry_sc, *, alpha):
    tm = x_ref.shape[0]
    half = a_ref.shape[1]
    i = pl.program_id(0)

    @pl.when(i == 0)
    def _():
        carry_sc[...] = jnp.zeros(carry_sc.shape, F32)

    mix = _dot(a_ref[...], wout_ref[0:half, :]) + _dot(b_ref[...], wout_ref[half:2 * half, :])
    h = _layer_norm(alpha * x_ref[...] + mix, g_ref[...], beta_ref[...])
    _to_row_tiles(h, h_ref)
    h_hi, h_lo = _split_bf16(h)
    w_hi, w_lo = _split_bf16(wr_ref[...])
    logits = _dot(h_hi, w_hi) + (_dot(h_lo, w_hi) + _dot(h_hi, w_lo)) + br_ref[...]
    lane = _lane_iota((tm, LANES))
    logits = jnp.where(lane < N_EXPERTS, logits, -jnp.inf)
    ids = jnp.zeros((tm, LANES), jnp.int32)
    vals, hots = [], []
    for k in range(TOP_K):
        m = jnp.max(logits, axis=1, keepdims=True)
        idx = jnp.min(jnp.where(logits == m, lane, LANES), axis=1, keepdims=True)
        hot = lane == idx
        ids = jnp.where(lane == k, idx, ids)
        logits = jnp.where(hot, -jnp.inf, logits)
        vals.append(m)
        hots.append(hot)
    es = [jnp.exp(v - vals[0]) for v in vals]
    den = (es[0] + es[1]) + (es[2] + es[3])
    ti_ref[...] = ids[:, :TOP_K]
    for k in range(TOP_K):
        tg_ref[:, k, :] = jnp.broadcast_to(es[k] / den, (tm, LANES))

    onehot = jnp.where(jnp.logical_or(jnp.logical_or(hots[0], hots[1]), jnp.logical_or(hots[2], hots[3])), 1.0, 0.0)
    r_i = lax.broadcasted_iota(jnp.int32, (tm, tm), 0)
    c_i = lax.broadcasted_iota(jnp.int32, (tm, tm), 1)
    before = _dot(jnp.where(c_i < r_i, 1.0, 0.0).astype(BF16), onehot.astype(BF16)) + carry_sc[...]
    ranks = jnp.zeros((tm, LANES), F32)
    for k in range(TOP_K):
        ranks = jnp.where(lane == k, jnp.sum(jnp.where(hots[k], before, 0.0), axis=1, keepdims=True), ranks)
    rank_ref[...] = ranks[:, :TOP_K].astype(jnp.int32)
    carry_sc[...] = carry_sc[...] + jnp.sum(onehot, axis=0, keepdims=True)
    cnt_ref[...] = carry_sc[...].astype(jnp.int32)


def _mix(x2, a_out, b_out, w_out, g, beta, w_router, b_router, alpha):
    n, d = x2.shape
    tm = MIX_TM
    half = a_out.shape[1]
    row = lambda i: (i, 0)
    const = lambda i: (0, 0)
    wr = jnp.pad(w_router, ((0, 0), (0, LANES - N_EXPERTS)))
    br = jnp.pad(b_router, (0, LANES - N_EXPERTS)).reshape(1, LANES)
    return pl.pallas_call(
        functools.partial(_mix_kernel, alpha=alpha),
        grid=(n // tm,),
        in_specs=[pl.BlockSpec((tm, d), row), pl.BlockSpec((tm, half), row), pl.BlockSpec((tm, half), row),
                  pl.BlockSpec(w_out.shape, const), pl.BlockSpec((1, d), const), pl.BlockSpec((1, d), const),
                  pl.BlockSpec(wr.shape, const), pl.BlockSpec((1, LANES), const)],
        out_specs=[pl.BlockSpec((tm, d // LANES, LANES), lambda i: (i, 0, 0)), pl.BlockSpec((tm, TOP_K), row),
                   pl.BlockSpec((tm, TOP_K, LANES), lambda i: (i, 0, 0)), pl.BlockSpec((tm, TOP_K), row),
                   pl.BlockSpec((1, LANES), const)],
        out_shape=[jax.ShapeDtypeStruct((n, d // LANES, LANES), F32), jax.ShapeDtypeStruct((n, TOP_K), jnp.int32),
                   jax.ShapeDtypeStruct((n, TOP_K, LANES), F32), jax.ShapeDtypeStruct((n, TOP_K), jnp.int32),
                   jax.ShapeDtypeStruct((1, LANES), jnp.int32)],
        scratch_shapes=[pltpu.VMEM((1, LANES), F32)],
        compiler_params=pltpu.CompilerParams(dimension_semantics=("arbitrary",), vmem_limit_bytes=VMEM_LIMIT),
        name="mix",
    )(x2, a_out, b_out, w_out.astype(BF16), g.reshape(1, d), beta.reshape(1, d), wr, br)


def _route(top_i, rank, counts, tm):
    n = top_i.shape[0]
    n_tiles = (n * TOP_K) // tm + N_EXPERTS
    counts = counts[0, :N_EXPERTS]
    tiles_per = (counts + tm - 1) // tm
    tile_end = jnp.cumsum(tiles_per)
    pstart = (tile_end - tiles_per) * tm
    pos = rank + jnp.sum(jnp.where(top_i[..., None] == jnp.arange(N_EXPERTS, dtype=jnp.int32), pstart, 0), axis=-1)
    tile_ids = jnp.arange(n_tiles, dtype=jnp.int32)
    n_valid = tile_end[-1]
    tile_expert = jnp.sum(tile_end[None, :] <= jnp.minimum(tile_ids, n_valid - 1)[:, None], axis=1).astype(jnp.int32)
    n_valid = n_valid[None].astype(jnp.int32)
    pad_start = jnp.concatenate([pstart + counts, n_valid]).astype(jnp.int32)
    return pos.astype(jnp.int32), tile_expert, n_valid, pad_start, n_tiles


def _dispatch_kernel(pad_ref, h_ref, pos_hbm, xs_hbm, pos_s, zbuf, sem_idx, sem_row):
    i = pl.program_id(0)
    tm = h_ref.shape[0]

    def pos_copy(t, slot):
        return pltpu.make_async_copy(pos_hbm.at[t], pos_s.at[pl.ds(slot * (tm * TOP_K), tm * TOP_K)], sem_idx.at[slot])

    @pl.when(i == 0)
    def _():
        zt = zbuf.shape[0]
        zbuf[...] = jnp.zeros(zbuf.shape, zbuf.dtype)
        for e in range(N_EXPERTS):
            cpe = pltpu.make_async_copy(zbuf, xs_hbm.at[pl.ds(pad_ref[e], zt)], sem_row)
            cpe.start()
            cpe.wait()

        def fill(t, _):
            cpz = pltpu.make_async_copy(zbuf, xs_hbm.at[pl.ds(pl.multiple_of(t * zt, zt), zt)], sem_row)
            cpz.start()
            cpz.wait()
            return 0

        lax.fori_loop(pad_ref[N_EXPERTS], xs_hbm.shape[0] // zt, fill, 0)
        pos_copy(0, 0).start()

    slot = i % 2
    pos_copy(i, slot).wait()

    @pl.when(i + 1 < pl.num_programs(0))
    def _():
        pos_copy(i + 1, 1 - slot).start()

    base = slot * (tm * TOP_K)

    def body(r, _):
        for s in range(TOP_K):
            pltpu.make_async_copy(h_ref.at[r], xs_hbm.at[pos_s[base + r * TOP_K + s]], sem_row).start(priority=s % 2)
        return 0

    lax.fori_loop(0, tm, body, 0, unroll=8)
    for s in range(TOP_K):
        pltpu.make_async_copy(h_ref, xs_hbm.at[pl.ds(0, tm)], sem_row).wait()


def _dispatch(h3, pos, pad_start, n_rows, zrows):
    n, nt8, _ = h3.shape
    tm = ROW_TM
    return pl.pallas_call(
        _dispatch_kernel,
        grid_spec=pltpu.PrefetchScalarGridSpec(
            num_scalar_prefetch=1,
            grid=(n // tm,),
            in_specs=[pl.BlockSpec((tm, nt8, LANES), lambda i, pad: (i, 0, 0)), pl.BlockSpec(memory_space=pl.ANY)],
            out_specs=pl.BlockSpec(memory_space=pl.ANY),
            scratch_shapes=[pltpu.SMEM((2 * tm * TOP_K,), jnp.int32), pltpu.VMEM((zrows, nt8, LANES), F32),
                            pltpu.SemaphoreType.DMA((2,)), pltpu.SemaphoreType.DMA]),
        out_shape=jax.ShapeDtypeStruct((n_rows + zrows, nt8, LANES), F32),
        compiler_params=pltpu.CompilerParams(dimension_semantics=("arbitrary",), vmem_limit_bytes=VMEM_LIMIT),
        name="dispatch",
    )(pad_start, h3, pos.reshape(n // tm, tm * TOP_K))


def _moe_kernel(te_ref, nv_ref, xs_hbm, wg_ref, bg_ref, wu_ref, bu_ref, wd_ref, bd_ref, ys_hbm,
                xbuf, ybuf, wg_bf, wu_bf, wd_bf, sem_in, sem_out):
    i = pl.program_id(0)
    n_valid = nv_ref[0]
    tm = ybuf.shape[0]
    nt8 = xs_hbm.shape[1]

    def in_copy(t, slot, c):
        return pltpu.make_async_copy(xs_hbm.at[pl.ds(t * tm, tm), c, :], xbuf.at[slot, :, pl.ds(c * LANES, LANES)],
                                     sem_in.at[slot])

    def out_copy(t, c):
        return pltpu.make_async_copy(ybuf.at[:, pl.ds(c * LANES, LANES)], ys_hbm.at[pl.ds(t * tm, tm), c, :], sem_out)

    def fetch(t, slot):
        for c in range(nt8):
            in_copy(t, slot, c).start()

    def store(t):
        for c in range(nt8):
            out_copy(t, c).start()

    def wait_store(t):
        for c in range(nt8):
            out_copy(t, c).wait()

    @pl.when(i == 0)
    def _():
        fetch(0, 0)

    @pl.when(i < n_valid)
    def _():
        slot = i % 2
        for c in range(nt8):
            in_copy(i, slot, c).wait()

        @pl.when(i + 1 < n_valid)
        def _():
            fetch(i + 1, 1 - slot)

        @pl.when(jnp.logical_or(i == 0, te_ref[i] != te_ref[jnp.maximum(i - 1, 0)]))
        def _():
            wg_bf[...] = wg_ref[0].astype(BF16)
            wu_bf[...] = wu_ref[0].astype(BF16)
            wd_bf[...] = wd_ref[0].astype(BF16)

        x = xbuf[slot].astype(BF16)
        g = jnp.minimum(_dot(x, wg_bf[...]) + bg_ref[0], SWIGLU_LIMIT)
        u = jnp.clip(_dot(x, wu_bf[...]) + bu_ref[0], -SWIGLU_LIMIT, SWIGLU_LIMIT)
        act = g * (1.0 / (1.0 + jnp.exp(-SWIGLU_ALPHA * g))) * (u + 1.0)
        y = _dot(act.astype(BF16), wd_bf[...]) + bd_ref[0]

        @pl.when(i > 0)
        def _():
            wait_store(i - 1)

        ybuf[...] = y
        store(i)

        @pl.when(i == n_valid - 1)
        def _():
            wait_store(i)

    @pl.when(i >= n_valid)
    def _():
        ybuf[...] = jnp.zeros(ybuf.shape, ybuf.dtype)
        store(i)
        wait_store(i)


def _moe(xs, tile_expert, n_valid, n_tiles, w_gate, b_gate, w_up, b_up, w_down, b_down):
    _, nt8, _ = xs.shape
    tm = MOE_TM
    ne, d, dff = w_gate.shape
    wmap = lambda i, te, nv: (te[i], 0, 0)
    any_spec = pl.BlockSpec(memory_space=pl.ANY)
    grid_spec = pltpu.PrefetchScalarGridSpec(
        num_scalar_prefetch=2,
        grid=(n_tiles,),
        in_specs=[any_spec,
                  pl.BlockSpec((1, d, dff), wmap), pl.BlockSpec((1, 1, dff), wmap),
                  pl.BlockSpec((1, d, dff), wmap), pl.BlockSpec((1, 1, dff), wmap),
                  pl.BlockSpec((1, dff, d), wmap), pl.BlockSpec((1, 1, d), wmap)],
        out_specs=any_spec,
        scratch_shapes=[pltpu.VMEM((2, tm, d), F32), pltpu.VMEM((tm, d), F32),
                        pltpu.VMEM((d, dff), BF16), pltpu.VMEM((d, dff), BF16), pltpu.VMEM((dff, d), BF16),
                        pltpu.SemaphoreType.DMA((2,)), pltpu.SemaphoreType.DMA])
    return pl.pallas_call(
        _moe_kernel,
        grid_spec=grid_spec,
        out_shape=jax.ShapeDtypeStruct((n_tiles * tm, nt8, LANES), F32),
        compiler_params=pltpu.CompilerParams(dimension_semantics=("arbitrary",), vmem_limit_bytes=VMEM_LIMIT),
        name="moe",
    )(tile_expert, n_valid, xs,
      w_gate, b_gate.reshape(ne, 1, dff), w_up, b_up.reshape(ne, 1, dff), w_down, b_down.reshape(ne, 1, d))


def _combine_kernel(h_ref, gate_ref, pos_hbm, ys_hbm, g_ref, beta_ref, o_ref, pos_s, ybuf, sem_idx, sem_row, *, alpha):
    i = pl.program_id(0)
    n = pl.num_programs(0)
    tm = h_ref.shape[0]
    slot = i % 2

    def pos_copy(t, sl):
        return pltpu.make_async_copy(pos_hbm.at[t], pos_s.at[pl.ds(sl * (tm * TOP_K), tm * TOP_K)], sem_idx.at[sl])

    def gather(sl):
        base = sl * (tm * TOP_K)

        def body(r, _):
            for s in range(TOP_K):
                pltpu.make_async_copy(ys_hbm.at[pos_s[base + r * TOP_K + s]], ybuf.at[sl, s, r],
                                      sem_row.at[sl]).start(priority=s % 2)
            return 0

        lax.fori_loop(0, tm, body, 0, unroll=8)

    @pl.when(i == 0)
    def _():
        first = pos_copy(0, 0)
        first.start()
        first.wait()
        gather(0)

        @pl.when(n > 1)
        def _():
            pos_copy(1, 1).start()

    @pl.when(i + 1 < n)
    def _():
        pos_copy(i + 1, 1 - slot).wait()
        gather(1 - slot)

        @pl.when(i + 2 < n)
        def _():
            pos_copy(i + 2, slot).start()

    for s in range(TOP_K):
        pltpu.make_async_copy(ys_hbm.at[pl.ds(0, tm)], ybuf.at[slot, s], sem_row.at[slot]).wait()

    z = alpha * h_ref[...]
    for s in range(TOP_K):
        z = z + gate_ref[:, s:s + 1, :] * ybuf[slot, s]
    inv_d = 1.0 / (z.shape[1] * z.shape[2])
    mu = jnp.sum(z, axis=(1, 2), keepdims=True) * inv_d
    dz = z - mu
    var = jnp.sum(dz * dz, axis=(1, 2), keepdims=True) * inv_d
    o_ref[...] = dz * lax.rsqrt(var + LN_EPS) * g_ref[...] + beta_ref[...]


def _combine(h3, gates, pos, ys, g, beta, alpha):
    n, nt8, _ = h3.shape
    tm = ROW_TM
    tile = lambda i: (i, 0, 0)
    const = lambda i: (0, 0, 0)
    return pl.pallas_call(
        functools.partial(_combine_kernel, alpha=alpha),
        grid=(n // tm,),
        in_specs=[pl.BlockSpec((tm, nt8, LANES), tile), pl.BlockSpec((tm, TOP_K, LANES), tile),
                  pl.BlockSpec(memory_space=pl.ANY), pl.BlockSpec(memory_space=pl.ANY),
                  pl.BlockSpec((1, nt8, LANES), const), pl.BlockSpec((1, nt8, LANES), const)],
        out_specs=pl.BlockSpec((tm, nt8, LANES), tile),
        out_shape=jax.ShapeDtypeStruct((n, nt8, LANES), F32),
        scratch_shapes=[pltpu.SMEM((2 * tm * TOP_K,), jnp.int32), pltpu.VMEM((2, TOP_K, tm, nt8, LANES), F32),
                        pltpu.SemaphoreType.DMA((2,)), pltpu.SemaphoreType.DMA((2,))],
        compiler_params=pltpu.CompilerParams(dimension_semantics=("arbitrary",), vmem_limit_bytes=VMEM_LIMIT),
        name="combine",
    )(h3, gates, pos.reshape(n // tm, tm * TOP_K), ys, g.reshape(1, nt8, LANES), beta.reshape(1, nt8, LANES))


def _layer(x, tabs, alpha, w_in, ik_g, ik_b, q_g, w_q_up, kv_g, w_kv_up, w_out, ln1_g, ln1_b,
           w_router, b_router, w_gate, b_gate, w_up, b_up, w_down, b_down, ln2_g, ln2_b):
    bsz, seq, d = x.shape
    w_t, w_n, wq_t, bd_uk_t, w_uv_t = _pack_weights(w_in, w_q_up, w_kv_up)
    aqt, iqt, iwt, akv, avt, ikp, qabst, ckv, latt = _proj(x, tabs, w_t, w_n, wq_t, bd_uk_t, ik_g, ik_b, q_g, kv_g)
    a_out = _dsa(aqt, iqt, iwt, akv, avt, ikp)
    b_out = _mla(qabst, ckv, latt, w_uv_t)
    n = bsz * seq
    h3, top_i, gates, rank, counts = _mix(x.reshape(n, d), a_out.reshape(n, -1), b_out.reshape(n, -1), w_out,
                                          ln1_g, ln1_b, w_router, b_router, alpha)
    pos, tile_expert, n_valid, pad_start, n_tiles = _route(top_i, rank, counts, MOE_TM)
    xs = _dispatch(h3, pos, pad_start, n_tiles * MOE_TM, MOE_TM)
    ys = _moe(xs, tile_expert, n_valid, n_tiles, w_gate, b_gate, w_up, b_up, w_down, b_down)
    return _combine(h3, gates, pos, ys, ln2_g, ln2_b, alpha).reshape(bsz, seq, d)


def kernel(x, positions, w_mix_in, idx_k_norm_g, idx_k_norm_b, mla_q_norm_g, mla_w_q_up, mla_kv_norm_g, mla_w_kv_up, w_mix_out, ln1_g, ln1_b, w_router, b_router, w_gate, b_gate, w_up, b_up, w_down, b_down, ln2_g, ln2_b):
    depth = w_mix_in.shape[0]
    alpha = float((2 * depth) ** 0.25)
    tabs = _rope_tables(positions)
    for l in range(depth):
        x = _layer(x, tabs, alpha, w_mix_in[l], idx_k_norm_g[l], idx_k_norm_b[l], mla_q_norm_g[l], mla_w_q_up[l],
                   mla_kv_norm_g[l], mla_w_kv_up[l], w_mix_out[l], ln1_g[l], ln1_b[l], w_router[l], b_router[l],
                   w_gate[l], b_gate[l], w_up[l], b_up[l], w_down[l], b_down[l], ln2_g[l], ln2_b[l])
    return x
```

```python
import functools

import jax
import jax.numpy as jnp
import numpy as np
from jax import lax
from jax.experimental import pallas as pl
from jax.experimental.pallas import tpu as pltpu

F32 = jnp.float32
BF16 = jnp.bfloat16

A_HEADS, A_HEAD_DIM = 8, 64
IDX_HEADS, IDX_DIM = 8, 64
TOPK_MAX = 256
B_HEADS, B_NOPE, B_ROPE, B_V = 8, 64, 32, 64
Q_LORA, KV_LORA = 256, 128
N_EXPERTS, TOP_K = 32, 4
SWIGLU_LIMIT, SWIGLU_ALPHA = 7.0, 1.702
ROPE_THETA = 10000.0
LN_EPS, RMS_EPS = 1e-5, 1e-6

LANES = 128
SUBLANES = 8
VMEM_LIMIT = 56 * 1024 * 1024

PROJ_TM = 512
ATT_TQ = LANES
ATT_TK = 1024
SCORE_TK = 1024
SEL_TK = 512
ONES_ROWS = 16
LOG2_E = 1.4426950408889634
MIX_TM = 512
MOE_TM = 256
ROW_TM = 256
SEL_MAX_ITERS = 400
SEL_UNROLL = 4

MASKED = -1e30
M_FLOOR = -1e29
TINY = float(np.finfo(np.float32).tiny)


def _dot(a, b):
    return jnp.dot(a, b, preferred_element_type=F32)


def _dot_t(a, b):
    return lax.dot_general(a, b, (((1,), (1,)), ((), ())), preferred_element_type=F32)


def _lane_iota(shape):
    return lax.broadcasted_iota(jnp.int32, shape, len(shape) - 1)


def _fold(v, op, rows=64):
    n = v.shape[0] // rows
    acc = v[0:rows]
    for i in range(1, n):
        acc = op(acc, v[i * rows:(i + 1) * rows])
    return acc


def _rows_reduce(v, op):
    if op == "sum":
        return jnp.sum(v, axis=0, keepdims=True)
    if op == "max":
        return jnp.max(v, axis=0, keepdims=True)
    return jnp.min(v, axis=0, keepdims=True)


_R_AQ, _R_AQR, _R_IQ, _R_IQR, _R_V, _R_IW, _R_QC, _R_KVC, _R_TOTAL = 0, 512, 1024, 1536, 2048, 2112, 2128, 2384, 2512
_C_KV, _C_KVR, _C_IK, _C_KVC, _C_KR, _C_KRR, _C_TOTAL = 0, 128, 256, 384, 512, 640, 768


def _rot_cols(w, dim):
    k, n = w.shape
    w3 = w.reshape(k, n // dim, dim)
    return jnp.concatenate([-w3[..., dim // 2:], w3[..., :dim // 2]], axis=-1).reshape(k, n)


def _pad_cols(w, width):
    return jnp.pad(w, ((0, 0), (0, width - w.shape[1])))


def _pack_weights(w_in, w_q_up, w_kv_up):
    o = np.cumsum((0, 512, 64, 64, 512, 64, 8, 256, 128, 32))
    a_q, a_k, a_v, i_q, i_k, i_w, b_qc, b_kvc, b_kr = [w_in[:, o[i]:o[i + 1]] for i in range(9)]
    w_t = jnp.concatenate([a_q, _rot_cols(a_q, 64), i_q, _rot_cols(i_q, 64), a_v, _pad_cols(i_w, 16),
                           b_qc, b_kvc], axis=1).T
    assert w_t.shape[0] == _R_TOTAL
    w_n = jnp.concatenate([a_k, a_v, _rot_cols(a_k, 64), jnp.zeros_like(a_k), _pad_cols(i_k, 128), b_kvc,
                           _pad_cols(b_kr, 128), _pad_cols(_rot_cols(b_kr, 32), 128)], axis=1)
    assert w_n.shape[1] == _C_TOTAL
    wq = w_q_up.reshape(Q_LORA, B_HEADS, B_NOPE + B_ROPE)
    wq_rope = wq[..., B_NOPE:].reshape(Q_LORA, B_HEADS * B_ROPE)
    wq_t = jnp.concatenate([wq[..., :B_NOPE].reshape(Q_LORA, B_HEADS * B_NOPE), wq_rope,
                            _rot_cols(wq_rope, B_ROPE)], axis=1).T
    wkv = w_kv_up.reshape(KV_LORA, B_HEADS, B_NOPE + B_V)
    eye = jnp.eye(B_HEADS, dtype=w_kv_up.dtype)
    bd_uk_t = jnp.einsum('lhd,hg->glhd', wkv[..., :B_NOPE], eye).reshape(B_HEADS * KV_LORA, B_HEADS * B_NOPE)
    w_uv_t = jnp.transpose(wkv[..., B_NOPE:], (1, 2, 0))
    return w_t.astype(BF16), w_n.astype(BF16), wq_t.astype(BF16), bd_uk_t.astype(BF16), w_uv_t.astype(BF16)


def _rope_tables(positions):
    pos = positions.astype(F32)[..., None]
    inv64 = ROPE_THETA ** (-jnp.arange(0, 64, 2, dtype=F32) / 64)
    inv32 = ROPE_THETA ** (-jnp.arange(0, 32, 2, dtype=F32) / 32)
    c64, s64 = jnp.cos(pos * inv64), jnp.sin(pos * inv64)
    c32, s32 = jnp.cos(pos * inv32), jnp.sin(pos * inv32)
    nat = (jnp.tile(c64, 4), jnp.tile(s64, 4), jnp.tile(c32, 8), jnp.tile(s32, 8))
    tr = tuple(jnp.swapaxes(jnp.tile(t, 2), 1, 2) for t in (c64, s64, c32, s32))
    return nat + tr


def _proj_kernel(x_ref, ca_ref, sa_ref, cr_ref, sr_ref, cat_ref, sat_ref, crt_ref, srt_ref,
                 wt_ref, wn_ref, wqt_ref, bdukt_ref, ikg_ref, ikb_ref, qg_ref, kvg_ref, kvgt_ref,
                 aqt_ref, iqt_ref, iwt_ref, akv_ref, avt_ref, ik_ref, qabst_ref, ckv_ref, latt_ref):
    tm = x_ref.shape[0]
    nqb = tm // ATT_TQ
    xb = x_ref[...].astype(BF16)
    ca, sa, cr, sr = ca_ref[...], sa_ref[...], cr_ref[...], sr_ref[...]
    lane = _lane_iota((tm, LANES))
    low64 = lane < 64

    def proj_t(r0, rows):
        return _dot_t(wt_ref[r0:r0 + rows, :], xb)

    def proj_n(c0, width):
        return _dot(xb, wn_ref[:, c0:c0 + width])

    cat8 = jnp.concatenate([cat_ref[...]] * 8, axis=0)
    sat8 = jnp.concatenate([sat_ref[...]] * 8, axis=0)

    def split_heads(v, out_ref):
        vb = v.astype(BF16)
        zeros = jnp.zeros((64, ATT_TQ), BF16)
        for c in range(nqb):
            for h in range(8):
                out_ref[c, 0:64, h * ATT_TQ:(h + 1) * ATT_TQ] = vb[64 * h:64 * h + 64, c * ATT_TQ:(c + 1) * ATT_TQ]
                out_ref[c, 64:128, h * ATT_TQ:(h + 1) * ATT_TQ] = zeros

    aq_t = (proj_t(_R_AQ, 512) * cat8 + proj_t(_R_AQR, 512) * sat8) * (A_HEAD_DIM ** -0.5 * LOG2_E)
    split_heads(aq_t, aqt_ref)
    iq_t = proj_t(_R_IQ, 512) * cat8 + proj_t(_R_IQR, 512) * sat8
    split_heads(iq_t, iqt_ref)
    iw_t = proj_t(_R_IW, 16)[0:IDX_HEADS] * ((IDX_HEADS * IDX_DIM) ** -0.5)
    for c in range(nqb):
        iwt_ref[c] = iw_t[:, c * ATT_TQ:(c + 1) * ATT_TQ]
    ones_rows = jnp.where(lax.broadcasted_iota(jnp.int32, (ONES_ROWS, tm), 0) == 0, 1.0, 0.0).astype(BF16)
    avt_ref[...] = jnp.concatenate([proj_t(_R_V, 64).astype(BF16), ones_rows], axis=0)

    kc = jnp.where(low64, ca, 1.0)
    ks = jnp.where(low64, sa, 0.0)
    akv_ref[...] = (proj_n(_C_KV, 128) * kc + proj_n(_C_KVR, 128) * ks).astype(BF16)

    y = proj_n(_C_IK, 128)
    mu = jnp.sum(y, axis=1, keepdims=True) * (1.0 / IDX_DIM)
    d = jnp.where(low64, y - mu, 0.0)
    var = jnp.sum(d * d, axis=1, keepdims=True) * (1.0 / IDX_DIM)
    yn = d * lax.rsqrt(var + LN_EPS) * ikg_ref[...] + ikb_ref[...]
    rot = jnp.where(lane < 32, -pltpu.roll(yn, LANES - 32, 1), pltpu.roll(yn, 32, 1))
    ik_ref[...] = jnp.where(low64, yn * ca + rot * sa, 0.0).astype(BF16)

    qc_t = proj_t(_R_QC, Q_LORA)
    qn_t = (qc_t * lax.rsqrt(jnp.mean(qc_t * qc_t, axis=0, keepdims=True) + RMS_EPS) * qg_ref[...]).astype(BF16)
    q_t = _dot(wqt_ref[...], qn_t)
    scale = (B_NOPE + B_ROPE) ** -0.5 * LOG2_E
    q_lat_t = (_dot(bdukt_ref[...], q_t[0:512].astype(BF16)) * scale).astype(BF16)
    crt8 = jnp.concatenate([crt_ref[...]] * 8, axis=0)
    srt8 = jnp.concatenate([srt_ref[...]] * 8, axis=0)
    q_r_t = ((q_t[512:768] * crt8 + q_t[768:1024] * srt8) * scale).astype(BF16)
    zeros = jnp.zeros((2 * LANES - KV_LORA - B_ROPE, ATT_TQ), BF16)
    for c in range(nqb):
        tok = slice(c * ATT_TQ, (c + 1) * ATT_TQ)
        for h in range(B_HEADS):
            col = slice(h * ATT_TQ, (h + 1) * ATT_TQ)
            qabst_ref[c, 0:KV_LORA, col] = q_lat_t[h * KV_LORA:(h + 1) * KV_LORA, tok]
            qabst_ref[c, KV_LORA:KV_LORA + B_ROPE, col] = q_r_t[h * B_ROPE:(h + 1) * B_ROPE, tok]
            qabst_ref[c, KV_LORA + B_ROPE:2 * LANES, col] = zeros

    kvc = proj_n(_C_KVC, KV_LORA)
    kvn = kvc * lax.rsqrt(jnp.mean(kvc * kvc, axis=1, keepdims=True) + RMS_EPS) * kvg_ref[...]
    kr = proj_n(_C_KR, 128) * cr + proj_n(_C_KRR, 128) * sr
    ckv_ref[...] = jnp.concatenate([kvn, kr], axis=1).astype(BF16)
    kvc_t = proj_t(_R_KVC, KV_LORA)
    lat_t = (kvc_t * lax.rsqrt(jnp.mean(kvc_t * kvc_t, axis=0, keepdims=True) + RMS_EPS) * kvgt_ref[...]).astype(BF16)
    latt_ref[...] = jnp.concatenate([lat_t, ones_rows], axis=0)


def _proj(x, tabs, w_t, w_n, wq_t, bd_uk_t, ik_g, ik_b, q_g, kv_g):
    bsz, seq, d = x.shape
    tm = PROJ_TM
    nqb = tm // ATT_TQ
    tok = lambda b, i: (b, i, 0)
    feat = lambda b, i: (b, 0, i)
    qblk = lambda b, i: (b, i, 0, 0)
    const2 = lambda b, i: (0, 0)
    nat_spec = pl.BlockSpec((None, tm, LANES), tok)
    col = lambda v: jnp.broadcast_to(v.reshape(-1, 1), (v.shape[0], tm))
    ik_g = jnp.pad(ik_g, (0, LANES - IDX_DIM)).reshape(1, LANES)
    ik_b = jnp.pad(ik_b, (0, LANES - IDX_DIM)).reshape(1, LANES)
    consts = (w_t, w_n, wq_t, bd_uk_t, ik_g, ik_b, col(q_g), kv_g.reshape(1, KV_LORA), col(kv_g))
    return pl.pallas_call(
        _proj_kernel,
        grid=(bsz, seq // tm),
        in_specs=[pl.BlockSpec((None, tm, d), tok), nat_spec, nat_spec, nat_spec, nat_spec,
                  pl.BlockSpec((None, 64, tm), feat), pl.BlockSpec((None, 64, tm), feat),
                  pl.BlockSpec((None, 32, tm), feat), pl.BlockSpec((None, 32, tm), feat)]
                 + [pl.BlockSpec(c.shape, const2) for c in consts],
        out_specs=[pl.BlockSpec((None, nqb, LANES, A_HEADS * ATT_TQ), qblk),
                   pl.BlockSpec((None, nqb, LANES, IDX_HEADS * ATT_TQ), qblk),
                   pl.BlockSpec((None, nqb, IDX_HEADS, ATT_TQ), qblk),
                   pl.BlockSpec((None, tm, LANES), tok),
                   pl.BlockSpec((None, A_HEAD_DIM + ONES_ROWS, tm), feat),
                   pl.BlockSpec((None, tm, LANES), tok),
                   pl.BlockSpec((None, nqb, 2 * LANES, B_HEADS * ATT_TQ), qblk),
                   pl.BlockSpec((None, tm, 2 * LANES), tok),
                   pl.BlockSpec((None, KV_LORA + ONES_ROWS, tm), feat)],
        out_shape=[jax.ShapeDtypeStruct((bsz, seq // ATT_TQ, LANES, A_HEADS * ATT_TQ), BF16),
                   jax.ShapeDtypeStruct((bsz, seq // ATT_TQ, LANES, IDX_HEADS * ATT_TQ), BF16),
                   jax.ShapeDtypeStruct((bsz, seq // ATT_TQ, IDX_HEADS, ATT_TQ), F32),
                   jax.ShapeDtypeStruct((bsz, seq, LANES), BF16),
                   jax.ShapeDtypeStruct((bsz, A_HEAD_DIM + ONES_ROWS, seq), BF16),
                   jax.ShapeDtypeStruct((bsz, seq, LANES), BF16),
                   jax.ShapeDtypeStruct((bsz, seq // ATT_TQ, 2 * LANES, B_HEADS * ATT_TQ), BF16),
                   jax.ShapeDtypeStruct((bsz, seq, 2 * LANES), BF16),
                   jax.ShapeDtypeStruct((bsz, KV_LORA + ONES_ROWS, seq), BF16)],
        compiler_params=pltpu.CompilerParams(
            dimension_semantics=("arbitrary", "arbitrary"), vmem_limit_bytes=VMEM_LIMIT),
        name="proj",
    )(x, *tabs, *consts)


def _attend_init(nh, val_rows):
    return ([jnp.full((1, LANES), M_FLOOR, F32) for _ in range(nh)],
            jnp.zeros((val_rows + ONES_ROWS, nh * LANES), F32))


def _attend_update(lg_ref, bias, vt_blk, state):
    ms, acc = state
    new_ms, ps, alphas = [], [], []
    for h in range(len(ms)):
        lh = lg_ref[:, h * LANES:(h + 1) * LANES]
        if bias is not None:
            lh = lh + bias
        m_new = jnp.maximum(ms[h], _rows_reduce(_fold(lh, jnp.maximum), "max"))
        alphas.append(jnp.exp2(ms[h] - m_new))
        ps.append(jnp.exp2(lh - m_new).astype(BF16))
        new_ms.append(m_new)
    return new_ms, jnp.concatenate(alphas, axis=1) * acc + _dot(vt_blk, jnp.concatenate(ps, axis=1))


def _attend_pipelined(n_blocks, scores, update, lg_a, lg_b, state):
    n_look = (n_blocks - 1) // 2

    def pair(i, st, lookahead):
        kb0 = 2 * i
        lg_b[...] = scores(kb0 + 1)
        st = update(kb0, lg_a, not lookahead, st)
        if lookahead:
            lg_a[...] = scores(kb0 + 2)
        return update(kb0 + 1, lg_b, not lookahead, st)

    lg_a[...] = scores(0)
    state = lax.fori_loop(0, n_look, lambda i, st: pair(i, st, True), state)
    return lax.cond(n_blocks % 2 == 0,
                    lambda st: pair(n_look, st, False),
                    lambda st: update(2 * n_look, lg_a, True, st), state)


def _attend_finish(state, val_rows):
    acc = state[1]
    return acc[0:val_rows] / acc[val_rows:val_rows + 1]


def _dsa_kernel(aqt_ref, iqt_ref, iwt_ref, akv_ref, avt_ref, ik_ref, out_ref, s_sc, lg_a, lg_b):
    nh = iwt_ref.shape[0]
    tq, tk = ATT_TQ, ATT_TK
    seq = akv_ref.shape[0]
    j = pl.program_id(1)
    nkb = (j * tq) // tk + 1
    iqt = iqt_ref[...]
    w = iwt_ref[...]

    ts = SCORE_TK
    krow = lax.broadcasted_iota(jnp.int32, (ts, tq), 0)
    qcol = j * tq + lax.broadcasted_iota(jnp.int32, (ts, tq), 1)

    def score_body(kb, carry):
        cmax, cmin = carry
        k0 = pl.multiple_of(kb * ts, ts)
        y = _dot(ik_ref[pl.ds(k0, ts), :], iqt)
        s = jnp.zeros((ts, tq), F32)
        for h in range(nh):
            s = s + w[h:h + 1, :] * jnp.maximum(y[:, h * tq:(h + 1) * tq], 0.0)
        causal = (k0 + krow) <= qcol
        s_lo = jnp.where(causal, s, -jnp.inf)
        s_sc[pl.ds(k0, ts), :] = s_lo
        cmax = jnp.maximum(cmax, _fold(s_lo, jnp.maximum))
        cmin = jnp.minimum(cmin, _fold(jnp.where(causal, s, jnp.inf), jnp.minimum))
        return cmax, cmin

    cmax, cmin = lax.fori_loop(0, (j * tq) // ts + 1, score_body,
                               (jnp.full((64, tq), -jnp.inf, F32), jnp.full((64, tq), jnp.inf, F32)))
    rmax = _rows_reduce(cmax, "max")
    rmin = _rows_reduce(cmin, "min")

    n_causal = (j * tq + lax.broadcasted_iota(jnp.int32, (1, tq), 1) + 1).astype(F32)
    kt = jnp.minimum(n_causal, float(TOPK_MAX))

    tc = SEL_TK
    ncb = (j * tq) // tc + 1
    crow_f = lax.broadcasted_iota(jnp.int32, (tc, tq), 0).astype(F32)

    def count(pred):
        def body(cb, acc):
            k0 = pl.multiple_of(cb * tc, tc)
            return acc + _fold(jnp.where(pred(s_sc[pl.ds(k0, tc), :], k0), 1.0, 0.0), jnp.add)
        return _rows_reduce(lax.fori_loop(0, ncb, body, jnp.zeros((64, tq), F32)), "sum")

    def propose(it, lo, hi, clo):
        mid = 0.5 * lo + 0.5 * hi
        mid = jnp.where(jnp.logical_and(it == 0, jnp.logical_and(lo < 0.0, hi > 0.0)), 0.0, mid)
        mid = jnp.where(jnp.logical_and(it == 1, jnp.logical_and(lo == 0.0, hi > TINY)), TINY, mid)
        active = jnp.logical_and(clo != kt, jnp.logical_and(mid > lo, mid < hi))
        return mid, active

    def sel_step(it, lo, hi, clo, chi):
        mid, active = propose(it, lo, hi, clo)
        cm = count(lambda blk, k0: blk >= mid)
        ge = cm >= kt
        up = jnp.logical_and(active, ge)
        dn = jnp.logical_and(active, jnp.logical_not(ge))
        return jnp.where(up, mid, lo), jnp.where(dn, mid, hi), jnp.where(up, cm, clo), jnp.where(dn, cm, chi)

    def sel_body(c):
        it, _, lo, hi, clo, chi = c
        for u in range(SEL_UNROLL):
            lo, hi, clo, chi = sel_step(it + u, lo, hi, clo, chi)
        _, active = propose(it + SEL_UNROLL, lo, hi, clo)
        go = jnp.logical_and(jnp.max(jnp.where(active, 1.0, 0.0)) > 0.0, it + SEL_UNROLL < SEL_MAX_ITERS)
        return it + SEL_UNROLL, go, lo, hi, clo, chi

    lo0 = rmin
    hi0 = rmax + jnp.abs(rmax) + 1e-30
    clo0, chi0 = n_causal, jnp.zeros((1, tq), F32)
    _, active0 = propose(0, lo0, hi0, clo0)
    go0 = jnp.max(jnp.where(active0, 1.0, 0.0)) > 0.0
    _, _, lo, hi, clo, chi = lax.while_loop(lambda c: c[1], sel_body, (jnp.int32(0), go0, lo0, hi0, clo0, chi0))

    tie = clo > kt

    @pl.when(jnp.max(jnp.where(tie, 1.0, 0.0)) > 0.0)
    def _():
        need = kt - chi

        def idx_body(_, c):
            jl, jh = c
            jm = jnp.floor((jl + jh) * 0.5)
            ok = count(lambda blk, k0: jnp.logical_and(blk == lo, crow_f + k0.astype(F32) <= jm)) >= need
            return jnp.where(ok, jl, jm), jnp.where(ok, jm, jh)

        n_steps = int(np.ceil(np.log2(seq))) + 1
        _, jh = lax.fori_loop(0, n_steps, idx_body,
                              (jnp.full((1, tq), -1.0, F32), jnp.full((1, tq), float(seq - 1), F32)))
        jh = jnp.where(tie, jh, float(seq))

        def fix_body(cb, _):
            k0 = pl.multiple_of(cb * tc, tc)
            blk = s_sc[pl.ds(k0, tc), :]
            drop = jnp.logical_and(blk == lo, crow_f + k0.astype(F32) > jh)
            s_sc[pl.ds(k0, tc), :] = jnp.where(drop, -jnp.inf, blk)
            return 0

        lax.fori_loop(0, ncb, fix_body, 0)

    def att_scores(kb):
        return _dot(akv_ref[pl.ds(pl.multiple_of(kb * tk, tk), tk), :], aqt_ref[...])

    def att_update(kb, lg_ref, tail, state):
        k0 = pl.multiple_of(kb * tk, tk)
        bias = jnp.where(s_sc[pl.ds(k0, tk), :] >= lo, 0.0, MASKED)
        return _attend_update(lg_ref, bias, avt_ref[:, pl.ds(k0, tk)], state)

    state = _attend_pipelined(nkb, att_scores, att_update, lg_a, lg_b, _attend_init(nh, A_HEAD_DIM))
    o_t = _attend_finish(state, A_HEAD_DIM)
    for p in range(nh // 2):
        pair = jnp.concatenate([o_t[:, (2 * p) * tq:(2 * p + 1) * tq], o_t[:, (2 * p + 1) * tq:(2 * p + 2) * tq]], axis=0)
        out_ref[:, p * LANES:(p + 1) * LANES] = pair.T.astype(out_ref.dtype)


def _dsa(aqt, iqt, iwt, akv, avt, ikp):
    bsz, nqb, _, hq = aqt.shape
    nh = iwt.shape[2]
    seq = akv.shape[1]
    tq = ATT_TQ
    qblk = lambda b, j: (b, j, 0, 0)
    full = lambda b, j: (b, 0, 0)
    return pl.pallas_call(
        _dsa_kernel,
        grid=(bsz, nqb),
        in_specs=[pl.BlockSpec((None, None, LANES, hq), qblk), pl.BlockSpec((None, None, LANES, hq), qblk),
                  pl.BlockSpec((None, None, nh, tq), qblk),
                  pl.BlockSpec((None, seq, LANES), full), pl.BlockSpec((None, A_HEAD_DIM + ONES_ROWS, seq), full),
                  pl.BlockSpec((None, seq, LANES), full)],
        out_specs=pl.BlockSpec((None, tq, nh * A_HEAD_DIM), lambda b, j: (b, j, 0)),
        out_shape=jax.ShapeDtypeStruct((bsz, seq, nh * A_HEAD_DIM), BF16),
        scratch_shapes=[pltpu.VMEM((seq, tq), F32), pltpu.VMEM((ATT_TK, hq), F32), pltpu.VMEM((ATT_TK, hq), F32)],
        compiler_params=pltpu.CompilerParams(
            dimension_semantics=("arbitrary", "arbitrary"), vmem_limit_bytes=VMEM_LIMIT),
        name="dsa",
    )(aqt, iqt, iwt, akv, avt, ikp)


def _mla_kernel(qt_ref, ckv_ref, latt_ref, wuvt_ref, out_ref, lg_a, lg_b):
    nh = wuvt_ref.shape[0]
    tq, tk = ATT_TQ, ATT_TK
    j = pl.program_id(1)
    krow = lax.broadcasted_iota(jnp.int32, (tk, tq), 0)
    qcol = j * tq + lax.broadcasted_iota(jnp.int32, (1, tq), 1)

    def scores(kb):
        return _dot(ckv_ref[pl.ds(pl.multiple_of(kb * tk, tk), tk), :], qt_ref[...])

    def update(kb, lg_ref, tail, state):
        k0 = pl.multiple_of(kb * tk, tk)
        bias = None
        if tail:
            bias = jnp.where((k0 + krow) <= qcol, 0.0, MASKED)
        return _attend_update(lg_ref, bias, latt_ref[:, pl.ds(k0, tk)], state)

    state = _attend_pipelined((j * tq) // tk + 1, scores, update, lg_a, lg_b, _attend_init(nh, KV_LORA))
    o_lat_t = _attend_finish(state, KV_LORA).astype(BF16)
    for p in range(nh // 2):
        pair = jnp.concatenate([_dot(wuvt_ref[h], o_lat_t[:, h * tq:(h + 1) * tq]) for h in (2 * p, 2 * p + 1)], axis=0)
        out_ref[:, p * LANES:(p + 1) * LANES] = pair.T.astype(out_ref.dtype)


def _mla(qabst, ckv, latt, w_uv_t):
    bsz, nqb, dq, hq = qabst.shape
    nh = w_uv_t.shape[0]
    seq = ckv.shape[1]
    tq = ATT_TQ
    return pl.pallas_call(
        _mla_kernel,
        grid=(bsz, nqb),
        in_specs=[pl.BlockSpec((None, None, dq, hq), lambda b, j: (b, j, 0, 0)),
                  pl.BlockSpec((None, seq, dq), lambda b, j: (b, 0, 0)),
                  pl.BlockSpec((None, KV_LORA + ONES_ROWS, seq), lambda b, j: (b, 0, 0)),
                  pl.BlockSpec(w_uv_t.shape, lambda b, j: (0, 0, 0))],
        out_specs=pl.BlockSpec((None, tq, nh * B_V), lambda b, j: (b, j, 0)),
        out_shape=jax.ShapeDtypeStruct((bsz, seq, nh * B_V), BF16),
        scratch_shapes=[pltpu.VMEM((ATT_TK, hq), F32), pltpu.VMEM((ATT_TK, hq), F32)],
        compiler_params=pltpu.CompilerParams(
            dimension_semantics=("arbitrary", "arbitrary"), vmem_limit_bytes=VMEM_LIMIT),
        name="mla",
    )(qabst, ckv, latt, w_uv_t)


def _layer_norm(z, g, b):
    mu = jnp.mean(z, axis=1, keepdims=True)
    d = z - mu
    var = jnp.mean(d * d, axis=1, keepdims=True)
    return d * lax.rsqrt(var + LN_EPS) * g + b


def _split_bf16(v):
    hi = v.astype(BF16)
    return hi, (v - hi.astype(F32)).astype(BF16)


def _to_row_tiles(v, ref):
    for c in range(v.shape[1] // LANES):
        ref[:, c, :] = v[:, c * LANES:(c + 1) * LANES]


def _from_row_tiles(ref):
    return jnp.concatenate([ref[:, c, :] for c in range(ref.shape[1])], axis=1)


def _mix_kernel(x_ref, a_ref, b_ref, wout_ref, g_ref, beta_ref, wr_ref, br_ref,
                h_ref, ti_ref, tg_ref, rank_ref, cnt_ref, carry_sc, *, alpha):
    tm = x_ref.shape[0]
    half = a_ref.shape[1]
    i = pl.program_id(0)

    @pl.when(i == 0)
    def _():
        carry_sc[...] = jnp.zeros(carry_sc.shape, F32)

    mix = _dot(a_ref[...], wout_ref[0:half, :]) + _dot(b_ref[...], wout_ref[half:2 * half, :])
    h = _layer_norm(alpha * x_ref[...] + mix, g_ref[...], beta_ref[...])
    _to_row_tiles(h, h_ref)
    h_hi, h_lo = _split_bf16(h)
    w_hi, w_lo = _split_bf16(wr_ref[...])
    logits = _dot(h_hi, w_hi) + (_dot(h_lo, w_hi) + _dot(h_hi, w_lo)) + br_ref[...]
    lane = _lane_iota((tm, LANES))
    logits = jnp.where(lane < N_EXPERTS, logits, -jnp.inf)
    ids = jnp.zeros((tm, LANES), jnp.int32)
    vals, hots = [], []
    for k in range(TOP_K):
        m = jnp.max(logits, axis=1, keepdims=True)
        idx = jnp.min(jnp.where(logits == m, lane, LANES), axis=1, keepdims=True)
        hot = lane == idx
        ids = jnp.where(lane == k, idx, ids)
        logits = jnp.where(hot, -jnp.inf, logits)
        vals.append(m)
        hots.append(hot)
    es = [jnp.exp(v - vals[0]) for v in vals]
    den = (es[0] + es[1]) + (es[2] + es[3])
    ti_ref[...] = ids[:, :TOP_K]
    for k in range(TOP_K):
        tg_ref[:, k, :] = jnp.broadcast_to(es[k] / den, (tm, LANES))

    onehot = jnp.where(jnp.logical_or(jnp.logical_or(hots[0], hots[1]), jnp.logical_or(hots[2], hots[3])), 1.0, 0.0)
    r_i = lax.broadcasted_iota(jnp.int32, (tm, tm), 0)
    c_i = lax.broadcasted_iota(jnp.int32, (tm, tm), 1)
    before = _dot(jnp.where(c_i < r_i, 1.0, 0.0).astype(BF16), onehot.astype(BF16)) + carry_sc[...]
    ranks = jnp.zeros((tm, LANES), F32)
    for k in range(TOP_K):
        ranks = jnp.where(lane == k, jnp.sum(jnp.where(hots[k], before, 0.0), axis=1, keepdims=True), ranks)
    rank_ref[...] = ranks[:, :TOP_K].astype(jnp.int32)
    carry_sc[...] = carry_sc[...] + jnp.sum(onehot, axis=0, keepdims=True)
    cnt_ref[...] = carry_sc[...].astype(jnp.int32)


def _mix(x2, a_out, b_out, w_out, g, beta, w_router, b_router, alpha):
    n, d = x2.shape
    tm = MIX_TM
    half = a_out.shape[1]
    row = lambda i: (i, 0)
    const = lambda i: (0, 0)
    wr = jnp.pad(w_router, ((0, 0), (0, LANES - N_EXPERTS)))
    br = jnp.pad(b_router, (0, LANES - N_EXPERTS)).reshape(1, LANES)
    return pl.pallas_call(
        functools.partial(_mix_kernel, alpha=alpha),
        grid=(n // tm,),
        in_specs=[pl.BlockSpec((tm, d), row), pl.BlockSpec((tm, half), row), pl.BlockSpec((tm, half), row),
                  pl.BlockSpec(w_out.shape, const), pl.BlockSpec((1, d), const), pl.BlockSpec((1, d), const),
                  pl.BlockSpec(wr.shape, const), pl.BlockSpec((1, LANES), const)],
        out_specs=[pl.BlockSpec((tm, d // LANES, LANES), lambda i: (i, 0, 0)), pl.BlockSpec((tm, TOP_K), row),
                   pl.BlockSpec((tm, TOP_K, LANES), lambda i: (i, 0, 0)), pl.BlockSpec((tm, TOP_K), row),
                   pl.BlockSpec((1, LANES), const)],
        out_shape=[jax.ShapeDtypeStruct((n, d // LANES, LANES), F32), jax.ShapeDtypeStruct((n, TOP_K), jnp.int32),
                   jax.ShapeDtypeStruct((n, TOP_K, LANES), F32), jax.ShapeDtypeStruct((n, TOP_K), jnp.int32),
                   jax.ShapeDtypeStruct((1, LANES), jnp.int32)],
        scratch_shapes=[pltpu.VMEM((1, LANES), F32)],
        compiler_params=pltpu.CompilerParams(dimension_semantics=("arbitrary",), vmem_limit_bytes=VMEM_LIMIT),
        name="mix",
    )(x2, a_out, b_out, w_out.astype(BF16), g.reshape(1, d), beta.reshape(1, d), wr, br)


def _route(top_i, rank, counts, tm):
    n = top_i.shape[0]
    n_tiles = (n * TOP_K) // tm + N_EXPERTS
    counts = counts[0, :N_EXPERTS]
    tiles_per = (counts + tm - 1) // tm
    tile_end = jnp.cumsum(tiles_per)
    pstart = (tile_end - tiles_per) * tm
    pos = rank + jnp.sum(jnp.where(top_i[..., None] == jnp.arange(N_EXPERTS, dtype=jnp.int32), pstart, 0), axis=-1)
    tile_ids = jnp.arange(n_tiles, dtype=jnp.int32)
    n_valid = tile_end[-1]
    tile_expert = jnp.sum(tile_end[None, :] <= jnp.minimum(tile_ids, n_valid - 1)[:, None], axis=1).astype(jnp.int32)
    n_valid = n_valid[None].astype(jnp.int32)
    pad_start = jnp.concatenate([pstart + counts, n_valid]).astype(jnp.int32)
    return pos.astype(jnp.int32), tile_expert, n_valid, pad_start, n_tiles


def _dispatch_kernel(pad_ref, h_ref, pos_hbm, xs_hbm, pos_s, zbuf, sem_idx, sem_row):
    i = pl.program_id(0)
    tm = h_ref.shape[0]

    def pos_copy(t, slot):
        return pltpu.make_async_copy(pos_hbm.at[t], pos_s.at[pl.ds(slot * (tm * TOP_K), tm * TOP_K)], sem_idx.at[slot])

    @pl.when(i == 0)
    def _():
        zt = zbuf.shape[0]
        zbuf[...] = jnp.zeros(zbuf.shape, zbuf.dtype)
        for e in range(N_EXPERTS):
            cpe = pltpu.make_async_copy(zbuf, xs_hbm.at[pl.ds(pad_ref[e], zt)], sem_row)
            cpe.start()
            cpe.wait()

        def fill(t, _):
            cpz = pltpu.make_async_copy(zbuf, xs_hbm.at[pl.ds(pl.multiple_of(t * zt, zt), zt)], sem_row)
            cpz.start()
            cpz.wait()
            return 0

        lax.fori_loop(pad_ref[N_EXPERTS], xs_hbm.shape[0] // zt, fill, 0)
        pos_copy(0, 0).start()

    slot = i % 2
    pos_copy(i, slot).wait()

    @pl.when(i + 1 < pl.num_programs(0))
    def _():
        pos_copy(i + 1, 1 - slot).start()

    base = slot * (tm * TOP_K)

    def body(r, _):
        for s in range(TOP_K):
            pltpu.make_async_copy(h_ref.at[r], xs_hbm.at[pos_s[base + r * TOP_K + s]], sem_row).start(priority=s % 2)
        return 0

    lax.fori_loop(0, tm, body, 0, unroll=8)
    for s in range(TOP_K):
        pltpu.make_async_copy(h_ref, xs_hbm.at[pl.ds(0, tm)], sem_row).wait()


def _dispatch(h3, pos, pad_start, n_rows, zrows):
    n, nt8, _ = h3.shape
    tm = ROW_TM
    return pl.pallas_call(
        _dispatch_kernel,
        grid_spec=pltpu.PrefetchScalarGridSpec(
            num_scalar_prefetch=1,
            grid=(n // tm,),
            in_specs=[pl.BlockSpec((tm, nt8, LANES), lambda i, pad: (i, 0, 0)), pl.BlockSpec(memory_space=pl.ANY)],
            out_specs=pl.BlockSpec(memory_space=pl.ANY),
            scratch_shapes=[pltpu.SMEM((2 * tm * TOP_K,), jnp.int32), pltpu.VMEM((zrows, nt8, LANES), F32),
                            pltpu.SemaphoreType.DMA((2,)), pltpu.SemaphoreType.DMA]),
        out_shape=jax.ShapeDtypeStruct((n_rows + zrows, nt8, LANES), F32),
        compiler_params=pltpu.CompilerParams(dimension_semantics=("arbitrary",), vmem_limit_bytes=VMEM_LIMIT),
        name="dispatch",
    )(pad_start, h3, pos.reshape(n // tm, tm * TOP_K))


def _moe_kernel(te_ref, nv_ref, xs_hbm, wg_ref, bg_ref, wu_ref, bu_ref, wd_ref, bd_ref, ys_hbm,
                xbuf, ybuf, wg_bf, wu_bf, wd_bf, sem_in, sem_out):
    i = pl.program_id(0)
    n_valid = nv_ref[0]
    tm = ybuf.shape[0]
    nt8 = xs_hbm.shape[1]

    def in_copy(t, slot, c):
        return pltpu.make_async_copy(xs_hbm.at[pl.ds(t * tm, tm), c, :], xbuf.at[slot, :, pl.ds(c * LANES, LANES)],
                                     sem_in.at[slot])

    def out_copy(t, c):
        return pltpu.make_async_copy(ybuf.at[:, pl.ds(c * LANES, LANES)], ys_hbm.at[pl.ds(t * tm, tm), c, :], sem_out)

    def fetch(t, slot):
        for c in range(nt8):
            in_copy(t, slot, c).start()

    def store(t):
        for c in range(nt8):
            out_copy(t, c).start()

    def wait_store(t):
        for c in range(nt8):
            out_copy(t, c).wait()

    @pl.when(i == 0)
    def _():
        fetch(0, 0)

    @pl.when(i < n_valid)
    def _():
        slot = i % 2
        for c in range(nt8):
            in_copy(i, slot, c).wait()

        @pl.when(i + 1 < n_valid)
        def _():
            fetch(i + 1, 1 - slot)

        @pl.when(jnp.logical_or(i == 0, te_ref[i] != te_ref[jnp.maximum(i - 1, 0)]))
        def _():
            wg_bf[...] = wg_ref[0].astype(BF16)
            wu_bf[...] = wu_ref[0].astype(BF16)
            wd_bf[...] = wd_ref[0].astype(BF16)

        x = xbuf[slot].astype(BF16)
        g = jnp.minimum(_dot(x, wg_bf[...]) + bg_ref[0], SWIGLU_LIMIT)
        u = jnp.clip(_dot(x, wu_bf[...]) + bu_ref[0], -SWIGLU_LIMIT, SWIGLU_LIMIT)
        act = g * (1.0 / (1.0 + jnp.exp(-SWIGLU_ALPHA * g))) * (u + 1.0)
        y = _dot(act.astype(BF16), wd_bf[...]) + bd_ref[0]

        @pl.when(i > 0)
        def _():
            wait_store(i - 1)

        ybuf[...] = y
        store(i)

        @pl.when(i == n_valid - 1)
        def _():
            wait_store(i)

    @pl.when(i >= n_valid)
    def _():
        ybuf[...] = jnp.zeros(ybuf.shape, ybuf.dtype)
        store(i)
        wait_store(i)


def _moe(xs, tile_expert, n_valid, n_tiles, w_gate, b_gate, w_up, b_up, w_down, b_down):
    _, nt8, _ = xs.shape
    tm = MOE_TM
    ne, d, dff = w_gate.shape
    wmap = lambda i, te, nv: (te[i], 0, 0)
    any_spec = pl.BlockSpec(memory_space=pl.ANY)
    grid_spec = pltpu.PrefetchScalarGridSpec(
        num_scalar_prefetch=2,
        grid=(n_tiles,),
        in_specs=[any_spec,
                  pl.BlockSpec((1, d, dff), wmap), pl.BlockSpec((1, 1, dff), wmap),
                  pl.BlockSpec((1, d, dff), wmap), pl.BlockSpec((1, 1, dff), wmap),
                  pl.BlockSpec((1, dff, d), wmap), pl.BlockSpec((1, 1, d), wmap)],
        out_specs=any_spec,
        scratch_shapes=[pltpu.VMEM((2, tm, d), F32), pltpu.VMEM((tm, d), F32),
                        pltpu.VMEM((d, dff), BF16), pltpu.VMEM((d, dff), BF16), pltpu.VMEM((dff, d), BF16),
                        pltpu.SemaphoreType.DMA((2,)), pltpu.SemaphoreType.DMA])
    return pl.pallas_call(
        _moe_kernel,
        grid_spec=grid_spec,
        out_shape=jax.ShapeDtypeStruct((n_tiles * tm, nt8, LANES), F32),
        compiler_params=pltpu.CompilerParams(dimension_semantics=("arbitrary",), vmem_limit_bytes=VMEM_LIMIT),
        name="moe",
    )(tile_expert, n_valid, xs,
      w_gate, b_gate.reshape(ne, 1, dff), w_up, b_up.reshape(ne, 1, dff), w_down, b_down.reshape(ne, 1, d))


def _combine_kernel(h_ref, gate_ref, pos_hbm, ys_hbm, g_ref, beta_ref, o_ref, pos_s, ybuf, sem_idx, sem_row, *, alpha):
    i = pl.program_id(0)
    n = pl.num_programs(0)
    tm = h_ref.shape[0]
    slot = i % 2

    def pos_copy(t, sl):
        return pltpu.make_async_copy(pos_hbm.at[t], pos_s.at[pl.ds(sl * (tm * TOP_K), tm * TOP_K)], sem_idx.at[sl])

    def gather(sl):
        base = sl * (tm * TOP_K)

        def body(r, _):
            for s in range(TOP_K):
                pltpu.make_async_copy(ys_hbm.at[pos_s[base + r * TOP_K + s]], ybuf.at[sl, s, r],
                                      sem_row.at[sl]).start(priority=s % 2)
            return 0

        lax.fori_loop(0, tm, body, 0, unroll=8)

    @pl.when(i == 0)
    def _():
        first = pos_copy(0, 0)
        first.start()
        first.wait()
        gather(0)

        @pl.when(n > 1)
        def _():
            pos_copy(1, 1).start()

    @pl.when(i + 1 < n)
    def _():
        pos_copy(i + 1, 1 - slot).wait()
        gather(1 - slot)

        @pl.when(i + 2 < n)
        def _():
            pos_copy(i + 2, slot).start()

    for s in range(TOP_K):
        pltpu.make_async_copy(ys_hbm.at[pl.ds(0, tm)], ybuf.at[slot, s], sem_row.at[slot]).wait()

    z = alpha * h_ref[...]
    for s in range(TOP_K):
        z = z + gate_ref[:, s:s + 1, :] * ybuf[slot, s]
    inv_d = 1.0 / (z.shape[1] * z.shape[2])
    mu = jnp.sum(z, axis=(1, 2), keepdims=True) * inv_d
    dz = z - mu
    var = jnp.sum(dz * dz, axis=(1, 2), keepdims=True) * inv_d
    o_ref[...] = dz * lax.rsqrt(var + LN_EPS) * g_ref[...] + beta_ref[...]


def _combine(h3, gates, pos, ys, g, beta, alpha):
    n, nt8, _ = h3.shape
    tm = ROW_TM
    tile = lambda i: (i, 0, 0)
    const = lambda i: (0, 0, 0)
    return pl.pallas_call(
        functools.partial(_combine_kernel, alpha=alpha),
        grid=(n // tm,),
        in_specs=[pl.BlockSpec((tm, nt8, LANES), tile), pl.BlockSpec((tm, TOP_K, LANES), tile),
                  pl.BlockSpec(memory_space=pl.ANY), pl.BlockSpec(memory_space=pl.ANY),
                  pl.BlockSpec((1, nt8, LANES), const), pl.BlockSpec((1, nt8, LANES), const)],
        out_specs=pl.BlockSpec((tm, nt8, LANES), tile),
        out_shape=jax.ShapeDtypeStruct((n, nt8, LANES), F32),
        scratch_shapes=[pltpu.SMEM((2 * tm * TOP_K,), jnp.int32), pltpu.VMEM((2, TOP_K, tm, nt8, LANES), F32),
                        pltpu.SemaphoreType.DMA((2,)), pltpu.SemaphoreType.DMA((2,))],
        compiler_params=pltpu.CompilerParams(dimension_semantics=("arbitrary",), vmem_limit_bytes=VMEM_LIMIT),
        name="combine",
    )(h3, gates, pos.reshape(n // tm, tm * TOP_K), ys, g.reshape(1, nt8, LANES), beta.reshape(1, nt8, LANES))


def _layer(x, tabs, alpha, w_in, ik_g, ik_b, q_g, w_q_up, kv_g, w_kv_up, w_out, ln1_g, ln1_b,
           w_router, b_router, w_gate, b_gate, w_up, b_up, w_down, b_down, ln2_g, ln2_b):
    bsz, seq, d = x.shape
    w_t, w_n, wq_t, bd_uk_t, w_uv_t = _pack_weights(w_in, w_q_up, w_kv_up)
    aqt, iqt, iwt, akv, avt, ikp, qabst, ckv, latt = _proj(x, tabs, w_t, w_n, wq_t, bd_uk_t, ik_g, ik_b, q_g, kv_g)
    a_out = _dsa(aqt, iqt, iwt, akv, avt, ikp)
    b_out = _mla(qabst, ckv, latt, w_uv_t)
    n = bsz * seq
    h3, top_i, gates, rank, counts = _mix(x.reshape(n, d), a_out.reshape(n, -1), b_out.reshape(n, -1), w_out,
                                          ln1_g, ln1_b, w_router, b_router, alpha)
    pos, tile_expert, n_valid, pad_start, n_tiles = _route(top_i, rank, counts, MOE_TM)
    xs = _dispatch(h3, pos, pad_start, n_tiles * MOE_TM, MOE_TM)
    ys = _moe(xs, tile_expert, n_valid, n_tiles, w_gate, b_gate, w_up, b_up, w_down, b_down)
    return _combine(h3, gates, pos, ys, ln2_g, ln2_b, alpha).reshape(bsz, seq, d)


def kernel(x, positions, w_mix_in, idx_k_norm_g, idx_k_norm_b, mla_q_norm_g, mla_w_q_up, mla_kv_norm_g, mla_w_kv_up, w_mix_out, ln1_g, ln1_b, w_router, b_router, w_gate, b_gate, w_up, b_up, w_down, b_down, ln2_g, ln2_b):
    depth = w_mix_in.shape[0]
    alpha = float((2 * depth) ** 0.25)
    tabs = _rope_tables(positions)
    for l in range(depth):
        x = _layer(x, tabs, alpha, w_mix_in[l], idx_k_norm_g[l], idx_k_norm_b[l], mla_q_norm_g[l], mla_w_q_up[l],
                   mla_kv_norm_g[l], mla_w_kv_up[l], w_mix_out[l], ln1_g[l], ln1_b[l], w_router[l], b_router[l],
                   w_gate[l], b_gate[l], w_up[l], b_up[l], w_down[l], b_down[l], ln2_g[l], ln2_b[l])
    return x
```

```python
import functools

import jax
import jax.numpy as jnp
import numpy as np
from jax import lax
from jax.experimental import pallas as pl
from jax.experimental.pallas import tpu as pltpu

F32 = jnp.float32
BF16 = jnp.bfloat16

A_HEADS, A_HEAD_DIM = 8, 64
IDX_HEADS, IDX_DIM = 8, 64
TOPK_MAX = 256
B_HEADS, B_NOPE, B_ROPE, B_V = 8, 64, 32, 64
Q_LORA, KV_LORA = 256, 128
N_EXPERTS, TOP_K = 32, 4
SWIGLU_LIMIT, SWIGLU_ALPHA = 7.0, 1.702
ROPE_THETA = 10000.0
LN_EPS, RMS_EPS = 1e-5, 1e-6

LANES = 128
SUBLANES = 8
VMEM_LIMIT = 56 * 1024 * 1024

PROJ_TM = 512
ATT_TQ = LANES
ATT_TK = 1024
SCORE_TK = 1024
SEL_TK = 512
ONES_ROWS = 16
LOG2_E = 1.4426950408889634
MIX_TM = 512
MOE_TM = 256
ROW_TM = 256
SEL_MAX_ITERS = 400
SEL_UNROLL = 4

MASKED = -1e30
M_FLOOR = -1e29
TINY = float(np.finfo(np.float32).tiny)


def _dot(a, b):
    return jnp.dot(a, b, preferred_element_type=F32)


def _dot_t(a, b):
    return lax.dot_general(a, b, (((1,), (1,)), ((), ())), preferred_element_type=F32)


def _lane_iota(shape):
    return lax.broadcasted_iota(jnp.int32, shape, len(shape) - 1)


def _fold(v, op, rows=64):
    n = v.shape[0] // rows
    acc = v[0:rows]
    for i in range(1, n):
        acc = op(acc, v[i * rows:(i + 1) * rows])
    return acc


def _rows_reduce(v, op):
    if op == "sum":
        return jnp.sum(v, axis=0, keepdims=True)
    if op == "max":
        return jnp.max(v, axis=0, keepdims=True)
    return jnp.min(v, axis=0, keepdims=True)


_R_AQ, _R_AQR, _R_IQ, _R_IQR, _R_V, _R_IW, _R_QC, _R_KVC, _R_TOTAL = 0, 512, 1024, 1536, 2048, 2112, 2128, 2384, 2512
_C_KV, _C_KVR, _C_IK, _C_KVC, _C_KR, _C_KRR, _C_TOTAL = 0, 128, 256, 384, 512, 640, 768


def _rot_cols(w, dim):
    k, n = w.shape
    w3 = w.reshape(k, n // dim, dim)
    return jnp.concatenate([-w3[..., dim // 2:], w3[..., :dim // 2]], axis=-1).reshape(k, n)


def _pad_cols(w, width):
    return jnp.pad(w, ((0, 0), (0, width - w.shape[1])))


def _pack_weights(w_in, w_q_up, w_kv_up):
    o = np.cumsum((0, 512, 64, 64, 512, 64, 8, 256, 128, 32))
    a_q, a_k, a_v, i_q, i_k, i_w, b_qc, b_kvc, b_kr = [w_in[:, o[i]:o[i + 1]] for i in range(9)]
    w_t = jnp.concatenate([a_q, _rot_cols(a_q, 64), i_q, _rot_cols(i_q, 64), a_v, _pad_cols(i_w, 16),
                           b_qc, b_kvc], axis=1).T
    assert w_t.shape[0] == _R_TOTAL
    w_n = jnp.concatenate([a_k, a_v, _rot_cols(a_k, 64), jnp.zeros_like(a_k), _pad_cols(i_k, 128), b_kvc,
                           _pad_cols(b_kr, 128), _pad_cols(_rot_cols(b_kr, 32), 128)], axis=1)
    assert w_n.shape[1] == _C_TOTAL
    wq = w_q_up.reshape(Q_LORA, B_HEADS, B_NOPE + B_ROPE)
    wq_rope = wq[..., B_NOPE:].reshape(Q_LORA, B_HEADS * B_ROPE)
    wq_t = jnp.concatenate([wq[..., :B_NOPE].reshape(Q_LORA, B_HEADS * B_NOPE), wq_rope,
                            _rot_cols(wq_rope, B_ROPE)], axis=1).T
    wkv = w_kv_up.reshape(KV_LORA, B_HEADS, B_NOPE + B_V)
    eye = jnp.eye(B_HEADS, dtype=w_kv_up.dtype)
    bd_uk_t = jnp.einsum('lhd,hg->glhd', wkv[..., :B_NOPE], eye).reshape(B_HEADS * KV_LORA, B_HEADS * B_NOPE)
    w_uv_t = jnp.transpose(wkv[..., B_NOPE:], (1, 2, 0))
    return w_t.astype(BF16), w_n.astype(BF16), wq_t.astype(BF16), bd_uk_t.astype(BF16), w_uv_t.astype(BF16)


def _rope_tables(positions):
    pos = positions.astype(F32)[..., None]
    inv64 = ROPE_THETA ** (-jnp.arange(0, 64, 2, dtype=F32) / 64)
    inv32 = ROPE_THETA ** (-jnp.arange(0, 32, 2, dtype=F32) / 32)
    c64, s64 = jnp.cos(pos * inv64), jnp.sin(pos * inv64)
    c32, s32 = jnp.cos(pos * inv32), jnp.sin(pos * inv32)
    nat = (jnp.tile(c64, 4), jnp.tile(s64, 4), jnp.tile(c32, 8), jnp.tile(s32, 8))
    tr = tuple(jnp.swapaxes(jnp.tile(t, 2), 1, 2) for t in (c64, s64, c32, s32))
    return nat + tr


def _proj_kernel(x_ref, ca_ref, sa_ref, cr_ref, sr_ref, cat_ref, sat_ref, crt_ref, srt_ref,
                 wt_ref, wn_ref, wqt_ref, bdukt_ref, ikg_ref, ikb_ref, qg_ref, kvg_ref, kvgt_ref,
                 aqt_ref, iqt_ref, iwt_ref, akv_ref, avt_ref, ik_ref, qabst_ref, ckv_ref, latt_ref):
    tm = x_ref.shape[0]
    nqb = tm // ATT_TQ
    xb = x_ref[...].astype(BF16)
    ca, sa, cr, sr = ca_ref[...], sa_ref[...], cr_ref[...], sr_ref[...]
    lane = _lane_iota((tm, LANES))
    low64 = lane < 64

    def proj_t(r0, rows):
        return _dot_t(wt_ref[r0:r0 + rows, :], xb)

    def proj_n(c0, width):
        return _dot(xb, wn_ref[:, c0:c0 + width])

    cat8 = jnp.concatenate([cat_ref[...]] * 8, axis=0)
    sat8 = jnp.concatenate([sat_ref[...]] * 8, axis=0)

    def split_heads(v, out_ref):
        vb = v.astype(BF16)
        zeros = jnp.zeros((64, ATT_TQ), BF16)
        for c in range(nqb):
            for h in range(8):
                out_ref[c, 0:64, h * ATT_TQ:(h + 1) * ATT_TQ] = vb[64 * h:64 * h + 64, c * ATT_TQ:(c + 1) * ATT_TQ]
                out_ref[c, 64:128, h * ATT_TQ:(h + 1) * ATT_TQ] = zeros

    aq_t = (proj_t(_R_AQ, 512) * cat8 + proj_t(_R_AQR, 512) * sat8) * (A_HEAD_DIM ** -0.5 * LOG2_E)
    split_heads(aq_t, aqt_ref)
    iq_t = proj_t(_R_IQ, 512) * cat8 + proj_t(_R_IQR, 512) * sat8
    split_heads(iq_t, iqt_ref)
    iw_t = proj_t(_R_IW, 16)[0:IDX_HEADS] * ((IDX_HEADS * IDX_DIM) ** -0.5)
    for c in range(nqb):
        iwt_ref[c] = iw_t[:, c * ATT_TQ:(c + 1) * ATT_TQ]
    ones_rows = jnp.where(lax.broadcasted_iota(jnp.int32, (ONES_ROWS, tm), 0) == 0, 1.0, 0.0).astype(BF16)
    avt_ref[...] = jnp.concatenate([proj_t(_R_V, 64).astype(BF16), ones_rows], axis=0)

    kc = jnp.where(low64, ca, 1.0)
    ks = jnp.where(low64, sa, 0.0)
    akv_ref[...] = (proj_n(_C_KV, 128) * kc + proj_n(_C_KVR, 128) * ks).astype(BF16)

    y = proj_n(_C_IK, 128)
    mu = jnp.sum(y, axis=1, keepdims=True) * (1.0 / IDX_DIM)
    d = jnp.where(low64, y - mu, 0.0)
    var = jnp.sum(d * d, axis=1, keepdims=True) * (1.0 / IDX_DIM)
    yn = d * lax.rsqrt(var + LN_EPS) * ikg_ref[...] + ikb_ref[...]
    rot = jnp.where(lane < 32, -pltpu.roll(yn, LANES - 32, 1), pltpu.roll(yn, 32, 1))
    ik_ref[...] = jnp.where(low64, yn * ca + rot * sa, 0.0).astype(BF16)

    qc_t = proj_t(_R_QC, Q_LORA)
    qn_t = (qc_t * lax.rsqrt(jnp.mean(qc_t * qc_t, axis=0, keepdims=True) + RMS_EPS) * qg_ref[...]).astype(BF16)
    q_t = _dot(wqt_ref[...], qn_t)
    scale = (B_NOPE + B_ROPE) ** -0.5 * LOG2_E
    q_lat_t = (_dot(bdukt_ref[...], q_t[0:512].astype(BF16)) * scale).astype(BF16)
    crt8 = jnp.concatenate([crt_ref[...]] * 8, axis=0)
    srt8 = jnp.concatenate([srt_ref[...]] * 8, axis=0)
    q_r_t = ((q_t[512:768] * crt8 + q_t[768:1024] * srt8) * scale).astype(BF16)
    zeros = jnp.zeros((2 * LANES - KV_LORA - B_ROPE, ATT_TQ), BF16)
    for c in range(nqb):
        tok = slice(c * ATT_TQ, (c + 1) * ATT_TQ)
        for h in range(B_HEADS):
            col = slice(h * ATT_TQ, (h + 1) * ATT_TQ)
            qabst_ref[c, 0:KV_LORA, col] = q_lat_t[h * KV_LORA:(h + 1) * KV_LORA, tok]
            qabst_ref[c, KV_LORA:KV_LORA + B_ROPE, col] = q_r_t[h * B_ROPE:(h + 1) * B_ROPE, tok]
            qabst_ref[c, KV_LORA + B_ROPE:2 * LANES, col] = zeros

    kvc = proj_n(_C_KVC, KV_LORA)
    kvn = kvc * lax.rsqrt(jnp.mean(kvc * kvc, axis=1, keepdims=True) + RMS_EPS) * kvg_ref[...]
    kr = proj_n(_C_KR, 128) * cr + proj_n(_C_KRR, 128) * sr
    ckv_ref[...] = jnp.concatenate([kvn, kr], axis=1).astype(BF16)
    kvc_t = proj_t(_R_KVC, KV_LORA)
    lat_t = (kvc_t * lax.rsqrt(jnp.mean(kvc_t * kvc_t, axis=0, keepdims=True) + RMS_EPS) * kvgt_ref[...]).astype(BF16)
    latt_ref[...] = jnp.concatenate([lat_t, ones_rows], axis=0)


def _proj(x, tabs, w_t, w_n, wq_t, bd_uk_t, ik_g, ik_b, q_g, kv_g):
    bsz, seq, d = x.shape
    tm = PROJ_TM
    nqb = tm // ATT_TQ
    tok = lambda b, i: (b, i, 0)
    feat = lambda b, i: (b, 0, i)
    qblk = lambda b, i: (b, i, 0, 0)
    const2 = lambda b, i: (0, 0)
    nat_spec = pl.BlockSpec((None, tm, LANES), tok)
    col = lambda v: jnp.broadcast_to(v.reshape(-1, 1), (v.shape[0], tm))
    ik_g = jnp.pad(ik_g, (0, LANES - IDX_DIM)).reshape(1, LANES)
    ik_b = jnp.pad(ik_b, (0, LANES - IDX_DIM)).reshape(1, LANES)
    consts = (w_t, w_n, wq_t, bd_uk_t, ik_g, ik_b, col(q_g), kv_g.reshape(1, KV_LORA), col(kv_g))
    return pl.pallas_call(
        _proj_kernel,
        grid=(bsz, seq // tm),
        in_specs=[pl.BlockSpec((None, tm, d), tok), nat_spec, nat_spec, nat_spec, nat_spec,
                  pl.BlockSpec((None, 64, tm), feat), pl.BlockSpec((None, 64, tm), feat),
                  pl.BlockSpec((None, 32, tm), feat), pl.BlockSpec((None, 32, tm), feat)]
                 + [pl.BlockSpec(c.shape, const2) for c in consts],
        out_specs=[pl.BlockSpec((None, nqb, LANES, A_HEADS * ATT_TQ), qblk),
                   pl.BlockSpec((None, nqb, LANES, IDX_HEADS * ATT_TQ), qblk),
                   pl.BlockSpec((None, nqb, IDX_HEADS, ATT_TQ), qblk),
                   pl.BlockSpec((None, tm, LANES), tok),
                   pl.BlockSpec((None, A_HEAD_DIM + ONES_ROWS, tm), feat),
                   pl.BlockSpec((None, tm, LANES), tok),
                   pl.BlockSpec((None, nqb, 2 * LANES, B_HEADS * ATT_TQ), qblk),
                   pl.BlockSpec((None, tm, 2 * LANES), tok),
                   pl.BlockSpec((None, KV_LORA + ONES_ROWS, tm), feat)],
        out_shape=[jax.ShapeDtypeStruct((bsz, seq // ATT_TQ, LANES, A_HEADS * ATT_TQ), BF16),
                   jax.ShapeDtypeStruct((bsz, seq // ATT_TQ, LANES, IDX_HEADS * ATT_TQ), BF16),
                   jax.ShapeDtypeStruct((bsz, seq // ATT_TQ, IDX_HEADS, ATT_TQ), F32),
                   jax.ShapeDtypeStruct((bsz, seq, LANES), BF16),
                   jax.ShapeDtypeStruct((bsz, A_HEAD_DIM + ONES_ROWS, seq), BF16),
                   jax.ShapeDtypeStruct((bsz, seq, LANES), BF16),
                   jax.ShapeDtypeStruct((bsz, seq // ATT_TQ, 2 * LANES, B_HEADS * ATT_TQ), BF16),
                   jax.ShapeDtypeStruct((bsz, seq, 2 * LANES), BF16),
                   jax.ShapeDtypeStruct((bsz, KV_LORA + ONES_ROWS, seq), BF16)],
        compiler_params=pltpu.CompilerParams(
            dimension_semantics=("arbitrary", "arbitrary"), vmem_limit_bytes=VMEM_LIMIT),
        name="proj",
    )(x, *tabs, *consts)


def _attend_init(nh, val_rows):
    return ([jnp.full((1, LANES), M_FLOOR, F32) for _ in range(nh)],
            jnp.zeros((val_rows + ONES_ROWS, nh * LANES), F32))


def _attend_update(lg_ref, bias, vt_blk, state):
    ms, acc = state
    new_ms, ps, alphas = [], [], []
    for h in range(len(ms)):
        lh = lg_ref[:, h * LANES:(h + 1) * LANES]
        if bias is not None:
            lh = lh + bias
        m_new = jnp.maximum(ms[h], _rows_reduce(_fold(lh, jnp.maximum), "max"))
        alphas.append(jnp.exp2(ms[h] - m_new))
        ps.append(jnp.exp2(lh - m_new).astype(BF16))
        new_ms.append(m_new)
    return new_ms, jnp.concatenate(alphas, axis=1) * acc + _dot(vt_blk, jnp.concatenate(ps, axis=1))


def _attend_pipelined(n_blocks, scores, update, lg_a, lg_b, state):
    n_look = (n_blocks - 1) // 2

    def pair(i, st, lookahead):
        kb0 = 2 * i
        lg_b[...] = scores(kb0 + 1)
        st = update(kb0, lg_a, not lookahead, st)
        if lookahead:
            lg_a[...] = scores(kb0 + 2)
        return update(kb0 + 1, lg_b, not lookahead, st)

    lg_a[...] = scores(0)
    state = lax.fori_loop(0, n_look, lambda i, st: pair(i, st, True), state)
    return lax.cond(n_blocks % 2 == 0,
                    lambda st: pair(n_look, st, False),
                    lambda st: update(2 * n_look, lg_a, True, st), state)


def _attend_finish(state, val_rows):
    acc = state[1]
    return acc[0:val_rows] / acc[val_rows:val_rows + 1]


def _dsa_kernel(aqt_ref, iqt_ref, iwt_ref, akv_ref, avt_ref, ik_ref, out_ref, s_sc, lg_a, lg_b):
    nh = iwt_ref.shape[0]
    tq, tk = ATT_TQ, ATT_TK
    seq = akv_ref.shape[0]
    j = pl.program_id(1)
    nkb = (j * tq) // tk + 1
    iqt = iqt_ref[...]
    w = iwt_ref[...]

    ts = SCORE_TK
    krow = lax.broadcasted_iota(jnp.int32, (ts, tq), 0)
    qcol = j * tq + lax.broadcasted_iota(jnp.int32, (ts, tq), 1)

    def score_body(kb, carry):
        cmax, cmin = carry
        k0 = pl.multiple_of(kb * ts, ts)
        y = _dot(ik_ref[pl.ds(k0, ts), :], iqt)
        s = jnp.zeros((ts, tq), F32)
        for h in range(nh):
            s = s + w[h:h + 1, :] * jnp.maximum(y[:, h * tq:(h + 1) * tq], 0.0)
        causal = (k0 + krow) <= qcol
        s_lo = jnp.where(causal, s, -jnp.inf)
        s_sc[pl.ds(k0, ts), :] = s_lo
        cmax = jnp.maximum(cmax, _fold(s_lo, jnp.maximum))
        cmin = jnp.minimum(cmin, _fold(jnp.where(causal, s, jnp.inf), jnp.minimum))
        return cmax, cmin

    cmax, cmin = lax.fori_loop(0, (j * tq) // ts + 1, score_body,
                               (jnp.full((64, tq), -jnp.inf, F32), jnp.full((64, tq), jnp.inf, F32)))
    rmax = _rows_reduce(cmax, "max")
    rmin = _rows_reduce(cmin, "min")

    n_causal = (j * tq + lax.broadcasted_iota(jnp.int32, (1, tq), 1) + 1).astype(F32)
    kt = jnp.minimum(n_causal, float(TOPK_MAX))

    tc = SEL_TK
    ncb = (j * tq) // tc + 1
    crow_f = lax.broadcasted_iota(jnp.int32, (tc, tq), 0).astype(F32)

    def count(pred):
        def body(cb, acc):
            k0 = pl.multiple_of(cb * tc, tc)
            return acc + _fold(jnp.where(pred(s_sc[pl.ds(k0, tc), :], k0), 1.0, 0.0), jnp.add)
        return _rows_reduce(lax.fori_loop(0, ncb, body, jnp.zeros((64, tq), F32)), "sum")

    def propose(it, lo, hi, clo):
        mid = 0.5 * lo + 0.5 * hi
        mid = jnp.where(jnp.logical_and(it == 0, jnp.logical_and(lo < 0.0, hi > 0.0)), 0.0, mid)
        mid = jnp.where(jnp.logical_and(it == 1, jnp.logical_and(lo == 0.0, hi > TINY)), TINY, mid)
        active = jnp.logical_and(clo != kt, jnp.logical_and(mid > lo, mid < hi))
        return mid, active

    def sel_step(it, lo, hi, clo, chi):
        mid, active = propose(it, lo, hi, clo)
        cm = count(lambda blk, k0: blk >= mid)
        ge = cm >= kt
        up = jnp.logical_and(active, ge)
        dn = jnp.logical_and(active, jnp.logical_not(ge))
        return jnp.where(up, mid, lo), jnp.where(dn, mid, hi), jnp.where(up, cm, clo), jnp.where(dn, cm, chi)

    def sel_body(c):
        it, _, lo, hi, clo, chi = c
        for u in range(SEL_UNROLL):
            lo, hi, clo, chi = sel_step(it + u, lo, hi, clo, chi)
        _, active = propose(it + SEL_UNROLL, lo, hi, clo)
        go = jnp.logical_and(jnp.max(jnp.where(active, 1.0, 0.0)) > 0.0, it + SEL_UNROLL < SEL_MAX_ITERS)
        return it + SEL_UNROLL, go, lo, hi, clo, chi

    lo0 = rmin
    hi0 = rmax + jnp.abs(rmax) + 1e-30
    clo0, chi0 = n_causal, jnp.zeros((1, tq), F32)
    _, active0 = propose(0, lo0, hi0, clo0)
    go0 = jnp.max(jnp.where(active0, 1.0, 0.0)) > 0.0
    _, _, lo, hi, clo, chi = lax.while_loop(lambda c: c[1], sel_body, (jnp.int32(0), go0, lo0, hi0, clo0, chi0))

    tie = clo > kt

    @pl.when(jnp.max(jnp.where(tie, 1.0, 0.0)) > 0.0)
    def _():
        need = kt - chi

        def idx_body(_, c):
            jl, jh = c
            jm = jnp.floor((jl + jh) * 0.5)
            ok = count(lambda blk, k0: jnp.logical_and(blk == lo, crow_f + k0.astype(F32) <= jm)) >= need
            return jnp.where(ok, jl, jm), jnp.where(ok, jm, jh)

        n_steps = int(np.ceil(np.log2(seq))) + 1
        _, jh = lax.fori_loop(0, n_steps, idx_body,
                              (jnp.full((1, tq), -1.0, F32), jnp.full((1, tq), float(seq - 1), F32)))
        jh = jnp.where(tie, jh, float(seq))

        def fix_body(cb, _):
            k0 = pl.multiple_of(cb * tc, tc)
            blk = s_sc[pl.ds(k0, tc), :]
            drop = jnp.logical_and(blk == lo, crow_f + k0.astype(F32) > jh)
            s_sc[pl.ds(k0, tc), :] = jnp.where(drop, -jnp.inf, blk)
            return 0

        lax.fori_loop(0, ncb, fix_body, 0)

    def att_scores(kb):
        return _dot(akv_ref[pl.ds(pl.multiple_of(kb * tk, tk), tk), :], aqt_ref[...])

    def att_update(kb, lg_ref, tail, state):
        k0 = pl.multiple_of(kb * tk, tk)
        bias = jnp.where(s_sc[pl.ds(k0, tk), :] >= lo, 0.0, MASKED)
        return _attend_update(lg_ref, bias, avt_ref[:, pl.ds(k0, tk)], state)

    state = _attend_pipelined(nkb, att_scores, att_update, lg_a, lg_b, _attend_init(nh, A_HEAD_DIM))
    o_t = _attend_finish(state, A_HEAD_DIM)
    for p in range(nh // 2):
        pair = jnp.concatenate([o_t[:, (2 * p) * tq:(2 * p + 1) * tq], o_t[:, (2 * p + 1) * tq:(2 * p + 2) * tq]], axis=0)
        out_ref[:, p * LANES:(p + 1) * LANES] = pair.T.astype(out_ref.dtype)


def _dsa(aqt, iqt, iwt, akv, avt, ikp):
    bsz, nqb, _, hq = aqt.shape
    nh = iwt.shape[2]
    seq = akv.shape[1]
    tq = ATT_TQ
    qblk = lambda b, j: (b, j, 0, 0)
    full = lambda b, j: (b, 0, 0)
    return pl.pallas_call(
        _dsa_kernel,
        grid=(bsz, nqb),
        in_specs=[pl.BlockSpec((None, None, LANES, hq), qblk), pl.BlockSpec((None, None, LANES, hq), qblk),
                  pl.BlockSpec((None, None, nh, tq), qblk),
                  pl.BlockSpec((None, seq, LANES), full), pl.BlockSpec((None, A_HEAD_DIM + ONES_ROWS, seq), full),
                  pl.BlockSpec((None, seq, LANES), full)],
        out_specs=pl.BlockSpec((None, tq, nh * A_HEAD_DIM), lambda b, j: (b, j, 0)),
        out_shape=jax.ShapeDtypeStruct((bsz, seq, nh * A_HEAD_DIM), BF16),
        scratch_shapes=[pltpu.VMEM((seq, tq), F32), pltpu.VMEM((ATT_TK, hq), F32), pltpu.VMEM((ATT_TK, hq), F32)],
        compiler_params=pltpu.CompilerParams(
            dimension_semantics=("arbitrary", "arbitrary"), vmem_limit_bytes=VMEM_LIMIT),
        name="dsa",
    )(aqt, iqt, iwt, akv, avt, ikp)


def _mla_kernel(qt_ref, ckv_ref, latt_ref, wuvt_ref, out_ref, lg_a, lg_b):
    nh = wuvt_ref.shape[0]
    tq, tk = ATT_TQ, ATT_TK
    j = pl.program_id(1)
    krow = lax.broadcasted_iota(jnp.int32, (tk, tq), 0)
    qcol = j * tq + lax.broadcasted_iota(jnp.int32, (1, tq), 1)

    def scores(kb):
        return _dot(ckv_ref[pl.ds(pl.multiple_of(kb * tk, tk), tk), :], qt_ref[...])

    def update(kb, lg_ref, tail, state):
        k0 = pl.multiple_of(kb * tk, tk)
        bias = None
        if tail:
            bias = jnp.where((k0 + krow) <= qcol, 0.0, MASKED)
        return _attend_update(lg_ref, bias, latt_ref[:, pl.ds(k0, tk)], state)

    state = _attend_pipelined((j * tq) // tk + 1, scores, update, lg_a, lg_b, _attend_init(nh, KV_LORA))
    o_lat_t = _attend_finish(state, KV_LORA).astype(BF16)
    for p in range(nh // 2):
        pair = jnp.concatenate([_dot(wuvt_ref[h], o_lat_t[:, h * tq:(h + 1) * tq]) for h in (2 * p, 2 * p + 1)], axis=0)
        out_ref[:, p * LANES:(p + 1) * LANES] = pair.T.astype(out_ref.dtype)


def _mla(qabst, ckv, latt, w_uv_t):
    bsz, nqb, dq, hq = qabst.shape
    nh = w_uv_t.shape[0]
    seq = ckv.shape[1]
    tq = ATT_TQ
    return pl.pallas_call(
        _mla_kernel,
        grid=(bsz, nqb),
        in_specs=[pl.BlockSpec((None, None, dq, hq), lambda b, j: (b, j, 0, 0)),
                  pl.BlockSpec((None, seq, dq), lambda b, j: (b, 0, 0)),
                  pl.BlockSpec((None, KV_LORA + ONES_ROWS, seq), lambda b, j: (b, 0, 0)),
                  pl.BlockSpec(w_uv_t.shape, lambda b, j: (0, 0, 0))],
        out_specs=pl.BlockSpec((None, tq, nh * B_V), lambda b, j: (b, j, 0)),
        out_shape=jax.ShapeDtypeStruct((bsz, seq, nh * B_V), BF16),
        scratch_shapes=[pltpu.VMEM((ATT_TK, hq), F32), pltpu.VMEM((ATT_TK, hq), F32)],
        compiler_params=pltpu.CompilerParams(
            dimension_semantics=("arbitrary", "arbitrary"), vmem_limit_bytes=VMEM_LIMIT),
        name="mla",
    )(qabst, ckv, latt, w_uv_t)


def _layer_norm(z, g, b):
    mu = jnp.mean(z, axis=1, keepdims=True)
    d = z - mu
    var = jnp.mean(d * d, axis=1, keepdims=True)
    return d * lax.rsqrt(var + LN_EPS) * g + b


def _split_bf16(v):
    hi = v.astype(BF16)
    return hi, (v - hi.astype(F32)).astype(BF16)


def _to_row_tiles(v, ref):
    for c in range(v.shape[1] // LANES):
        ref[:, c, :] = v[:, c * LANES:(c + 1) * LANES]


def _from_row_tiles(ref):
    return jnp.concatenate([ref[:, c, :] for c in range(ref.shape[1])], axis=1)


def _mix_kernel(x_ref, a_ref, b_ref, wout_ref, g_ref, beta_ref, wr_ref, br_ref,
                h_ref, ti_ref, tg_ref, rank_ref, cnt_ref, carry_sc, *, alpha):
    tm = x_ref.shape[0]
    half = a_ref.shape[1]
    i = pl.program_id(0)

    @pl.when(i == 0)
    def _():
        carry_sc[...] = jnp.zeros(carry_sc.shape, F32)

    mix = _dot(a_ref[...], wout_ref[0:half, :]) + _dot(b_ref[...], wout_ref[half:2 * half, :])
    h = _layer_norm(alpha * x_ref[...] + mix, g_ref[...], beta_ref[...])
    _to_row_tiles(h, h_ref)
    h_hi, h_lo = _split_bf16(h)
    w_hi, w_lo = _split_bf16(wr_ref[...])
    logits = _dot(h_hi, w_hi) + (_dot(h_lo, w_hi) + _dot(h_hi, w_lo)) + br_ref[...]
    lane = _lane_iota((tm, LANES))
    logits = jnp.where(lane < N_EXPERTS, logits, -jnp.inf)
    ids = jnp.zeros((tm, LANES), jnp.int32)
    vals, hots = [], []
    for k in range(TOP_K):
        m = jnp.max(logits, axis=1, keepdims=True)
        idx = jnp.min(jnp.where(logits == m, lane, LANES), axis=1, keepdims=True)
        hot = lane == idx
        ids = jnp.where(lane == k, idx, ids)
        logits = jnp.where(hot, -jnp.inf, logits)
        vals.append(m)
        hots.append(hot)
    es = [jnp.exp(v - vals[0]) for v in vals]
    den = (es[0] + es[1]) + (es[2] + es[3])
    ti_ref[...] = ids[:, :TOP_K]
    for k in range(TOP_K):
        tg_ref[:, k, :] = jnp.broadcast_to(es[k] / den, (tm, LANES))

    onehot = jnp.where(jnp.logical_or(jnp.logical_or(hots[0], hots[1]), jnp.logical_or(hots[2], hots[3])), 1.0, 0.0)
    r_i = lax.broadcasted_iota(jnp.int32, (tm, tm), 0)
    c_i = lax.broadcasted_iota(jnp.int32, (tm, tm), 1)
    before = _dot(jnp.where(c_i < r_i, 1.0, 0.0).astype(BF16), onehot.astype(BF16)) + carry_sc[...]
    ranks = jnp.zeros((tm, LANES), F32)
    for k in range(TOP_K):
        ranks = jnp.where(lane == k, jnp.sum(jnp.where(hots[k], before, 0.0), axis=1, keepdims=True), ranks)
    rank_ref[...] = ranks[:, :TOP_K].astype(jnp.int32)
    carry_sc[...] = carry_sc[...] + jnp.sum(onehot, axis=0, keepdims=True)
    cnt_ref[...] = carry_sc[...].astype(jnp.int32)


def _mix(x2, a_out, b_out, w_out, g, beta, w_router, b_router, alpha):
    n, d = x2.shape
    tm = MIX_TM
    half = a_out.shape[1]
    row = lambda i: (i, 0)
    const = lambda i: (0, 0)
    wr = jnp.pad(w_router, ((0, 0), (0, LANES - N_EXPERTS)))
    br = jnp.pad(b_router, (0, LANES - N_EXPERTS)).reshape(1, LANES)
    return pl.pallas_call(
        functools.partial(_mix_kernel, alpha=alpha),
        grid=(n // tm,),
        in_specs=[pl.BlockSpec((tm, d), row), pl.BlockSpec((tm, half), row), pl.BlockSpec((tm, half), row),
                  pl.BlockSpec(w_out.shape, const), pl.BlockSpec((1, d), const), pl.BlockSpec((1, d), const),
                  pl.BlockSpec(wr.shape, const), pl.BlockSpec((1, LANES), const)],
        out_specs=[pl.BlockSpec((tm, d // LANES, LANES), lambda i: (i, 0, 0)), pl.BlockSpec((tm, TOP_K), row),
                   pl.BlockSpec((tm, TOP_K, LANES), lambda i: (i, 0, 0)), pl.BlockSpec((tm, TOP_K), row),
                   pl.BlockSpec((1, LANES), const)],
        out_shape=[jax.ShapeDtypeStruct((n, d // LANES, LANES), F32), jax.ShapeDtypeStruct((n, TOP_K), jnp.int32),
                   jax.ShapeDtypeStruct((n, TOP_K, LANES), F32), jax.ShapeDtypeStruct((n, TOP_K), jnp.int32),
                   jax.ShapeDtypeStruct((1, LANES), jnp.int32)],
        scratch_shapes=[pltpu.VMEM((1, LANES), F32)],
        compiler_params=pltpu.CompilerParams(dimension_semantics=("arbitrary",), vmem_limit_bytes=VMEM_LIMIT),
        name="mix",
    )(x2, a_out, b_out, w_out.astype(BF16), g.reshape(1, d), beta.reshape(1, d), wr, br)


def _route(top_i, rank, counts, tm):
    n = top_i.shape[0]
    n_tiles = (n * TOP_K) // tm + N_EXPERTS
    counts = counts[0, :N_EXPERTS]
    tiles_per = (counts + tm - 1) // tm
    tile_end = jnp.cumsum(tiles_per)
    pstart = (tile_end - tiles_per) * tm
    pos = rank + jnp.sum(jnp.where(top_i[..., None] == jnp.arange(N_EXPERTS, dtype=jnp.int32), pstart, 0), axis=-1)
    tile_ids = jnp.arange(n_tiles, dtype=jnp.int32)
    n_valid = tile_end[-1]
    tile_expert = jnp.sum(tile_end[None, :] <= jnp.minimum(tile_ids, n_valid - 1)[:, None], axis=1).astype(jnp.int32)
    n_valid = n_valid[None].astype(jnp.int32)
    pad_start = jnp.concatenate([pstart + counts, n_valid]).astype(jnp.int32)
    return pos.astype(jnp.int32), tile_expert, n_valid, pad_start, n_tiles


def _dispatch_kernel(pad_ref, h_ref, pos_hbm, xs_hbm, pos_s, zbuf, sem_idx, sem_row):
    i = pl.program_id(0)
    tm = h_ref.shape[0]

    def pos_copy(t, slot):
        return pltpu.make_async_copy(pos_hbm.at[t], pos_s.at[pl.ds(slot * (tm * TOP_K), tm * TOP_K)], sem_idx.at[slot])

    @pl.when(i == 0)
    def _():
        zt = zbuf.shape[0]
        zbuf[...] = jnp.zeros(zbuf.shape, zbuf.dtype)
        for e in range(N_EXPERTS):
            cpe = pltpu.make_async_copy(zbuf, xs_hbm.at[pl.ds(pad_ref[e], zt)], sem_row)
            cpe.start()
            cpe.wait()

        def fill(t, _):
            cpz = pltpu.make_async_copy(zbuf, xs_hbm.at[pl.ds(pl.multiple_of(t * zt, zt), zt)], sem_row)
            cpz.start()
            cpz.wait()
            return 0

        lax.fori_loop(pad_ref[N_EXPERTS], xs_hbm.shape[0] // zt, fill, 0)
        pos_copy(0, 0).start()

    slot = i % 2
    pos_copy(i, slot).wait()

    @pl.when(i + 1 < pl.num_programs(0))
    def _():
        pos_copy(i + 1, 1 - slot).start()

    base = slot * (tm * TOP_K)

    def body(r, _):
        for s in range(TOP_K):
            pltpu.make_async_copy(h_ref.at[r], xs_hbm.at[pos_s[base + r * TOP_K + s]], sem_row).start(priority=s % 2)
        return 0

    lax.fori_loop(0, tm, body, 0, unroll=8)
    for s in range(TOP_K):
        pltpu.make_async_copy(h_ref, xs_hbm.at[pl.ds(0, tm)], sem_row).wait()


def _dispatch(h3, pos, pad_start, n_rows, zrows):
    n, nt8, _ = h3.shape
    tm = ROW_TM
    return pl.pallas_call(
        _dispatch_kernel,
        grid_spec=pltpu.PrefetchScalarGridSpec(
            num_scalar_prefetch=1,
            grid=(n // tm,),
            in_specs=[pl.BlockSpec((tm, nt8, LANES), lambda i, pad: (i, 0, 0)), pl.BlockSpec(memory_space=pl.ANY)],
            out_specs=pl.BlockSpec(memory_space=pl.ANY),
            scratch_shapes=[pltpu.SMEM((2 * tm * TOP_K,), jnp.int32), pltpu.VMEM((zrows, nt8, LANES), F32),
                            pltpu.SemaphoreType.DMA((2,)), pltpu.SemaphoreType.DMA]),
        out_shape=jax.ShapeDtypeStruct((n_rows + zrows, nt8, LANES), F32),
        compiler_params=pltpu.CompilerParams(dimension_semantics=("arbitrary",), vmem_limit_bytes=VMEM_LIMIT),
        name="dispatch",
    )(pad_start, h3, pos.reshape(n // tm, tm * TOP_K))


def _moe_kernel(te_ref, nv_ref, xs_hbm, wg_ref, bg_ref, wu_ref, bu_ref, wd_ref, bd_ref, ys_hbm,
                xbuf, ybuf, wg_bf, wu_bf, wd_bf, sem_in, sem_out):
    i = pl.program_id(0)
    n_valid = nv_ref[0]
    tm = ybuf.shape[0]
    nt8 = xs_hbm.shape[1]

    def in_copy(t, slot, c):
        return pltpu.make_async_copy(xs_hbm.at[pl.ds(t * tm, tm), c, :], xbuf.at[slot, :, pl.ds(c * LANES, LANES)],
                                     sem_in.at[slot])

    def out_copy(t, c):
        return pltpu.make_async_copy(ybuf.at[:, pl.ds(c * LANES, LANES)], ys_hbm.at[pl.ds(t * tm, tm), c, :], sem_out)

    def fetch(t, slot):
        for c in range(nt8):
            in_copy(t, slot, c).start()

    def store(t):
        for c in range(nt8):
            out_copy(t, c).start()

    def wait_store(t):
        for c in range(nt8):
            out_copy(t, c).wait()

    @pl.when(i == 0)
    def _():
        fetch(0, 0)

    @pl.when(i < n_valid)
    def _():
        slot = i % 2
        for c in range(nt8):
            in_copy(i, slot, c).wait()

        @pl.when(i + 1 < n_valid)
        def _():
            fetch(i + 1, 1 - slot)

        @pl.when(jnp.logical_or(i == 0, te_ref[i] != te_ref[jnp.maximum(i - 1, 0)]))
        def _():
            wg_bf[...] = wg_ref[0].astype(BF16)
            wu_bf[...] = wu_ref[0].astype(BF16)
            wd_bf[...] = wd_ref[0].astype(BF16)

        x = xbuf[slot].astype(BF16)
        g = jnp.minimum(_dot(x, wg_bf[...]) + bg_ref[0], SWIGLU_LIMIT)
        u = jnp.clip(_dot(x, wu_bf[...]) + bu_ref[0], -SWIGLU_LIMIT, SWIGLU_LIMIT)
        act = g * (1.0 / (1.0 + jnp.exp(-SWIGLU_ALPHA * g))) * (u + 1.0)
        y = _dot(act.astype(BF16), wd_bf[...]) + bd_ref[0]

        @pl.when(i > 0)
        def _():
            wait_store(i - 1)

        ybuf[...] = y
        store(i)

        @pl.when(i == n_valid - 1)
        def _():
            wait_store(i)

    @pl.when(i >= n_valid)
    def _():
        ybuf[...] = jnp.zeros(ybuf.shape, ybuf.dtype)
        store(i)
        wait_store(i)


def _moe(xs, tile_expert, n_valid, n_tiles, w_gate, b_gate, w_up, b_up, w_down, b_down):
    _, nt8, _ = xs.shape
    tm = MOE_TM
    ne, d, dff = w_gate.shape
    wmap = lambda i, te, nv: (te[i], 0, 0)
    any_spec = pl.BlockSpec(memory_space=pl.ANY)
    grid_spec = pltpu.PrefetchScalarGridSpec(
        num_scalar_prefetch=2,
        grid=(n_tiles,),
        in_specs=[any_spec,
                  pl.BlockSpec((1, d, dff), wmap), pl.BlockSpec((1, 1, dff), wmap),
                  pl.BlockSpec((1, d, dff), wmap), pl.BlockSpec((1, 1, dff), wmap),
                  pl.BlockSpec((1, dff, d), wmap), pl.BlockSpec((1, 1, d), wmap)],
        out_specs=any_spec,
        scratch_shapes=[pltpu.VMEM((2, tm, d), F32), pltpu.VMEM((tm, d), F32),
                        pltpu.VMEM((d, dff), BF16), pltpu.VMEM((d, dff), BF16), pltpu.VMEM((dff, d), BF16),
                        pltpu.SemaphoreType.DMA((2,)), pltpu.SemaphoreType.DMA])
    return pl.pallas_call(
        _moe_kernel,
        grid_spec=grid_spec,
        out_shape=jax.ShapeDtypeStruct((n_tiles * tm, nt8, LANES), F32),
        compiler_params=pltpu.CompilerParams(dimension_semantics=("arbitrary",), vmem_limit_bytes=VMEM_LIMIT),
        name="moe",
    )(tile_expert, n_valid, xs,
      w_gate, b_gate.reshape(ne, 1, dff), w_up, b_up.reshape(ne, 1, dff), w_down, b_down.reshape(ne, 1, d))


def _combine_kernel(h_ref, gate_ref, pos_hbm, ys_hbm, g_ref, beta_ref, o_hbm, pos_s, ybuf, obuf, sem_idx, sem_row,
                    sem_out, *, alpha):
    i = pl.program_id(0)
    n = pl.num_programs(0)
    tm = h_ref.shape[0]
    slot = i % 2

    def pos_copy(t, sl):
        return pltpu.make_async_copy(pos_hbm.at[t], pos_s.at[pl.ds(sl * (tm * TOP_K), tm * TOP_K)], sem_idx.at[sl])

    def gather(sl):
        base = sl * (tm * TOP_K)

        def body(r, _):
            for s in range(TOP_K):
                pltpu.make_async_copy(ys_hbm.at[pos_s[base + r * TOP_K + s]], ybuf.at[sl, s, r],
                                      sem_row.at[sl]).start(priority=s % 2)
            return 0

        lax.fori_loop(0, tm, body, 0, unroll=8)

    @pl.when(i == 0)
    def _():
        first = pos_copy(0, 0)
        first.start()
        first.wait()
        gather(0)

        @pl.when(n > 1)
        def _():
            pos_copy(1, 1).start()

    @pl.when(i + 1 < n)
    def _():
        pos_copy(i + 1, 1 - slot).wait()
        gather(1 - slot)

        @pl.when(i + 2 < n)
        def _():
            pos_copy(i + 2, slot).start()

    for s in range(TOP_K):
        pltpu.make_async_copy(ys_hbm.at[pl.ds(0, tm)], ybuf.at[slot, s], sem_row.at[slot]).wait()

    z = alpha * h_ref[...]
    for s in range(TOP_K):
        z = z + gate_ref[:, s:s + 1, :] * ybuf[slot, s]
    inv_d = 1.0 / (z.shape[1] * z.shape[2])
    mu = jnp.sum(z, axis=(1, 2), keepdims=True) * inv_d
    dz = z - mu
    var = jnp.sum(dz * dz, axis=(1, 2), keepdims=True) * inv_d
    out = dz * lax.rsqrt(var + LN_EPS) * g_ref[...] + beta_ref[...]

    def out_copy(t, c):
        return pltpu.make_async_copy(obuf.at[:, c, :], o_hbm.at[pl.ds(t * tm, tm), pl.ds(c * LANES, LANES)], sem_out)

    nt8 = obuf.shape[1]

    @pl.when(i > 0)
    def _():
        for c in range(nt8):
            out_copy(i - 1, c).wait()

    obuf[...] = out
    for c in range(nt8):
        out_copy(i, c).start()

    @pl.when(i == n - 1)
    def _():
        for c in range(nt8):
            out_copy(i, c).wait()


def _combine(h3, gates, pos, ys, g, beta, alpha):
    n, nt8, _ = h3.shape
    tm = ROW_TM
    tile = lambda i: (i, 0, 0)
    const = lambda i: (0, 0, 0)
    return pl.pallas_call(
        functools.partial(_combine_kernel, alpha=alpha),
        grid=(n // tm,),
        in_specs=[pl.BlockSpec((tm, nt8, LANES), tile), pl.BlockSpec((tm, TOP_K, LANES), tile),
                  pl.BlockSpec(memory_space=pl.ANY), pl.BlockSpec(memory_space=pl.ANY),
                  pl.BlockSpec((1, nt8, LANES), const), pl.BlockSpec((1, nt8, LANES), const)],
        out_specs=pl.BlockSpec(memory_space=pl.ANY),
        out_shape=jax.ShapeDtypeStruct((n, nt8 * LANES), F32),
        scratch_shapes=[pltpu.SMEM((2 * tm * TOP_K,), jnp.int32), pltpu.VMEM((2, TOP_K, tm, nt8, LANES), F32),
                        pltpu.VMEM((tm, nt8, LANES), F32),
                        pltpu.SemaphoreType.DMA((2,)), pltpu.SemaphoreType.DMA((2,)), pltpu.SemaphoreType.DMA],
        compiler_params=pltpu.CompilerParams(dimension_semantics=("arbitrary",), vmem_limit_bytes=VMEM_LIMIT),
        name="combine",
    )(h3, gates, pos.reshape(n // tm, tm * TOP_K), ys, g.reshape(1, nt8, LANES), beta.reshape(1, nt8, LANES))


def _layer(x, tabs, alpha, w_in, ik_g, ik_b, q_g, w_q_up, kv_g, w_kv_up, w_out, ln1_g, ln1_b,
           w_router, b_router, w_gate, b_gate, w_up, b_up, w_down, b_down, ln2_g, ln2_b):
    bsz, seq, d = x.shape
    w_t, w_n, wq_t, bd_uk_t, w_uv_t = _pack_weights(w_in, w_q_up, w_kv_up)
    aqt, iqt, iwt, akv, avt, ikp, qabst, ckv, latt = _proj(x, tabs, w_t, w_n, wq_t, bd_uk_t, ik_g, ik_b, q_g, kv_g)
    a_out = _dsa(aqt, iqt, iwt, akv, avt, ikp)
    b_out = _mla(qabst, ckv, latt, w_uv_t)
    n = bsz * seq
    h3, top_i, gates, rank, counts = _mix(x.reshape(n, d), a_out.reshape(n, -1), b_out.reshape(n, -1), w_out,
                                          ln1_g, ln1_b, w_router, b_router, alpha)
    pos, tile_expert, n_valid, pad_start, n_tiles = _route(top_i, rank, counts, MOE_TM)
    xs = _dispatch(h3, pos, pad_start, n_tiles * MOE_TM, MOE_TM)
    ys = _moe(xs, tile_expert, n_valid, n_tiles, w_gate, b_gate, w_up, b_up, w_down, b_down)
    return _combine(h3, gates, pos, ys, ln2_g, ln2_b, alpha).reshape(bsz, seq, d)


def kernel(x, positions, w_mix_in, idx_k_norm_g, idx_k_norm_b, mla_q_norm_g, mla_w_q_up, mla_kv_norm_g, mla_w_kv_up, w_mix_out, ln1_g, ln1_b, w_router, b_router, w_gate, b_gate, w_up, b_up, w_down, b_down, ln2_g, ln2_b):
    depth = w_mix_in.shape[0]
    alpha = float((2 * depth) ** 0.25)
    tabs = _rope_tables(positions)
    for l in range(depth):
        x = _layer(x, tabs, alpha, w_mix_in[l], idx_k_norm_g[l], idx_k_norm_b[l], mla_q_norm_g[l], mla_w_q_up[l],
                   mla_kv_norm_g[l], mla_w_kv_up[l], w_mix_out[l], ln1_g[l], ln1_b[l], w_router[l], b_router[l],
                   w_gate[l], b_gate[l], w_up[l], b_up[l], w_down[l], b_down[l], ln2_g[l], ln2_b[l])
    return x
```

```python
import functools

import jax
import jax.numpy as jnp
import numpy as np
from jax import lax
from jax.experimental import pallas as pl
from jax.experimental.pallas import tpu as pltpu

F32 = jnp.float32
BF16 = jnp.bfloat16

A_HEADS, A_HEAD_DIM = 8, 64
IDX_HEADS, IDX_DIM = 8, 64
TOPK_MAX = 256
B_HEADS, B_NOPE, B_ROPE, B_V = 8, 64, 32, 64
Q_LORA, KV_LORA = 256, 128
N_EXPERTS, TOP_K = 32, 4
SWIGLU_LIMIT, SWIGLU_ALPHA = 7.0, 1.702
ROPE_THETA = 10000.0
LN_EPS, RMS_EPS = 1e-5, 1e-6

LANES = 128
SUBLANES = 8
VMEM_LIMIT = 56 * 1024 * 1024

PROJ_TM = 512
ATT_TQ = LANES
ATT_TK = 1024
SCORE_TK = 1024
SEL_TK = 512
ONES_ROWS = 16
LOG2_E = 1.4426950408889634
MIX_TM = 512
MOE_TM = 256
ROW_TM = 512
SEL_MAX_ITERS = 400
SEL_UNROLL = 4

MASKED = -1e30
M_FLOOR = -1e29
TINY = float(np.finfo(np.float32).tiny)


def _dot(a, b):
    return jnp.dot(a, b, preferred_element_type=F32)


def _dot_t(a, b):
    return lax.dot_general(a, b, (((1,), (1,)), ((), ())), preferred_element_type=F32)


def _lane_iota(shape):
    return lax.broadcasted_iota(jnp.int32, shape, len(shape) - 1)


def _fold(v, op, rows=64):
    n = v.shape[0] // rows
    acc = v[0:rows]
    for i in range(1, n):
        acc = op(acc, v[i * rows:(i + 1) * rows])
    return acc


def _rows_reduce(v, op):
    if op == "sum":
        return jnp.sum(v, axis=0, keepdims=True)
    if op == "max":
        return jnp.max(v, axis=0, keepdims=True)
    return jnp.min(v, axis=0, keepdims=True)


_R_AQ, _R_AQR, _R_IQ, _R_IQR, _R_V, _R_IW, _R_QC, _R_KVC, _R_TOTAL = 0, 512, 1024, 1536, 2048, 2112, 2128, 2384, 2512
_C_KV, _C_KVR, _C_IK, _C_KVC, _C_KR, _C_KRR, _C_TOTAL = 0, 128, 256, 384, 512, 640, 768


def _rot_cols(w, dim):
    k, n = w.shape
    w3 = w.reshape(k, n // dim, dim)
    return jnp.concatenate([-w3[..., dim // 2:], w3[..., :dim // 2]], axis=-1).reshape(k, n)


def _pad_cols(w, width):
    return jnp.pad(w, ((0, 0), (0, width - w.shape[1])))


def _pack_weights(w_in, w_q_up, w_kv_up):
    o = np.cumsum((0, 512, 64, 64, 512, 64, 8, 256, 128, 32))
    a_q, a_k, a_v, i_q, i_k, i_w, b_qc, b_kvc, b_kr = [w_in[:, o[i]:o[i + 1]] for i in range(9)]
    w_t = jnp.concatenate([a_q, _rot_cols(a_q, 64), i_q, _rot_cols(i_q, 64), a_v, _pad_cols(i_w, 16),
                           b_qc, b_kvc], axis=1).T
    assert w_t.shape[0] == _R_TOTAL
    w_n = jnp.concatenate([a_k, a_v, _rot_cols(a_k, 64), jnp.zeros_like(a_k), _pad_cols(i_k, 128), b_kvc,
                           _pad_cols(b_kr, 128), _pad_cols(_rot_cols(b_kr, 32), 128)], axis=1)
    assert w_n.shape[1] == _C_TOTAL
    wq = w_q_up.reshape(Q_LORA, B_HEADS, B_NOPE + B_ROPE)
    wq_rope = wq[..., B_NOPE:].reshape(Q_LORA, B_HEADS * B_ROPE)
    wq_t = jnp.concatenate([wq[..., :B_NOPE].reshape(Q_LORA, B_HEADS * B_NOPE), wq_rope,
                            _rot_cols(wq_rope, B_ROPE)], axis=1).T
    wkv = w_kv_up.reshape(KV_LORA, B_HEADS, B_NOPE + B_V)
    eye = jnp.eye(B_HEADS, dtype=w_kv_up.dtype)
    bd_uk_t = jnp.einsum('lhd,hg->glhd', wkv[..., :B_NOPE], eye).reshape(B_HEADS * KV_LORA, B_HEADS * B_NOPE)
    w_uv_t = jnp.transpose(wkv[..., B_NOPE:], (1, 2, 0))
    return w_t.astype(BF16), w_n.astype(BF16), wq_t.astype(BF16), bd_uk_t.astype(BF16), w_uv_t.astype(BF16)


def _rope_tables(positions):
    pos = positions.astype(F32)[..., None]
    inv64 = ROPE_THETA ** (-jnp.arange(0, 64, 2, dtype=F32) / 64)
    inv32 = ROPE_THETA ** (-jnp.arange(0, 32, 2, dtype=F32) / 32)
    c64, s64 = jnp.cos(pos * inv64), jnp.sin(pos * inv64)
    c32, s32 = jnp.cos(pos * inv32), jnp.sin(pos * inv32)
    nat = (jnp.tile(c64, 4), jnp.tile(s64, 4), jnp.tile(c32, 8), jnp.tile(s32, 8))
    tr = tuple(jnp.swapaxes(jnp.tile(t, 2), 1, 2) for t in (c64, s64, c32, s32))
    return nat + tr


def _proj_kernel(x_ref, ca_ref, sa_ref, cr_ref, sr_ref, cat_ref, sat_ref, crt_ref, srt_ref,
                 wt_ref, wn_ref, wqt_ref, bdukt_ref, ikg_ref, ikb_ref, qg_ref, kvg_ref, kvgt_ref,
                 aqt_ref, iqt_ref, iwt_ref, akv_ref, avt_ref, ik_ref, qabst_ref, ckv_ref, latt_ref):
    tm = x_ref.shape[0]
    nqb = tm // ATT_TQ
    xb = x_ref[...].astype(BF16)
    ca, sa, cr, sr = ca_ref[...], sa_ref[...], cr_ref[...], sr_ref[...]
    lane = _lane_iota((tm, LANES))
    low64 = lane < 64

    def proj_t(r0, rows):
        return _dot_t(wt_ref[r0:r0 + rows, :], xb)

    def proj_n(c0, width):
        return _dot(xb, wn_ref[:, c0:c0 + width])

    cat8 = jnp.concatenate([cat_ref[...]] * 8, axis=0)
    sat8 = jnp.concatenate([sat_ref[...]] * 8, axis=0)

    def split_heads(v, out_ref):
        vb = v.astype(BF16)
        zeros = jnp.zeros((64, ATT_TQ), BF16)
        for c in range(nqb):
            for h in range(8):
                out_ref[c, 0:64, h * ATT_TQ:(h + 1) * ATT_TQ] = vb[64 * h:64 * h + 64, c * ATT_TQ:(c + 1) * ATT_TQ]
                out_ref[c, 64:128, h * ATT_TQ:(h + 1) * ATT_TQ] = zeros

    aq_t = (proj_t(_R_AQ, 512) * cat8 + proj_t(_R_AQR, 512) * sat8) * (A_HEAD_DIM ** -0.5 * LOG2_E)
    split_heads(aq_t, aqt_ref)
    iq_t = proj_t(_R_IQ, 512) * cat8 + proj_t(_R_IQR, 512) * sat8
    split_heads(iq_t, iqt_ref)
    iw_t = proj_t(_R_IW, 16)[0:IDX_HEADS] * ((IDX_HEADS * IDX_DIM) ** -0.5)
    for c in range(nqb):
        iwt_ref[c] = iw_t[:, c * ATT_TQ:(c + 1) * ATT_TQ]
    ones_rows = jnp.where(lax.broadcasted_iota(jnp.int32, (ONES_ROWS, tm), 0) == 0, 1.0, 0.0).astype(BF16)
    avt_ref[...] = jnp.concatenate([proj_t(_R_V, 64).astype(BF16), ones_rows], axis=0)

    kc = jnp.where(low64, ca, 1.0)
    ks = jnp.where(low64, sa, 0.0)
    akv_ref[...] = (proj_n(_C_KV, 128) * kc + proj_n(_C_KVR, 128) * ks).astype(BF16)

    y = proj_n(_C_IK, 128)
    mu = jnp.sum(y, axis=1, keepdims=True) * (1.0 / IDX_DIM)
    d = jnp.where(low64, y - mu, 0.0)
    var = jnp.sum(d * d, axis=1, keepdims=True) * (1.0 / IDX_DIM)
    yn = d * lax.rsqrt(var + LN_EPS) * ikg_ref[...] + ikb_ref[...]
    rot = jnp.where(lane < 32, -pltpu.roll(yn, LANES - 32, 1), pltpu.roll(yn, 32, 1))
    ik_ref[...] = jnp.where(low64, yn * ca + rot * sa, 0.0).astype(BF16)

    qc_t = proj_t(_R_QC, Q_LORA)
    qn_t = (qc_t * lax.rsqrt(jnp.mean(qc_t * qc_t, axis=0, keepdims=True) + RMS_EPS) * qg_ref[...]).astype(BF16)
    q_t = _dot(wqt_ref[...], qn_t)
    scale = (B_NOPE + B_ROPE) ** -0.5 * LOG2_E
    q_lat_t = (_dot(bdukt_ref[...], q_t[0:512].astype(BF16)) * scale).astype(BF16)
    crt8 = jnp.concatenate([crt_ref[...]] * 8, axis=0)
    srt8 = jnp.concatenate([srt_ref[...]] * 8, axis=0)
    q_r_t = ((q_t[512:768] * crt8 + q_t[768:1024] * srt8) * scale).astype(BF16)
    zeros = jnp.zeros((2 * LANES - KV_LORA - B_ROPE, ATT_TQ), BF16)
    for c in range(nqb):
        tok = slice(c * ATT_TQ, (c + 1) * ATT_TQ)
        for h in range(B_HEADS):
            col = slice(h * ATT_TQ, (h + 1) * ATT_TQ)
            qabst_ref[c, 0:KV_LORA, col] = q_lat_t[h * KV_LORA:(h + 1) * KV_LORA, tok]
            qabst_ref[c, KV_LORA:KV_LORA + B_ROPE, col] = q_r_t[h * B_ROPE:(h + 1) * B_ROPE, tok]
            qabst_ref[c, KV_LORA + B_ROPE:2 * LANES, col] = zeros

    kvc = proj_n(_C_KVC, KV_LORA)
    kvn = kvc * lax.rsqrt(jnp.mean(kvc * kvc, axis=1, keepdims=True) + RMS_EPS) * kvg_ref[...]
    kr = proj_n(_C_KR, 128) * cr + proj_n(_C_KRR, 128) * sr
    ckv_ref[...] = jnp.concatenate([kvn, kr], axis=1).astype(BF16)
    kvc_t = proj_t(_R_KVC, KV_LORA)
    lat_t = (kvc_t * lax.rsqrt(jnp.mean(kvc_t * kvc_t, axis=0, keepdims=True) + RMS_EPS) * kvgt_ref[...]).astype(BF16)
    latt_ref[...] = jnp.concatenate([lat_t, ones_rows], axis=0)


def _proj(x, tabs, w_t, w_n, wq_t, bd_uk_t, ik_g, ik_b, q_g, kv_g):
    bsz, seq, d = x.shape
    tm = PROJ_TM
    nqb = tm // ATT_TQ
    tok = lambda b, i: (b, i, 0)
    feat = lambda b, i: (b, 0, i)
    qblk = lambda b, i: (b, i, 0, 0)
    const2 = lambda b, i: (0, 0)
    nat_spec = pl.BlockSpec((None, tm, LANES), tok)
    col = lambda v: jnp.broadcast_to(v.reshape(-1, 1), (v.shape[0], tm))
    ik_g = jnp.pad(ik_g, (0, LANES - IDX_DIM)).reshape(1, LANES)
    ik_b = jnp.pad(ik_b, (0, LANES - IDX_DIM)).reshape(1, LANES)
    consts = (w_t, w_n, wq_t, bd_uk_t, ik_g, ik_b, col(q_g), kv_g.reshape(1, KV_LORA), col(kv_g))
    return pl.pallas_call(
        _proj_kernel,
        grid=(bsz, seq // tm),
        in_specs=[pl.BlockSpec((None, tm, d), tok), nat_spec, nat_spec, nat_spec, nat_spec,
                  pl.BlockSpec((None, 64, tm), feat), pl.BlockSpec((None, 64, tm), feat),
                  pl.BlockSpec((None, 32, tm), feat), pl.BlockSpec((None, 32, tm), feat)]
                 + [pl.BlockSpec(c.shape, const2) for c in consts],
        out_specs=[pl.BlockSpec((None, nqb, LANES, A_HEADS * ATT_TQ), qblk),
                   pl.BlockSpec((None, nqb, LANES, IDX_HEADS * ATT_TQ), qblk),
                   pl.BlockSpec((None, nqb, IDX_HEADS, ATT_TQ), qblk),
                   pl.BlockSpec((None, tm, LANES), tok),
                   pl.BlockSpec((None, A_HEAD_DIM + ONES_ROWS, tm), feat),
                   pl.BlockSpec((None, tm, LANES), tok),
                   pl.BlockSpec((None, nqb, 2 * LANES, B_HEADS * ATT_TQ), qblk),
                   pl.BlockSpec((None, tm, 2 * LANES), tok),
                   pl.BlockSpec((None, KV_LORA + ONES_ROWS, tm), feat)],
        out_shape=[jax.ShapeDtypeStruct((bsz, seq // ATT_TQ, LANES, A_HEADS * ATT_TQ), BF16),
                   jax.ShapeDtypeStruct((bsz, seq // ATT_TQ, LANES, IDX_HEADS * ATT_TQ), BF16),
                   jax.ShapeDtypeStruct((bsz, seq // ATT_TQ, IDX_HEADS, ATT_TQ), F32),
                   jax.ShapeDtypeStruct((bsz, seq, LANES), BF16),
                   jax.ShapeDtypeStruct((bsz, A_HEAD_DIM + ONES_ROWS, seq), BF16),
                   jax.ShapeDtypeStruct((bsz, seq, LANES), BF16),
                   jax.ShapeDtypeStruct((bsz, seq // ATT_TQ, 2 * LANES, B_HEADS * ATT_TQ), BF16),
                   jax.ShapeDtypeStruct((bsz, seq, 2 * LANES), BF16),
                   jax.ShapeDtypeStruct((bsz, KV_LORA + ONES_ROWS, seq), BF16)],
        compiler_params=pltpu.CompilerParams(
            dimension_semantics=("arbitrary", "arbitrary"), vmem_limit_bytes=VMEM_LIMIT),
        name="proj",
    )(x, *tabs, *consts)


def _attend_init(nh, val_rows):
    return ([jnp.full((1, LANES), M_FLOOR, F32) for _ in range(nh)],
            jnp.zeros((val_rows + ONES_ROWS, nh * LANES), F32))


def _attend_update(lg_ref, bias, vt_blk, state):
    ms, acc = state
    new_ms, ps, alphas = [], [], []
    for h in range(len(ms)):
        lh = lg_ref[:, h * LANES:(h + 1) * LANES]
        if bias is not None:
            lh = lh + bias
        m_new = jnp.maximum(ms[h], _rows_reduce(_fold(lh, jnp.maximum), "max"))
        alphas.append(jnp.exp2(ms[h] - m_new))
        ps.append(jnp.exp2(lh - m_new).astype(BF16))
        new_ms.append(m_new)
    return new_ms, jnp.concatenate(alphas, axis=1) * acc + _dot(vt_blk, jnp.concatenate(ps, axis=1))


def _attend_pipelined(n_blocks, scores, update, lg_a, lg_b, state):
    n_look = (n_blocks - 1) // 2

    def pair(i, st, lookahead):
        kb0 = 2 * i
        lg_b[...] = scores(kb0 + 1)
        st = update(kb0, lg_a, not lookahead, st)
        if lookahead:
            lg_a[...] = scores(kb0 + 2)
        return update(kb0 + 1, lg_b, not lookahead, st)

    lg_a[...] = scores(0)
    state = lax.fori_loop(0, n_look, lambda i, st: pair(i, st, True), state)
    return lax.cond(n_blocks % 2 == 0,
                    lambda st: pair(n_look, st, False),
                    lambda st: update(2 * n_look, lg_a, True, st), state)


def _attend_finish(state, val_rows):
    acc = state[1]
    return acc[0:val_rows] / acc[val_rows:val_rows + 1]


def _dsa_kernel(aqt_ref, iqt_ref, iwt_ref, akv_ref, avt_ref, ik_ref, out_ref, s_sc, lg_a, lg_b):
    nh = iwt_ref.shape[0]
    tq, tk = ATT_TQ, ATT_TK
    seq = akv_ref.shape[0]
    j = pl.program_id(1)
    nkb = (j * tq) // tk + 1
    iqt = iqt_ref[...]
    w = iwt_ref[...]

    ts = SCORE_TK
    krow = lax.broadcasted_iota(jnp.int32, (ts, tq), 0)
    qcol = j * tq + lax.broadcasted_iota(jnp.int32, (ts, tq), 1)

    def score_body(kb, carry):
        cmax, cmin = carry
        k0 = pl.multiple_of(kb * ts, ts)
        y = _dot(ik_ref[pl.ds(k0, ts), :], iqt)
        s = jnp.zeros((ts, tq), F32)
        for h in range(nh):
            s = s + w[h:h + 1, :] * jnp.maximum(y[:, h * tq:(h + 1) * tq], 0.0)
        causal = (k0 + krow) <= qcol
        s_lo = jnp.where(causal, s, -jnp.inf)
        s_sc[pl.ds(k0, ts), :] = s_lo
        cmax = jnp.maximum(cmax, _fold(s_lo, jnp.maximum))
        cmin = jnp.minimum(cmin, _fold(jnp.where(causal, s, jnp.inf), jnp.minimum))
        return cmax, cmin

    cmax, cmin = lax.fori_loop(0, (j * tq) // ts + 1, score_body,
                               (jnp.full((64, tq), -jnp.inf, F32), jnp.full((64, tq), jnp.inf, F32)))
    rmax = _rows_reduce(cmax, "max")
    rmin = _rows_reduce(cmin, "min")

    n_causal = (j * tq + lax.broadcasted_iota(jnp.int32, (1, tq), 1) + 1).astype(F32)
    kt = jnp.minimum(n_causal, float(TOPK_MAX))

    tc = SEL_TK
    ncb = (j * tq) // tc + 1
    crow_f = lax.broadcasted_iota(jnp.int32, (tc, tq), 0).astype(F32)

    def count(pred):
        def body(cb, acc):
            k0 = pl.multiple_of(cb * tc, tc)
            return acc + _fold(jnp.where(pred(s_sc[pl.ds(k0, tc), :], k0), 1.0, 0.0), jnp.add)
        return _rows_reduce(lax.fori_loop(0, ncb, body, jnp.zeros((64, tq), F32)), "sum")

    def propose(it, lo, hi, clo):
        mid = 0.5 * lo + 0.5 * hi
        mid = jnp.where(jnp.logical_and(it == 0, jnp.logical_and(lo < 0.0, hi > 0.0)), 0.0, mid)
        mid = jnp.where(jnp.logical_and(it == 1, jnp.logical_and(lo == 0.0, hi > TINY)), TINY, mid)
        active = jnp.logical_and(clo != kt, jnp.logical_and(mid > lo, mid < hi))
        return mid, active

    def sel_step(it, lo, hi, clo, chi):
        mid, active = propose(it, lo, hi, clo)
        cm = count(lambda blk, k0: blk >= mid)
        ge = cm >= kt
        up = jnp.logical_and(active, ge)
        dn = jnp.logical_and(active, jnp.logical_not(ge))
        return jnp.where(up, mid, lo), jnp.where(dn, mid, hi), jnp.where(up, cm, clo), jnp.where(dn, cm, chi)

    def sel_body(c):
        it, _, lo, hi, clo, chi = c
        for u in range(SEL_UNROLL):
            lo, hi, clo, chi = sel_step(it + u, lo, hi, clo, chi)
        _, active = propose(it + SEL_UNROLL, lo, hi, clo)
        go = jnp.logical_and(jnp.max(jnp.where(active, 1.0, 0.0)) > 0.0, it + SEL_UNROLL < SEL_MAX_ITERS)
        return it + SEL_UNROLL, go, lo, hi, clo, chi

    lo0 = rmin
    hi0 = rmax + jnp.abs(rmax) + 1e-30
    clo0, chi0 = n_causal, jnp.zeros((1, tq), F32)
    _, active0 = propose(0, lo0, hi0, clo0)
    go0 = jnp.max(jnp.where(active0, 1.0, 0.0)) > 0.0
    _, _, lo, hi, clo, chi = lax.while_loop(lambda c: c[1], sel_body, (jnp.int32(0), go0, lo0, hi0, clo0, chi0))

    tie = clo > kt

    @pl.when(jnp.max(jnp.where(tie, 1.0, 0.0)) > 0.0)
    def _():
        need = kt - chi

        def idx_body(_, c):
            jl, jh = c
            jm = jnp.floor((jl + jh) * 0.5)
            ok = count(lambda blk, k0: jnp.logical_and(blk == lo, crow_f + k0.astype(F32) <= jm)) >= need
            return jnp.where(ok, jl, jm), jnp.where(ok, jm, jh)

        n_steps = int(np.ceil(np.log2(seq))) + 1
        _, jh = lax.fori_loop(0, n_steps, idx_body,
                              (jnp.full((1, tq), -1.0, F32), jnp.full((1, tq), float(seq - 1), F32)))
        jh = jnp.where(tie, jh, float(seq))

        def fix_body(cb, _):
            k0 = pl.multiple_of(cb * tc, tc)
            blk = s_sc[pl.ds(k0, tc), :]
            drop = jnp.logical_and(blk == lo, crow_f + k0.astype(F32) > jh)
            s_sc[pl.ds(k0, tc), :] = jnp.where(drop, -jnp.inf, blk)
            return 0

        lax.fori_loop(0, ncb, fix_body, 0)

    def att_scores(kb):
        return _dot(akv_ref[pl.ds(pl.multiple_of(kb * tk, tk), tk), :], aqt_ref[...])

    def att_update(kb, lg_ref, tail, state):
        k0 = pl.multiple_of(kb * tk, tk)
        bias = jnp.where(s_sc[pl.ds(k0, tk), :] >= lo, 0.0, MASKED)
        return _attend_update(lg_ref, bias, avt_ref[:, pl.ds(k0, tk)], state)

    state = _attend_pipelined(nkb, att_scores, att_update, lg_a, lg_b, _attend_init(nh, A_HEAD_DIM))
    o_t = _attend_finish(state, A_HEAD_DIM)
    for p in range(nh // 2):
        pair = jnp.concatenate([o_t[:, (2 * p) * tq:(2 * p + 1) * tq], o_t[:, (2 * p + 1) * tq:(2 * p + 2) * tq]], axis=0)
        out_ref[:, p * LANES:(p + 1) * LANES] = pair.T.astype(out_ref.dtype)


def _dsa(aqt, iqt, iwt, akv, avt, ikp):
    bsz, nqb, _, hq = aqt.shape
    nh = iwt.shape[2]
    seq = akv.shape[1]
    tq = ATT_TQ
    qblk = lambda b, j: (b, j, 0, 0)
    full = lambda b, j: (b, 0, 0)
    return pl.pallas_call(
        _dsa_kernel,
        grid=(bsz, nqb),
        in_specs=[pl.BlockSpec((None, None, LANES, hq), qblk), pl.BlockSpec((None, None, LANES, hq), qblk),
                  pl.BlockSpec((None, None, nh, tq), qblk),
                  pl.BlockSpec((None, seq, LANES), full), pl.BlockSpec((None, A_HEAD_DIM + ONES_ROWS, seq), full),
                  pl.BlockSpec((None, seq, LANES), full)],
        out_specs=pl.BlockSpec((None, tq, nh * A_HEAD_DIM), lambda b, j: (b, j, 0)),
        out_shape=jax.ShapeDtypeStruct((bsz, seq, nh * A_HEAD_DIM), BF16),
        scratch_shapes=[pltpu.VMEM((seq, tq), F32), pltpu.VMEM((ATT_TK, hq), F32), pltpu.VMEM((ATT_TK, hq), F32)],
        compiler_params=pltpu.CompilerParams(
            dimension_semantics=("arbitrary", "arbitrary"), vmem_limit_bytes=VMEM_LIMIT),
        name="dsa",
    )(aqt, iqt, iwt, akv, avt, ikp)


def _mla_kernel(qt_ref, ckv_ref, latt_ref, wuvt_ref, out_ref, lg_a, lg_b):
    nh = wuvt_ref.shape[0]
    tq, tk = ATT_TQ, ATT_TK
    j = pl.program_id(1)
    krow = lax.broadcasted_iota(jnp.int32, (tk, tq), 0)
    qcol = j * tq + lax.broadcasted_iota(jnp.int32, (1, tq), 1)

    def scores(kb):
        return _dot(ckv_ref[pl.ds(pl.multiple_of(kb * tk, tk), tk), :], qt_ref[...])

    def update(kb, lg_ref, tail, state):
        k0 = pl.multiple_of(kb * tk, tk)
        bias = None
        if tail:
            bias = jnp.where((k0 + krow) <= qcol, 0.0, MASKED)
        return _attend_update(lg_ref, bias, latt_ref[:, pl.ds(k0, tk)], state)

    state = _attend_pipelined((j * tq) // tk + 1, scores, update, lg_a, lg_b, _attend_init(nh, KV_LORA))
    o_lat_t = _attend_finish(state, KV_LORA).astype(BF16)
    for p in range(nh // 2):
        pair = jnp.concatenate([_dot(wuvt_ref[h], o_lat_t[:, h * tq:(h + 1) * tq]) for h in (2 * p, 2 * p + 1)], axis=0)
        out_ref[:, p * LANES:(p + 1) * LANES] = pair.T.astype(out_ref.dtype)


def _mla(qabst, ckv, latt, w_uv_t):
    bsz, nqb, dq, hq = qabst.shape
    nh = w_uv_t.shape[0]
    seq = ckv.shape[1]
    tq = ATT_TQ
    return pl.pallas_call(
        _mla_kernel,
        grid=(bsz, nqb),
        in_specs=[pl.BlockSpec((None, None, dq, hq), lambda b, j: (b, j, 0, 0)),
                  pl.BlockSpec((None, seq, dq), lambda b, j: (b, 0, 0)),
                  pl.BlockSpec((None, KV_LORA + ONES_ROWS, seq), lambda b, j: (b, 0, 0)),
                  pl.BlockSpec(w_uv_t.shape, lambda b, j: (0, 0, 0))],
        out_specs=pl.BlockSpec((None, tq, nh * B_V), lambda b, j: (b, j, 0)),
        out_shape=jax.ShapeDtypeStruct((bsz, seq, nh * B_V), BF16),
        scratch_shapes=[pltpu.VMEM((ATT_TK, hq), F32), pltpu.VMEM((ATT_TK, hq), F32)],
        compiler_params=pltpu.CompilerParams(
            dimension_semantics=("arbitrary", "arbitrary"), vmem_limit_bytes=VMEM_LIMIT),
        name="mla",
    )(qabst, ckv, latt, w_uv_t)


def _layer_norm(z, g, b):
    mu = jnp.mean(z, axis=1, keepdims=True)
    d = z - mu
    var = jnp.mean(d * d, axis=1, keepdims=True)
    return d * lax.rsqrt(var + LN_EPS) * g + b


def _split_bf16(v):
    hi = v.astype(BF16)
    return hi, (v - hi.astype(F32)).astype(BF16)


def _to_row_tiles(v, ref):
    for c in range(v.shape[1] // LANES):
        ref[:, c, :] = v[:, c * LANES:(c + 1) * LANES]


def _from_row_tiles(ref):
    return jnp.concatenate([ref[:, c, :] for c in range(ref.shape[1])], axis=1)


def _mix_kernel(x_ref, a_ref, b_ref, wout_ref, g_ref, beta_ref, wr_ref, br_ref,
                h_ref, ti_ref, tg_ref, rank_ref, cnt_ref, carry_sc, *, alpha):
    tm = x_ref.shape[0]
    half = a_ref.shape[1]
    i = pl.program_id(0)

    @pl.when(i == 0)
    def _():
        carry_sc[...] = jnp.zeros(carry_sc.shape, F32)

    mix = _dot(a_ref[...], wout_ref[0:half, :]) + _dot(b_ref[...], wout_ref[half:2 * half, :])
    h = _layer_norm(alpha * x_ref[...] + mix, g_ref[...], beta_ref[...])
    _to_row_tiles(h, h_ref)
    h_hi, h_lo = _split_bf16(h)
    w_hi, w_lo = _split_bf16(wr_ref[...])
    logits = _dot(h_hi, w_hi) + (_dot(h_lo, w_hi) + _dot(h_hi, w_lo)) + br_ref[...]
    lane = _lane_iota((tm, LANES))
    logits = jnp.where(lane < N_EXPERTS, logits, -jnp.inf)
    ids = jnp.zeros((tm, LANES), jnp.int32)
    vals, hots = [], []
    for k in range(TOP_K):
        m = jnp.max(logits, axis=1, keepdims=True)
        idx = jnp.min(jnp.where(logits == m, lane, LANES), axis=1, keepdims=True)
        hot = lane == idx
        ids = jnp.where(lane == k, idx, ids)
        logits = jnp.where(hot, -jnp.inf, logits)
        vals.append(m)
        hots.append(hot)
    es = [jnp.exp(v - vals[0]) for v in vals]
    den = (es[0] + es[1]) + (es[2] + es[3])
    ti_ref[...] = ids[:, :TOP_K]
    for k in range(TOP_K):
        tg_ref[:, k, :] = jnp.broadcast_to(es[k] / den, (tm, LANES))

    onehot = jnp.where(jnp.logical_or(jnp.logical_or(hots[0], hots[1]), jnp.logical_or(hots[2], hots[3])), 1.0, 0.0)
    r_i = lax.broadcasted_iota(jnp.int32, (tm, tm), 0)
    c_i = lax.broadcasted_iota(jnp.int32, (tm, tm), 1)
    before = _dot(jnp.where(c_i < r_i, 1.0, 0.0).astype(BF16), onehot.astype(BF16)) + carry_sc[...]
    ranks = jnp.zeros((tm, LANES), F32)
    for k in range(TOP_K):
        ranks = jnp.where(lane == k, jnp.sum(jnp.where(hots[k], before, 0.0), axis=1, keepdims=True), ranks)
    rank_ref[...] = ranks[:, :TOP_K].astype(jnp.int32)
    carry_sc[...] = carry_sc[...] + jnp.sum(onehot, axis=0, keepdims=True)
    cnt_ref[...] = carry_sc[...].astype(jnp.int32)


def _mix(x2, a_out, b_out, w_out, g, beta, w_router, b_router, alpha):
    n, d = x2.shape
    tm = MIX_TM
    half = a_out.shape[1]
    row = lambda i: (i, 0)
    const = lambda i: (0, 0)
    wr = jnp.pad(w_router, ((0, 0), (0, LANES - N_EXPERTS)))
    br = jnp.pad(b_router, (0, LANES - N_EXPERTS)).reshape(1, LANES)
    return pl.pallas_call(
        functools.partial(_mix_kernel, alpha=alpha),
        grid=(n // tm,),
        in_specs=[pl.BlockSpec((tm, d), row), pl.BlockSpec((tm, half), row), pl.BlockSpec((tm, half), row),
                  pl.BlockSpec(w_out.shape, const), pl.BlockSpec((1, d), const), pl.BlockSpec((1, d), const),
                  pl.BlockSpec(wr.shape, const), pl.BlockSpec((1, LANES), const)],
        out_specs=[pl.BlockSpec((tm, d // LANES, LANES), lambda i: (i, 0, 0)), pl.BlockSpec((tm, TOP_K), row),
                   pl.BlockSpec((tm, TOP_K, LANES), lambda i: (i, 0, 0)), pl.BlockSpec((tm, TOP_K), row),
                   pl.BlockSpec((1, LANES), const)],
        out_shape=[jax.ShapeDtypeStruct((n, d // LANES, LANES), F32), jax.ShapeDtypeStruct((n, TOP_K), jnp.int32),
                   jax.ShapeDtypeStruct((n, TOP_K, LANES), F32), jax.ShapeDtypeStruct((n, TOP_K), jnp.int32),
                   jax.ShapeDtypeStruct((1, LANES), jnp.int32)],
        scratch_shapes=[pltpu.VMEM((1, LANES), F32)],
        compiler_params=pltpu.CompilerParams(dimension_semantics=("arbitrary",), vmem_limit_bytes=VMEM_LIMIT),
        name="mix",
    )(x2, a_out, b_out, w_out.astype(BF16), g.reshape(1, d), beta.reshape(1, d), wr, br)


def _route(top_i, rank, counts, tm):
    n = top_i.shape[0]
    n_tiles = (n * TOP_K) // tm + N_EXPERTS
    counts = counts[0, :N_EXPERTS]
    tiles_per = (counts + tm - 1) // tm
    tile_end = jnp.cumsum(tiles_per)
    pstart = (tile_end - tiles_per) * tm
    pos = rank + jnp.sum(jnp.where(top_i[..., None] == jnp.arange(N_EXPERTS, dtype=jnp.int32), pstart, 0), axis=-1)
    tile_ids = jnp.arange(n_tiles, dtype=jnp.int32)
    n_valid = tile_end[-1]
    tile_expert = jnp.sum(tile_end[None, :] <= jnp.minimum(tile_ids, n_valid - 1)[:, None], axis=1).astype(jnp.int32)
    n_valid = n_valid[None].astype(jnp.int32)
    pad_start = jnp.concatenate([pstart + counts, n_valid]).astype(jnp.int32)
    return pos.astype(jnp.int32), tile_expert, n_valid, pad_start, n_tiles


def _dispatch_kernel(pad_ref, h_ref, pos_hbm, xs_hbm, pos_s, zbuf, sem_idx, sem_row):
    i = pl.program_id(0)
    tm = h_ref.shape[0]

    def pos_copy(t, slot):
        return pltpu.make_async_copy(pos_hbm.at[t], pos_s.at[pl.ds(slot * (tm * TOP_K), tm * TOP_K)], sem_idx.at[slot])

    @pl.when(i == 0)
    def _():
        zt = zbuf.shape[0]
        zbuf[...] = jnp.zeros(zbuf.shape, zbuf.dtype)
        for e in range(N_EXPERTS):
            cpe = pltpu.make_async_copy(zbuf, xs_hbm.at[pl.ds(pad_ref[e], zt)], sem_row)
            cpe.start()
            cpe.wait()

        def fill(t, _):
            cpz = pltpu.make_async_copy(zbuf, xs_hbm.at[pl.ds(pl.multiple_of(t * zt, zt), zt)], sem_row)
            cpz.start()
            cpz.wait()
            return 0

        lax.fori_loop(pad_ref[N_EXPERTS], xs_hbm.shape[0] // zt, fill, 0)
        pos_copy(0, 0).start()

    slot = i % 2
    pos_copy(i, slot).wait()

    @pl.when(i + 1 < pl.num_programs(0))
    def _():
        pos_copy(i + 1, 1 - slot).start()

    base = slot * (tm * TOP_K)

    def body(r, _):
        for s in range(TOP_K):
            pltpu.make_async_copy(h_ref.at[r], xs_hbm.at[pos_s[base + r * TOP_K + s]], sem_row).start(priority=s % 2)
        return 0

    lax.fori_loop(0, tm, body, 0, unroll=8)
    for s in range(TOP_K):
        pltpu.make_async_copy(h_ref, xs_hbm.at[pl.ds(0, tm)], sem_row).wait()


def _dispatch(h3, pos, pad_start, n_rows, zrows):
    n, nt8, _ = h3.shape
    tm = ROW_TM
    return pl.pallas_call(
        _dispatch_kernel,
        grid_spec=pltpu.PrefetchScalarGridSpec(
            num_scalar_prefetch=1,
            grid=(n // tm,),
            in_specs=[pl.BlockSpec((tm, nt8, LANES), lambda i, pad: (i, 0, 0)), pl.BlockSpec(memory_space=pl.ANY)],
            out_specs=pl.BlockSpec(memory_space=pl.ANY),
            scratch_shapes=[pltpu.SMEM((2 * tm * TOP_K,), jnp.int32), pltpu.VMEM((zrows, nt8, LANES), F32),
                            pltpu.SemaphoreType.DMA((2,)), pltpu.SemaphoreType.DMA]),
        out_shape=jax.ShapeDtypeStruct((n_rows + zrows, nt8, LANES), F32),
        compiler_params=pltpu.CompilerParams(dimension_semantics=("arbitrary",), vmem_limit_bytes=VMEM_LIMIT),
        name="dispatch",
    )(pad_start, h3, pos.reshape(n // tm, tm * TOP_K))


def _moe_kernel(te_ref, nv_ref, xs_hbm, wg_ref, bg_ref, wu_ref, bu_ref, wd_ref, bd_ref, ys_hbm,
                xbuf, ybuf, wg_bf, wu_bf, wd_bf, sem_in, sem_out):
    i = pl.program_id(0)
    n_valid = nv_ref[0]
    tm = ybuf.shape[0]
    nt8 = xs_hbm.shape[1]

    def in_copy(t, slot, c):
        return pltpu.make_async_copy(xs_hbm.at[pl.ds(t * tm, tm), c, :], xbuf.at[slot, :, pl.ds(c * LANES, LANES)],
                                     sem_in.at[slot])

    def out_copy(t, c):
        return pltpu.make_async_copy(ybuf.at[:, pl.ds(c * LANES, LANES)], ys_hbm.at[pl.ds(t * tm, tm), c, :], sem_out)

    def fetch(t, slot):
        for c in range(nt8):
            in_copy(t, slot, c).start()

    def store(t):
        for c in range(nt8):
            out_copy(t, c).start()

    def wait_store(t):
        for c in range(nt8):
            out_copy(t, c).wait()

    @pl.when(i == 0)
    def _():
        fetch(0, 0)

    @pl.when(i < n_valid)
    def _():
        slot = i % 2
        for c in range(nt8):
            in_copy(i, slot, c).wait()

        @pl.when(i + 1 < n_valid)
        def _():
            fetch(i + 1, 1 - slot)

        @pl.when(jnp.logical_or(i == 0, te_ref[i] != te_ref[jnp.maximum(i - 1, 0)]))
        def _():
            wg_bf[...] = wg_ref[0].astype(BF16)
            wu_bf[...] = wu_ref[0].astype(BF16)
            wd_bf[...] = wd_ref[0].astype(BF16)

        x = xbuf[slot].astype(BF16)
        g = jnp.minimum(_dot(x, wg_bf[...]) + bg_ref[0], SWIGLU_LIMIT)
        u = jnp.clip(_dot(x, wu_bf[...]) + bu_ref[0], -SWIGLU_LIMIT, SWIGLU_LIMIT)
        act = g * (1.0 / (1.0 + jnp.exp(-SWIGLU_ALPHA * g))) * (u + 1.0)
        y = _dot(act.astype(BF16), wd_bf[...]) + bd_ref[0]

        @pl.when(i > 0)
        def _():
            wait_store(i - 1)

        ybuf[...] = y
        store(i)

        @pl.when(i == n_valid - 1)
        def _():
            wait_store(i)

    @pl.when(i >= n_valid)
    def _():
        ybuf[...] = jnp.zeros(ybuf.shape, ybuf.dtype)
        store(i)
        wait_store(i)


def _moe(xs, tile_expert, n_valid, n_tiles, w_gate, b_gate, w_up, b_up, w_down, b_down):
    _, nt8, _ = xs.shape
    tm = MOE_TM
    ne, d, dff = w_gate.shape
    wmap = lambda i, te, nv: (te[i], 0, 0)
    any_spec = pl.BlockSpec(memory_space=pl.ANY)
    grid_spec = pltpu.PrefetchScalarGridSpec(
        num_scalar_prefetch=2,
        grid=(n_tiles,),
        in_specs=[any_spec,
                  pl.BlockSpec((1, d, dff), wmap), pl.BlockSpec((1, 1, dff), wmap),
                  pl.BlockSpec((1, d, dff), wmap), pl.BlockSpec((1, 1, dff), wmap),
                  pl.BlockSpec((1, dff, d), wmap), pl.BlockSpec((1, 1, d), wmap)],
        out_specs=any_spec,
        scratch_shapes=[pltpu.VMEM((2, tm, d), F32), pltpu.VMEM((tm, d), F32),
                        pltpu.VMEM((d, dff), BF16), pltpu.VMEM((d, dff), BF16), pltpu.VMEM((dff, d), BF16),
                        pltpu.SemaphoreType.DMA((2,)), pltpu.SemaphoreType.DMA])
    return pl.pallas_call(
        _moe_kernel,
        grid_spec=grid_spec,
        out_shape=jax.ShapeDtypeStruct((n_tiles * tm, nt8, LANES), F32),
        compiler_params=pltpu.CompilerParams(dimension_semantics=("arbitrary",), vmem_limit_bytes=VMEM_LIMIT),
        name="moe",
    )(tile_expert, n_valid, xs,
      w_gate, b_gate.reshape(ne, 1, dff), w_up, b_up.reshape(ne, 1, dff), w_down, b_down.reshape(ne, 1, d))


def _combine_kernel(h_ref, gate_ref, pos_hbm, ys_hbm, g_ref, beta_ref, o_hbm, pos_s, ybuf, obuf, sem_idx, sem_row,
                    sem_out, *, alpha):
    i = pl.program_id(0)
    n = pl.num_programs(0)
    tm = h_ref.shape[0]
    slot = i % 2

    def pos_copy(t, sl):
        return pltpu.make_async_copy(pos_hbm.at[t], pos_s.at[pl.ds(sl * (tm * TOP_K), tm * TOP_K)], sem_idx.at[sl])

    def gather(sl):
        base = sl * (tm * TOP_K)

        def body(r, _):
            for s in range(TOP_K):
                pltpu.make_async_copy(ys_hbm.at[pos_s[base + r * TOP_K + s]], ybuf.at[sl, s, r],
                                      sem_row.at[sl]).start(priority=s % 2)
            return 0

        lax.fori_loop(0, tm, body, 0, unroll=8)

    @pl.when(i == 0)
    def _():
        first = pos_copy(0, 0)
        first.start()
        first.wait()
        gather(0)

        @pl.when(n > 1)
        def _():
            pos_copy(1, 1).start()

    @pl.when(i + 1 < n)
    def _():
        pos_copy(i + 1, 1 - slot).wait()
        gather(1 - slot)

        @pl.when(i + 2 < n)
        def _():
            pos_copy(i + 2, slot).start()

    for s in range(TOP_K):
        pltpu.make_async_copy(ys_hbm.at[pl.ds(0, tm)], ybuf.at[slot, s], sem_row.at[slot]).wait()

    z = alpha * h_ref[...]
    for s in range(TOP_K):
        z = z + gate_ref[:, s:s + 1, :] * ybuf[slot, s]
    inv_d = 1.0 / (z.shape[1] * z.shape[2])
    mu = jnp.sum(z, axis=(1, 2), keepdims=True) * inv_d
    dz = z - mu
    var = jnp.sum(dz * dz, axis=(1, 2), keepdims=True) * inv_d
    out = dz * lax.rsqrt(var + LN_EPS) * g_ref[...] + beta_ref[...]

    def out_copy(t, c):
        return pltpu.make_async_copy(obuf.at[:, c, :], o_hbm.at[pl.ds(t * tm, tm), pl.ds(c * LANES, LANES)], sem_out)

    nt8 = obuf.shape[1]

    @pl.when(i > 0)
    def _():
        for c in range(nt8):
            out_copy(i - 1, c).wait()

    obuf[...] = out
    for c in range(nt8):
        out_copy(i, c).start()

    @pl.when(i == n - 1)
    def _():
        for c in range(nt8):
            out_copy(i, c).wait()


def _combine(h3, gates, pos, ys, g, beta, alpha):
    n, nt8, _ = h3.shape
    tm = ROW_TM
    tile = lambda i: (i, 0, 0)
    const = lambda i: (0, 0, 0)
    return pl.pallas_call(
        functools.partial(_combine_kernel, alpha=alpha),
        grid=(n // tm,),
        in_specs=[pl.BlockSpec((tm, nt8, LANES), tile), pl.BlockSpec((tm, TOP_K, LANES), tile),
                  pl.BlockSpec(memory_space=pl.ANY), pl.BlockSpec(memory_space=pl.ANY),
                  pl.BlockSpec((1, nt8, LANES), const), pl.BlockSpec((1, nt8, LANES), const)],
        out_specs=pl.BlockSpec(memory_space=pl.ANY),
        out_shape=jax.ShapeDtypeStruct((n, nt8 * LANES), F32),
        scratch_shapes=[pltpu.SMEM((2 * tm * TOP_K,), jnp.int32), pltpu.VMEM((2, TOP_K, tm, nt8, LANES), F32),
                        pltpu.VMEM((tm, nt8, LANES), F32),
                        pltpu.SemaphoreType.DMA((2,)), pltpu.SemaphoreType.DMA((2,)), pltpu.SemaphoreType.DMA],
        compiler_params=pltpu.CompilerParams(dimension_semantics=("arbitrary",), vmem_limit_bytes=VMEM_LIMIT),
        name="combine",
    )(h3, gates, pos.reshape(n // tm, tm * TOP_K), ys, g.reshape(1, nt8, LANES), beta.reshape(1, nt8, LANES))


def _layer(x, tabs, alpha, w_in, ik_g, ik_b, q_g, w_q_up, kv_g, w_kv_up, w_out, ln1_g, ln1_b,
           w_router, b_router, w_gate, b_gate, w_up, b_up, w_down, b_down, ln2_g, ln2_b):
    bsz, seq, d = x.shape
    w_t, w_n, wq_t, bd_uk_t, w_uv_t = _pack_weights(w_in, w_q_up, w_kv_up)
    aqt, iqt, iwt, akv, avt, ikp, qabst, ckv, latt = _proj(x, tabs, w_t, w_n, wq_t, bd_uk_t, ik_g, ik_b, q_g, kv_g)
    a_out = _dsa(aqt, iqt, iwt, akv, avt, ikp)
    b_out = _mla(qabst, ckv, latt, w_uv_t)
    n = bsz * seq
    h3, top_i, gates, rank, counts = _mix(x.reshape(n, d), a_out.reshape(n, -1), b_out.reshape(n, -1), w_out,
                                          ln1_g, ln1_b, w_router, b_router, alpha)
    pos, tile_expert, n_valid, pad_start, n_tiles = _route(top_i, rank, counts, MOE_TM)
    xs = _dispatch(h3, pos, pad_start, n_tiles * MOE_TM, MOE_TM)
    ys = _moe(xs, tile_expert, n_valid, n_tiles, w_gate, b_gate, w_up, b_up, w_down, b_down)
    return _combine(h3, gates, pos, ys, ln2_g, ln2_b, alpha).reshape(bsz, seq, d)


def kernel(x, positions, w_mix_in, idx_k_norm_g, idx_k_norm_b, mla_q_norm_g, mla_w_q_up, mla_kv_norm_g, mla_w_kv_up, w_mix_out, ln1_g, ln1_b, w_router, b_router, w_gate, b_gate, w_up, b_up, w_down, b_down, ln2_g, ln2_b):
    depth = w_mix_in.shape[0]
    alpha = float((2 * depth) ** 0.25)
    tabs = _rope_tables(positions)
    for l in range(depth):
        x = _layer(x, tabs, alpha, w_mix_in[l], idx_k_norm_g[l], idx_k_norm_b[l], mla_q_norm_g[l], mla_w_q_up[l],
                   mla_kv_norm_g[l], mla_w_kv_up[l], w_mix_out[l], ln1_g[l], ln1_b[l], w_router[l], b_router[l],
                   w_gate[l], b_gate[l], w_up[l], b_up[l], w_down[l], b_down[l], ln2_g[l], ln2_b[l])
    return x
```
